```python
import jax, jax.numpy as jnp
from jax import lax
import numpy as np

D_MODEL = 1024
BATCH = 8
SEQ = 4096
DEPTH = 1
DEC_BATCH = 32
DEC_SEQ = 2048
PAST_LEN = 128

HEAD_DIM = 64
N_Q_HEADS = 8
N_KV_HEADS = 2
Q_PER_KV = N_Q_HEADS // N_KV_HEADS
ATTN_WIDTH = N_Q_HEADS * HEAD_DIM
KV_WIDTH = N_KV_HEADS * HEAD_DIM
N_SGU_GROUPS = 8
SGU_GROUP_DIM = 64
SGU_WIDTH = N_SGU_GROUPS * SGU_GROUP_DIM
MIX_WIDTH = ATTN_WIDTH + SGU_WIDTH
IN_PROJ_WIDTH = ATTN_WIDTH + 2 * KV_WIDTH + 2 * SGU_WIDTH
WINDOW = 128
BLOCK = 128
CHUNK = 128
ROPE_THETA = 500000.0
ROPE_DIM = HEAD_DIM // 4
N_EXPERTS = 32
TOP_K = 4
D_FF = D_MODEL
SWIGLU_LIMIT = 7.0
SWIGLU_ALPHA = 1.702
MOE_BLOCK = 512
EPS = 1e-6

kernel_name = 'hymba_sgu_swa_moe_encoder'


def rms_norm(x, g):
    xf = x.astype(jnp.float32)
    y = xf * lax.rsqrt(jnp.mean(xf * xf, axis=-1, keepdims=True) + EPS)
    return (y * g.astype(jnp.float32)).astype(x.dtype)


def layer_norm(x, g, b):
    xf = x.astype(jnp.float32)
    mu = jnp.mean(xf, axis=-1, keepdims=True)
    xc = xf - mu
    y = xc * lax.rsqrt(jnp.mean(xc * xc, axis=-1, keepdims=True) + EPS)
    return (y * g.astype(jnp.float32) + b.astype(jnp.float32)).astype(x.dtype)


def partial_rotary(x, pos):
    half = ROPE_DIM // 2
    inv_freq = ROPE_THETA ** (-(jnp.arange(half, dtype=jnp.float32) * 2.0) / ROPE_DIM)
    ang = pos.astype(jnp.float32)[:, None] * inv_freq[None, :]
    cos = jnp.cos(ang)[:, None, :]
    sin = jnp.sin(ang)[:, None, :]
    xf = x.astype(jnp.float32)
    x1 = xf[..., :half]
    x2 = xf[..., half:ROPE_DIM]
    out = jnp.concatenate([x1 * cos - x2 * sin, x2 * cos + x1 * sin, xf[..., ROPE_DIM:]], axis=-1)
    return out.astype(x.dtype)


def banded_gqa_with_sink(q, k, v, sink):
    B, S = q.shape[0], q.shape[1]
    nb = S // BLOCK
    qb = q.reshape(B, nb, BLOCK, N_KV_HEADS, Q_PER_KV, HEAD_DIM)
    pad = ((0, 0), (BLOCK, BLOCK), (0, 0), (0, 0))
    kp = jnp.pad(k, pad).reshape(B, nb + 2, BLOCK, N_KV_HEADS, HEAD_DIM)
    vp = jnp.pad(v, pad).reshape(B, nb + 2, BLOCK, N_KV_HEADS, HEAD_DIM)
    kw = jnp.concatenate([kp[:, :-2], kp[:, 1:-1], kp[:, 2:]], axis=2)
    vw = jnp.concatenate([vp[:, :-2], vp[:, 1:-1], vp[:, 2:]], axis=2)
    s = jnp.einsum('bnqhgd,bnkhd->bnhgqk', qb, kw, preferred_element_type=jnp.float32) * (HEAD_DIM ** -0.5)
    qpos = jnp.arange(BLOCK)[:, None]
    krel = jnp.arange(3 * BLOCK)[None, :] - BLOCK
    band = jnp.abs(krel - qpos) <= WINDOW
    kabs = jnp.arange(nb)[:, None] * BLOCK + krel
    in_seq = (kabs >= 0) & (kabs < S)
    valid = band[None, :, :] & in_seq[:, None, :]
    s = jnp.where(valid[None, :, None, None, :, :], s, -jnp.inf)
    sink_f = sink.astype(jnp.float32).reshape(N_KV_HEADS, Q_PER_KV)[None, None, :, :, None, None]
    m = jnp.maximum(jnp.max(s, axis=-1, keepdims=True), sink_f)
    p = jnp.exp(s - m)
    denom = jnp.sum(p, axis=-1, keepdims=True) + jnp.exp(sink_f - m)
    p = (p / denom).astype(v.dtype)
    o = jnp.einsum('bnhgqk,bnkhd->bnqhgd', p, vw)
    return o.reshape(B, S, ATTN_WIDTH)


def spatial_gating(u, v, ln_g, ln_b, w_s, b_s):
    B, S = u.shape[0], u.shape[1]
    nc = S // CHUNK
    u = jax.nn.gelu(u)
    v = layer_norm(jax.nn.gelu(v), ln_g, ln_b)
    vc = v.reshape(B, nc, CHUNK, N_SGU_GROUPS, SGU_GROUP_DIM)
    mixed = jnp.einsum('gpt,bctgd->bcpgd', w_s, vc) + b_s.T[None, None, :, :, None]
    return u * mixed.reshape(B, S, SGU_WIDTH)


def clamped_swiglu(gate, up):
    gate = jnp.minimum(gate, SWIGLU_LIMIT)
    up = jnp.clip(up, -SWIGLU_LIMIT, SWIGLU_LIMIT)
    return (up + 1.0) * (gate * jax.nn.sigmoid(gate * SWIGLU_ALPHA))


def moe_ffn(x, w_router, b_router, w1, b1, w2, b2):
    B, S, D = x.shape
    T = B * S
    xt = x.reshape(T, D)
    logits = jnp.dot(xt, w_router, preferred_element_type=jnp.float32) + b_router.astype(jnp.float32)
    top_val, top_idx = lax.top_k(logits, TOP_K)
    gates = jax.nn.softmax(top_val, axis=-1)
    A = T * TOP_K
    e_flat = top_idx.reshape(A).astype(jnp.int32)
    g_flat = gates.reshape(A)
    order = jnp.argsort(e_flat)
    e_sorted = e_flat[order]
    counts = jnp.bincount(e_flat, length=N_EXPERTS)
    padded = (counts + MOE_BLOCK - 1) // MOE_BLOCK * MOE_BLOCK
    starts = jnp.cumsum(counts) - counts
    pends = jnp.cumsum(padded)
    pstarts = pends - padded
    dest = pstarts[e_sorted] + jnp.arange(A, dtype=jnp.int32) - starts[e_sorted]
    cap = (A + N_EXPERTS * MOE_BLOCK + MOE_BLOCK - 1) // MOE_BLOCK * MOE_BLOCK
    n_blk = cap // MOE_BLOCK
    tok_buf = jnp.full((cap,), T, jnp.int32).at[dest].set((order // TOP_K).astype(jnp.int32))
    gate_buf = jnp.zeros((cap,), jnp.float32).at[dest].set(g_flat[order])
    blk_e = jnp.minimum(jnp.searchsorted(pends, jnp.arange(n_blk) * MOE_BLOCK, side='right'), N_EXPERTS - 1)
    x_pad = jnp.concatenate([xt, jnp.zeros((1, D), xt.dtype)], axis=0)

    def expert_block(args):
        tok, gate, e = args
        xb = x_pad[tok]
        h = xb @ w1[e] + b1[e]
        act = clamped_swiglu(h[:, :D_FF], h[:, D_FF:])
        out = act @ w2[e] + b2[e]
        return out.astype(jnp.float32) * gate[:, None]

    outs = lax.map(expert_block, (tok_buf.reshape(n_blk, MOE_BLOCK), gate_buf.reshape(n_blk, MOE_BLOCK), blk_e))
    y = jnp.zeros((T + 1, D), jnp.float32).at[tok_buf].add(outs.reshape(cap, D))
    return y[:T].reshape(B, S, D).astype(x.dtype)


def encoder_layer(x, norm1_g, w_in, q_norm_g, k_norm_g, attn_sink, sgu_ln_g, sgu_ln_b, w_spatial, b_spatial,
                  attn_out_g, sgu_out_g, w_out, norm2_g, w_router, b_router, w_moe_in, b_moe_in, w_moe_out, b_moe_out):
    B, S = x.shape[0], x.shape[1]
    h = rms_norm(x, norm1_g)
    z = h @ w_in
    c1 = ATTN_WIDTH
    c2 = c1 + KV_WIDTH
    c3 = c2 + KV_WIDTH
    c4 = c3 + SGU_WIDTH
    q = z[..., :c1].reshape(B, S, N_Q_HEADS, HEAD_DIM)
    k = z[..., c1:c2].reshape(B, S, N_KV_HEADS, HEAD_DIM)
    v = z[..., c2:c3].reshape(B, S, N_KV_HEADS, HEAD_DIM)
    su = z[..., c3:c4]
    sv = z[..., c4:]
    pos = jnp.arange(S)
    q = partial_rotary(rms_norm(q, q_norm_g), pos)
    k = partial_rotary(rms_norm(k, k_norm_g), pos)
    a = banded_gqa_with_sink(q, k, v, attn_sink)
    g = spatial_gating(su, sv, sgu_ln_g, sgu_ln_b, w_spatial, b_spatial)
    mix = jnp.concatenate([rms_norm(a, attn_out_g), rms_norm(g, sgu_out_g)], axis=-1)
    x = x + mix @ w_out
    x = x + moe_ffn(rms_norm(x, norm2_g), w_router, b_router, w_moe_in, b_moe_in, w_moe_out, b_moe_out)
    return x


def setup_inputs(seed: int = 0) -> dict:
    key = jax.random.key(seed)
    ks = jax.random.split(key, 21)
    f32 = jnp.float32
    L = DEPTH

    def nrm(k, shape, scale):
        return jax.random.normal(k, shape, f32) * scale

    return {
        'x_prompt': nrm(ks[0], (BATCH, SEQ, D_MODEL), 1.0),
        'x_sample': nrm(ks[1], (DEC_BATCH, DEC_SEQ, D_MODEL), 1.0),
        'norm1_g': 1.0 + nrm(ks[2], (L, D_MODEL), 0.02),
        'w_in': nrm(ks[3], (L, D_MODEL, IN_PROJ_WIDTH), D_MODEL ** -0.5),
        'q_norm_g': 1.0 + nrm(ks[4], (L, HEAD_DIM), 0.02),
        'k_norm_g': 1.0 + nrm(ks[5], (L, HEAD_DIM), 0.02),
        'attn_sink': nrm(ks[6], (L, N_Q_HEADS), 0.5),
        'sgu_ln_g': 1.0 + nrm(ks[7], (L, SGU_WIDTH), 0.02),
        'sgu_ln_b': nrm(ks[8], (L, SGU_WIDTH), 0.02),
        'w_spatial': nrm(ks[9], (L, N_SGU_GROUPS, CHUNK, CHUNK), CHUNK ** -0.5),
        'b_spatial': 1.0 + nrm(ks[10], (L, N_SGU_GROUPS, CHUNK), 0.02),
        'attn_out_g': 1.0 + nrm(ks[11], (L, ATTN_WIDTH), 0.02),
        'sgu_out_g': 1.0 + nrm(ks[12], (L, SGU_WIDTH), 0.02),
        'w_out': nrm(ks[13], (L, MIX_WIDTH, D_MODEL), MIX_WIDTH ** -0.5),
        'norm2_g': 1.0 + nrm(ks[14], (L, D_MODEL), 0.02),
        'w_router': nrm(ks[15], (L, D_MODEL, N_EXPERTS), D_MODEL ** -0.5),
        'b_router': nrm(ks[16], (L, N_EXPERTS), 0.01),
        'w_moe_in': nrm(ks[17], (L, N_EXPERTS, D_MODEL, 2 * D_FF), D_MODEL ** -0.5),
        'b_moe_in': nrm(ks[18], (L, N_EXPERTS, 2 * D_FF), 0.01),
        'w_moe_out': nrm(ks[19], (L, N_EXPERTS, D_FF, D_MODEL), D_FF ** -0.5),
        'b_moe_out': nrm(ks[20], (L, N_EXPERTS, D_MODEL), 0.01),
    }


def reference(x_prompt, x_sample, norm1_g, w_in, q_norm_g, k_norm_g, attn_sink, sgu_ln_g, sgu_ln_b, w_spatial,
              b_spatial, attn_out_g, sgu_out_g, w_out, norm2_g, w_router, b_router, w_moe_in, b_moe_in,
              w_moe_out, b_moe_out):
    y_prompt = x_prompt
    y_sample = x_sample
    for l in range(DEPTH):
        params = (norm1_g[l], w_in[l], q_norm_g[l], k_norm_g[l], attn_sink[l], sgu_ln_g[l], sgu_ln_b[l],
                  w_spatial[l], b_spatial[l], attn_out_g[l], sgu_out_g[l], w_out[l], norm2_g[l], w_router[l],
                  b_router[l], w_moe_in[l], b_moe_in[l], w_moe_out[l], b_moe_out[l])
        y_prompt = encoder_layer(y_prompt, *params)
        y_sample = encoder_layer(y_sample, *params)
    return (y_prompt, y_sample)
```

```python
import functools

import jax
import jax.numpy as jnp
from jax import lax
from jax.experimental import pallas as pl
from jax.experimental.pallas import tpu as pltpu

D_MODEL = 1024
HEAD_DIM = 64
N_Q_HEADS = 8
N_KV_HEADS = 2
Q_PER_KV = N_Q_HEADS // N_KV_HEADS
ATTN_WIDTH = N_Q_HEADS * HEAD_DIM
KV_WIDTH = N_KV_HEADS * HEAD_DIM
QK_WIDTH = ATTN_WIDTH + KV_WIDTH
N_SGU_GROUPS = 8
SGU_GROUP_DIM = 64
SGU_WIDTH = N_SGU_GROUPS * SGU_GROUP_DIM
IN_PROJ_WIDTH = ATTN_WIDTH + 2 * KV_WIDTH + 2 * SGU_WIDTH
BLOCK = 128
ROPE_THETA = 500000.0
ROPE_DIM = HEAD_DIM // 4
N_EXPERTS = 32
TOP_K = 4
D_FF = D_MODEL
SWIGLU_LIMIT = 7.0
SWIGLU_ALPHA = 1.702
EPS = 1e-6

LANES = 128
IN_PROJ_ROWS = 512
MIXER_ROWS = 512
ROUTER_ROWS = 256
MOVE_ROWS = 256
EXPERT_ROWS = 512
VMEM_LIMIT_BYTES = 56 * 1024 * 1024

F32 = jnp.float32
BF16 = jnp.bfloat16


def _cparams(*semantics):
    return pltpu.CompilerParams(dimension_semantics=semantics, vmem_limit_bytes=VMEM_LIMIT_BYTES)


def _in_proj_kernel(x_ref, g1_ref, w_ref, qkg_ref, cos_ref, sina_ref, sinb_ref, seg_ref, lng_ref, lnb_ref,
                    qk_ref, v_ref, u_ref, vn_ref):
    x = x_ref[...]
    h = x * lax.rsqrt(jnp.mean(x * x, axis=-1, keepdims=True) + EPS) * g1_ref[...]
    z = jnp.dot(h.astype(BF16), w_ref[...], preferred_element_type=F32)

    qk = z[:, :QK_WIDTH]
    ss = jnp.dot((qk * qk).astype(BF16), seg_ref[...], preferred_element_type=F32)
    qkn = qk * lax.rsqrt(ss * (1.0 / HEAD_DIM) + EPS) * qkg_ref[...]
    cos, sina, sinb = cos_ref[...], sina_ref[...], sinb_ref[...]
    for c in range(QK_WIDTH // LANES):
        xc = qkn[:, c * LANES:(c + 1) * LANES]
        up = pltpu.roll(xc, LANES - ROPE_DIM // 2, axis=1)
        dn = pltpu.roll(xc, ROPE_DIM // 2, axis=1)
        rc = xc * cos + up * sina + dn * sinb
        if c < ATTN_WIDTH // LANES:
            rc = rc * (HEAD_DIM ** -0.5)
        qk_ref[:, c * LANES:(c + 1) * LANES] = rc.astype(BF16)

    v_ref[...] = z[:, QK_WIDTH:QK_WIDTH + KV_WIDTH].astype(BF16)
    su = z[:, QK_WIDTH + KV_WIDTH:QK_WIDTH + KV_WIDTH + SGU_WIDTH]
    sv = z[:, QK_WIDTH + KV_WIDTH + SGU_WIDTH:]
    u_ref[...] = jax.nn.gelu(su).astype(BF16)
    gv = jax.nn.gelu(sv)
    mu = jnp.mean(gv, axis=-1, keepdims=True)
    gc = gv - mu
    ln = gc * lax.rsqrt(jnp.mean(gc * gc, axis=-1, keepdims=True) + EPS) * lng_ref[...] + lnb_ref[...]
    vn_ref[...] = ln.astype(BF16)


def _rope_tables(seq):
    half = ROPE_DIM // 2
    inv_freq = ROPE_THETA ** (-(jnp.arange(half, dtype=F32) * 2.0) / ROPE_DIM)
    ang = jnp.arange(seq).astype(F32)[:, None] * inv_freq[None, :]
    cos, sin = jnp.cos(ang), jnp.sin(ang)
    j = jnp.arange(LANES) % HEAD_DIM
    f = j % half
    cos_t = jnp.where(j[None, :] < ROPE_DIM, cos[:, f], 1.0)
    sina_t = jnp.where(j[None, :] < half, -sin[:, f], 0.0)
    sinb_t = jnp.where((j[None, :] >= half) & (j[None, :] < ROPE_DIM), sin[:, f], 0.0)
    return cos_t.astype(F32), sina_t.astype(F32), sinb_t.astype(F32)


def _in_proj(x2d, seq, g1, w_in, qkg, tables, seg, lng, lnb):
    t = x2d.shape[0]
    rows = IN_PROJ_ROWS
    n_seq = seq // rows
    const = lambda i: (0, 0)
    tab = pl.BlockSpec((rows, LANES), lambda i: (i % n_seq, 0))
    return pl.pallas_call(
        _in_proj_kernel,
        grid=(t // rows,),
        in_specs=[
            pl.BlockSpec((rows, D_MODEL), lambda i: (i, 0)),
            pl.BlockSpec((1, D_MODEL), const),
            pl.BlockSpec((D_MODEL, IN_PROJ_WIDTH), const),
            pl.BlockSpec((1, QK_WIDTH), const),
            tab, tab, tab,
            pl.BlockSpec((QK_WIDTH, QK_WIDTH), const),
            pl.BlockSpec((1, SGU_WIDTH), const),
            pl.BlockSpec((1, SGU_WIDTH), const),
        ],
        out_specs=[
            pl.BlockSpec((rows, QK_WIDTH), lambda i: (i, 0)),
            pl.BlockSpec((rows, KV_WIDTH), lambda i: (i, 0)),
            pl.BlockSpec((rows, SGU_WIDTH), lambda i: (i, 0)),
            pl.BlockSpec((rows, SGU_WIDTH), lambda i: (i, 0)),
        ],
        out_shape=[
            jax.ShapeDtypeStruct((t, QK_WIDTH), BF16),
            jax.ShapeDtypeStruct((t, KV_WIDTH), BF16),
            jax.ShapeDtypeStruct((t, SGU_WIDTH), BF16),
            jax.ShapeDtypeStruct((t, SGU_WIDTH), BF16),
        ],
        compiler_params=_cparams("parallel"),
        name="in_proj",
    )(x2d, g1, w_in, qkg, *tables, seg, lng, lnb)


def _mixer_kernel(sink_ref, q_ref, kp_ref, kc_ref, kn_ref, vp_ref, vc_ref, vx_ref, u_ref, g_ref, x_ref,
                  ws_ref, bs_ref, ag_ref, sg_ref, wo_ref, o_ref, mix_ref):
    i = pl.program_id(1)
    n_i = pl.num_programs(1)
    n_sub = MIXER_ROWS // BLOCK
    kwin = jnp.concatenate([kp_ref[...], kc_ref[...], kn_ref[...]], axis=0)
    vwin = jnp.concatenate([vp_ref[...], vc_ref[...], vx_ref[...]], axis=0)

    srows = Q_PER_KV * BLOCK
    r = lax.broadcasted_iota(jnp.int32, (srows, 3 * BLOCK), 0) & (BLOCK - 1)
    c = lax.broadcasted_iota(jnp.int32, (srows, 3 * BLOCK), 1)
    band = (c >= r) & (c <= r + 2 * BLOCK)
    hrow = lax.broadcasted_iota(jnp.int32, (srows, 1), 0) // BLOCK

    for j in range(n_sub):
        valid = band
        if j == 0:
            valid = valid & ((c >= BLOCK) | (i > 0))
        if j == n_sub - 1:
            valid = valid & ((c < 2 * BLOCK) | (i < n_i - 1))
        qj = q_ref[j * BLOCK:(j + 1) * BLOCK, :]
        kj = kwin[j * BLOCK:(j + 3) * BLOCK, :]
        vj = vwin[j * BLOCK:(j + 3) * BLOCK, :]
        pieces = []
        for hk in range(N_KV_HEADS):
            qs = jnp.concatenate(
                [qj[:, (hk * Q_PER_KV + g) * HEAD_DIM:(hk * Q_PER_KV + g + 1) * HEAD_DIM] for g in range(Q_PER_KV)],
                axis=0)
            kh = kj[:, hk * HEAD_DIM:(hk + 1) * HEAD_DIM]
            vh = vj[:, hk * HEAD_DIM:(hk + 1) * HEAD_DIM]
            s = lax.dot_general(qs, kh, (((1,), (1,)), ((), ())), preferred_element_type=F32)
            s = jnp.where(valid, s, -jnp.inf)
            sink = jnp.zeros((srows, 1), F32)
            for g in range(Q_PER_KV):
                sink = jnp.where(hrow == g, sink_ref[hk * Q_PER_KV + g], sink)
            m = jnp.maximum(jnp.max(s, axis=-1, keepdims=True), sink)
            p = jnp.exp(s - m)
            denom = jnp.sum(p, axis=-1, keepdims=True) + jnp.exp(sink - m)
            o = jnp.dot(p.astype(BF16), vh, preferred_element_type=F32) / denom
            pieces += [o[g * BLOCK:(g + 1) * BLOCK, :] for g in range(Q_PER_KV)]
        a = jnp.concatenate(pieces, axis=-1)
        a = a * lax.rsqrt(jnp.mean(a * a, axis=-1, keepdims=True) + EPS) * ag_ref[...]

        vn = g_ref[j * BLOCK:(j + 1) * BLOCK, :]
        mixed = jnp.concatenate(
            [jnp.dot(ws_ref[g], vn[:, g * SGU_GROUP_DIM:(g + 1) * SGU_GROUP_DIM], preferred_element_type=F32)
             for g in range(N_SGU_GROUPS)], axis=-1) + bs_ref[...]
        gated = u_ref[j * BLOCK:(j + 1) * BLOCK, :].astype(F32) * mixed
        gated = gated * lax.rsqrt(jnp.mean(gated * gated, axis=-1, keepdims=True) + EPS) * sg_ref[...]
        mix_ref[j * BLOCK:(j + 1) * BLOCK, :] = jnp.concatenate([a, gated], axis=-1).astype(BF16)

    o_ref[...] = x_ref[...] + jnp.dot(mix_ref[...], wo_ref[...], preferred_element_type=F32)


def _mixer(batch, seq, sink, qk, v, u, vn, x2d, ws, bs, ag, sg, wo):
    rows = MIXER_ROWS
    n_i = seq // rows
    sub = rows // BLOCK
    n_blk = batch * seq // BLOCK
    const2 = lambda b, i, s: (0, 0)
    cur = lambda col: (lambda b, i, s: (b * n_i + i, col))
    prv = lambda col: (lambda b, i, s: (jnp.maximum((b * n_i + i) * sub - 1, 0), col))
    nxt = lambda col: (lambda b, i, s: (jnp.minimum((b * n_i + i + 1) * sub, n_blk - 1), col))
    kcol = ATTN_WIDTH // KV_WIDTH
    grid_spec = pltpu.PrefetchScalarGridSpec(
        num_scalar_prefetch=1,
        grid=(batch, n_i),
        in_specs=[
            pl.BlockSpec((rows, ATTN_WIDTH), cur(0)),
            pl.BlockSpec((BLOCK, KV_WIDTH), prv(kcol)),
            pl.BlockSpec((rows, KV_WIDTH), cur(kcol)),
            pl.BlockSpec((BLOCK, KV_WIDTH), nxt(kcol)),
            pl.BlockSpec((BLOCK, KV_WIDTH), prv(0)),
            pl.BlockSpec((rows, KV_WIDTH), cur(0)),
            pl.BlockSpec((BLOCK, KV_WIDTH), nxt(0)),
            pl.BlockSpec((rows, SGU_WIDTH), cur(0)),
            pl.BlockSpec((rows, SGU_WIDTH), cur(0)),
            pl.BlockSpec((rows, D_MODEL), cur(0)),
            pl.BlockSpec((N_SGU_GROUPS, BLOCK, BLOCK), lambda b, i, s: (0, 0, 0)),
            pl.BlockSpec((BLOCK, SGU_WIDTH), const2),
            pl.BlockSpec((1, ATTN_WIDTH), const2),
            pl.BlockSpec((1, SGU_WIDTH), const2),
            pl.BlockSpec((D_MODEL, D_MODEL), const2),
        ],
        out_specs=pl.BlockSpec((rows, D_MODEL), cur(0)),
        scratch_shapes=[pltpu.VMEM((rows, D_MODEL), BF16)],
    )
    return pl.pallas_call(
        _mixer_kernel,
        grid_spec=grid_spec,
        out_shape=jax.ShapeDtypeStruct((batch * seq, D_MODEL), F32),
        compiler_params=_cparams("parallel", "parallel"),
        name="mixer",
    )(sink, qk, qk, qk, qk, v, v, v, u, vn, x2d, ws, bs, ag, sg, wo)


def _mix_half(x, p):
    batch, seq, _ = x.shape
    x2d = x.reshape(batch * seq, D_MODEL)
    qk, v, u, vn = _in_proj(x2d, seq, p["g1"], p["w_in"], p["qkg"], _rope_tables(seq), p["seg"], p["lng"], p["lnb"])
    return _mixer(batch, seq, p["sink"], qk, v, u, vn, x2d, p["ws"], p["bs"], p["ag"], p["sg"], p["wo"])


def _prep_params(norm1_g, w_in, q_norm_g, k_norm_g, attn_sink, sgu_ln_g, sgu_ln_b, w_spatial, b_spatial,
                 attn_out_g, sgu_out_g, w_out):
    head = jnp.arange(QK_WIDTH) // HEAD_DIM
    return dict(
        g1=norm1_g.reshape(1, D_MODEL),
        w_in=w_in.astype(BF16),
        qkg=jnp.concatenate([jnp.tile(q_norm_g, N_Q_HEADS), jnp.tile(k_norm_g, N_KV_HEADS)]).reshape(1, QK_WIDTH),
        seg=(head[:, None] == head[None, :]).astype(BF16),
        lng=sgu_ln_g.reshape(1, SGU_WIDTH),
        lnb=sgu_ln_b.reshape(1, SGU_WIDTH),
        sink=attn_sink.astype(F32),
        ws=w_spatial.astype(BF16),
        bs=jnp.repeat(b_spatial.T, SGU_GROUP_DIM, axis=1),
        ag=attn_out_g.reshape(1, ATTN_WIDTH),
        sg=sgu_out_g.reshape(1, SGU_WIDTH),
        wo=w_out.astype(BF16),
    )


def _pack_bf16_pair(lo, hi):
    lo_b = lax.bitcast_convert_type(lo.astype(BF16).astype(F32), jnp.uint32) >> 16
    hi_b = lax.bitcast_convert_type(hi.astype(BF16).astype(F32), jnp.uint32) & jnp.uint32(0xFFFF0000)
    return hi_b | lo_b


def _unpack_bf16_pair(packed):
    lo = lax.bitcast_convert_type(packed << 16, F32).astype(BF16)
    hi = lax.bitcast_convert_type(packed & jnp.uint32(0xFFFF0000), F32).astype(BF16)
    return lo, hi


def _router_kernel(x_ref, g2_ref, wr_ref, br_ref, tri_ref, xn_ref, idx_ref, rank_ref, gate_ref, cnt_ref, run_ref):
    @pl.when(pl.program_id(0) == 0)
    def _():
        run_ref[...] = jnp.zeros_like(run_ref)

    x = x_ref[...]
    xn = x * lax.rsqrt(jnp.mean(x * x, axis=-1, keepdims=True) + EPS) * g2_ref[...]
    xn_ref[...] = _pack_bf16_pair(xn[:, :D_MODEL // 2], xn[:, D_MODEL // 2:])

    logits = lax.dot_general(wr_ref[...], xn, (((1,), (1,)), ((), ())), precision=lax.Precision.HIGHEST,
                             preferred_element_type=F32) + br_ref[...]
    rows = logits.shape[1]
    erow = lax.broadcasted_iota(jnp.int32, (N_EXPERTS, rows), 0)
    work = logits
    vals, sels = [], []
    for k in range(TOP_K):
        m = jnp.max(work, axis=0, keepdims=True)
        ik = jnp.min(jnp.where(work == m, erow, N_EXPERTS), axis=0, keepdims=True)
        sel = erow == ik
        idx_ref[k:k + 1, :] = ik
        vals.append(m)
        sels.append(sel)
        work = jnp.where(sel, -jnp.inf, work)

    exps = [jnp.exp(v - vals[0]) for v in vals]
    den = exps[0] + exps[1] + exps[2] + exps[3]
    gate_ref[...] = jnp.zeros_like(gate_ref)
    for k in range(TOP_K):
        gate_ref[k:k + 1, :] = exps[k] / den

    onehot = jnp.zeros((N_EXPERTS, rows), F32)
    for sel in sels:
        onehot = onehot + sel.astype(F32)
    before = jnp.dot(onehot.astype(BF16), tri_ref[...], preferred_element_type=F32) + run_ref[:, :1]
    for k in range(TOP_K):
        rank_ref[k:k + 1, :] = jnp.sum(jnp.where(sels[k], before, 0.0), axis=0, keepdims=True).astype(jnp.int32)
    run_ref[...] = run_ref[...] + jnp.sum(onehot, axis=1, keepdims=True)
    cnt_ref[...] = run_ref[...]


def _router(x2d, g2, wr_t, br, tri):
    t = x2d.shape[0]
    rows = ROUTER_ROWS
    const = lambda i: (0, 0)
    return pl.pallas_call(
        _router_kernel,
        grid=(t // rows,),
        in_specs=[
            pl.BlockSpec((rows, D_MODEL), lambda i: (i, 0)),
            pl.BlockSpec((1, D_MODEL), const),
            pl.BlockSpec((N_EXPERTS, D_MODEL), const),
            pl.BlockSpec((N_EXPERTS, 1), const),
            pl.BlockSpec((rows, rows), const),
        ],
        out_specs=[
            pl.BlockSpec((rows, D_MODEL // 2), lambda i: (i, 0)),
            pl.BlockSpec((TOP_K, rows), lambda i: (0, i)),
            pl.BlockSpec((TOP_K, rows), lambda i: (0, i)),
            pl.BlockSpec((2 * TOP_K, rows), lambda i: (0, i)),
            pl.BlockSpec((N_EXPERTS, LANES), const),
        ],
        out_shape=[
            jax.ShapeDtypeStruct((t, D_MODEL // 2), jnp.uint32),
            jax.ShapeDtypeStruct((TOP_K, t), jnp.int32),
            jax.ShapeDtypeStruct((TOP_K, t), jnp.int32),
            jax.ShapeDtypeStruct((2 * TOP_K, t), F32),
            jax.ShapeDtypeStruct((N_EXPERTS, LANES), F32),
        ],
        scratch_shapes=[pltpu.VMEM((N_EXPERTS, LANES), F32)],
        compiler_params=_cparams("arbitrary"),
        name="router",
    )(x2d, g2, wr_t, br, tri)


def _dest_kernel(pstart_ref, idx_ref, rank_ref, dest_ref):
    idx = idx_ref[...]
    dest = rank_ref[...]
    for e in range(N_EXPERTS):
        dest = dest + jnp.where(idx == e, pstart_ref[e], 0)
    dest_ref[...] = dest


def _dest(pstart, idx, rank):
    t = idx.shape[1]
    rows = MOVE_ROWS
    grid_spec = pltpu.PrefetchScalarGridSpec(
        num_scalar_prefetch=1,
        grid=(t // rows,),
        in_specs=[pl.BlockSpec((TOP_K, rows), lambda i, s: (0, i)), pl.BlockSpec((TOP_K, rows), lambda i, s: (0, i))],
        out_specs=pl.BlockSpec((None, TOP_K, rows), lambda i, s: (i, 0, 0)),
    )
    return pl.pallas_call(
        _dest_kernel,
        grid_spec=grid_spec,
        out_shape=jax.ShapeDtypeStruct((t // rows, TOP_K, rows), jnp.int32),
        compiler_params=_cparams("parallel"),
        name="dest",
    )(pstart, idx, rank)


def _dispatch_kernel(dest_ref, xn_ref, init_ref, sorted_ref, sem):
    del init_ref
    rows = xn_ref.shape[0]

    def copy(j, k):
        return pltpu.make_async_copy(xn_ref.at[pl.ds(j, 1), :], sorted_ref.at[pl.ds(dest_ref[k, j], 1), :], sem)

    def start(j, carry):
        for k in range(TOP_K):
            copy(j, k).start()
        return carry

    def wait(j, carry):
        for k in range(TOP_K):
            copy(j, k).wait()
        return carry

    lax.fori_loop(0, rows, start, 0, unroll=8)
    lax.fori_loop(0, rows, wait, 0, unroll=8)


def _dispatch(dest, xn, cap):
    t = xn.shape[0]
    rows = MOVE_ROWS
    init = jnp.zeros((cap, D_MODEL // 2), jnp.uint32)
    return pl.pallas_call(
        _dispatch_kernel,
        grid=(t // rows,),
        in_specs=[
            pl.BlockSpec((None, TOP_K, rows), lambda i: (i, 0, 0), memory_space=pltpu.SMEM),
            pl.BlockSpec((rows, D_MODEL // 2), lambda i: (i, 0)),
            pl.BlockSpec(memory_space=pl.ANY),
        ],
        out_specs=pl.BlockSpec(memory_space=pl.ANY),
        out_shape=jax.ShapeDtypeStruct((cap, D_MODEL // 2), jnp.uint32),
        scratch_shapes=[pltpu.SemaphoreType.DMA(())],
        input_output_aliases={2: 0},
        compiler_params=_cparams("arbitrary"),
        name="dispatch",
    )(dest, xn, init)


def _experts_kernel(blk_e_ref, blk_src_ref, n_used_ref, x_ref, w1_ref, b1_ref, w2_ref, b2_ref, o_ref):
    del blk_e_ref, blk_src_ref

    @pl.when(pl.program_id(0) < n_used_ref[0])
    def _():
        lo, hi = _unpack_bf16_pair(x_ref[...])
        half = D_MODEL // 2
        h = (jnp.dot(lo, w1_ref[:half, :], preferred_element_type=F32)
             + jnp.dot(hi, w1_ref[half:, :], preferred_element_type=F32) + b1_ref[...])
        gate = jnp.minimum(h[:, :D_FF], SWIGLU_LIMIT)
        up = jnp.clip(h[:, D_FF:], -SWIGLU_LIMIT, SWIGLU_LIMIT)
        act = (up + 1.0) * (gate * jax.nn.sigmoid(gate * SWIGLU_ALPHA))
        o_ref[...] = jnp.dot(act.astype(BF16), w2_ref[...], preferred_element_type=F32) + b2_ref[...]

    @pl.when(pl.program_id(0) >= n_used_ref[0])
    def _():
        o_ref[...] = jnp.zeros_like(o_ref)


def _experts(blk_e, blk_src, n_used, xs, w1, b1, w2, b2):
    cap = xs.shape[0]
    rows = EXPERT_ROWS
    grid_spec = pltpu.PrefetchScalarGridSpec(
        num_scalar_prefetch=3,
        grid=(cap // rows,),
        in_specs=[
            pl.BlockSpec((rows, D_MODEL // 2), lambda b, be, bs, nu: (bs[b], 0)),
            pl.BlockSpec((None, D_MODEL, 2 * D_FF), lambda b, be, bs, nu: (be[b], 0, 0)),
            pl.BlockSpec((None, 1, 2 * D_FF), lambda b, be, bs, nu: (be[b], 0, 0)),
            pl.BlockSpec((None, D_FF, D_MODEL), lambda b, be, bs, nu: (be[b], 0, 0)),
            pl.BlockSpec((None, 1, D_MODEL), lambda b, be, bs, nu: (be[b], 0, 0)),
        ],
        out_specs=pl.BlockSpec((rows, D_MODEL), lambda b, be, bs, nu: (b, 0)),
    )
    return pl.pallas_call(
        _experts_kernel,
        grid_spec=grid_spec,
        out_shape=jax.ShapeDtypeStruct((cap, D_MODEL), F32),
        compiler_params=_cparams("arbitrary"),
        name="experts",
    )(blk_e, blk_src, n_used, xs, w1, b1, w2, b2)


def _combine_kernel(dest_ref, gate_ref, x_ref, sorted_ref, y_ref, buf_ref, sem):
    rows = x_ref.shape[0]

    def copy(j, k):
        return pltpu.make_async_copy(sorted_ref.at[pl.ds(dest_ref[k, j], 1), :], buf_ref.at[k, pl.ds(j, 1), :], sem)

    def start(j, carry):
        for k in range(TOP_K):
            copy(j, k).start()
        return carry

    def wait(j, carry):
        for k in range(TOP_K):
            copy(j, k).wait()
        return carry

    lax.fori_loop(0, rows, start, 0, unroll=8)
    lax.fori_loop(0, rows, wait, 0, unroll=8)
    gate_t = gate_ref[...].T
    y = x_ref[...]
    for k in range(TOP_K):
        y = y + gate_t[:, k:k + 1] * buf_ref[k]
    y_ref[...] = y


def _combine(dest, gate, x2d, out_sorted):
    t = x2d.shape[0]
    rows = MOVE_ROWS
    return pl.pallas_call(
        _combine_kernel,
        grid=(t // rows,),
        in_specs=[
            pl.BlockSpec((None, TOP_K, rows), lambda i: (i, 0, 0), memory_space=pltpu.SMEM),
            pl.BlockSpec((2 * TOP_K, rows), lambda i: (0, i)),
            pl.BlockSpec((rows, D_MODEL), lambda i: (i, 0)),
            pl.BlockSpec(memory_space=pl.ANY),
        ],
        out_specs=pl.BlockSpec((rows, D_MODEL), lambda i: (i, 0)),
        out_shape=jax.ShapeDtypeStruct((t, D_MODEL), F32),
        scratch_shapes=[pltpu.VMEM((TOP_K, rows, D_MODEL), F32), pltpu.SemaphoreType.DMA(())],
        compiler_params=_cparams("arbitrary"),
        name="combine",
    )(dest, gate, x2d, out_sorted)


def _moe_half(x2d, m):
    t = x2d.shape[0]
    rows = EXPERT_ROWS
    cap = t * TOP_K + N_EXPERTS * rows
    n_blk = cap // rows
    xn, idx, rank, gate, cnt = _router(x2d, m["g2"], m["wr_t"], m["br"], m["tri"])

    counts = cnt[:, 0].astype(jnp.int32)
    padded = (counts + rows - 1) // rows * rows
    pends = jnp.cumsum(padded)
    pstart = pends - padded
    n_used = pends[-1:] // rows
    blk_src = jnp.minimum(jnp.arange(n_blk, dtype=jnp.int32), n_used - 1)
    blk_e = jnp.minimum(jnp.sum(pends[None, :] <= (blk_src * rows)[:, None], axis=1), N_EXPERTS - 1).astype(jnp.int32)

    dest = _dest(pstart.astype(jnp.int32), idx, rank)
    xs = _dispatch(dest, xn, cap)
    out_sorted = _experts(blk_e, blk_src, n_used.astype(jnp.int32), xs, m["w1"], m["b1"], m["w2"], m["b2"])
    return _combine(dest, gate, x2d, out_sorted)


def _prep_moe(norm2_g, w_router, b_router, w_moe_in, b_moe_in, w_moe_out, b_moe_out):
    r = jnp.arange(ROUTER_ROWS)
    return dict(
        g2=norm2_g.reshape(1, D_MODEL),
        wr_t=w_router.T,
        br=b_router.reshape(N_EXPERTS, 1),
        tri=(r[:, None] < r[None, :]).astype(BF16),
        w1=w_moe_in.astype(BF16),
        b1=b_moe_in.reshape(N_EXPERTS, 1, 2 * D_FF),
        w2=w_moe_out.astype(BF16),
        b2=b_moe_out.reshape(N_EXPERTS, 1, D_MODEL),
    )


def kernel(x_prompt, x_sample, norm1_g, w_in, q_norm_g, k_norm_g, attn_sink, sgu_ln_g, sgu_ln_b, w_spatial,
           b_spatial, attn_out_g, sgu_out_g, w_out, norm2_g, w_router, b_router, w_moe_in, b_moe_in, w_moe_out,
           b_moe_out):
    p = _prep_params(norm1_g[0], w_in[0], q_norm_g[0], k_norm_g[0], attn_sink[0], sgu_ln_g[0], sgu_ln_b[0],
                     w_spatial[0], b_spatial[0], attn_out_g[0], sgu_out_g[0], w_out[0])
    m = _prep_moe(norm2_g[0], w_router[0], b_router[0], w_moe_in[0], b_moe_in[0], w_moe_out[0], b_moe_out[0])
    outs = []
    for x in (x_prompt, x_sample):
        x2 = _mix_half(x, p)
        outs.append(_moe_half(x2, m).reshape(x.shape))
    return tuple(outs)
```

```python
import functools

import jax
import jax.numpy as jnp
from jax import lax
from jax.experimental import pallas as pl
from jax.experimental.pallas import tpu as pltpu
from jax.experimental.pallas import tpu_sc as plsc

D_MODEL = 1024
HEAD_DIM = 64
N_Q_HEADS = 8
N_KV_HEADS = 2
Q_PER_KV = N_Q_HEADS // N_KV_HEADS
ATTN_WIDTH = N_Q_HEADS * HEAD_DIM
KV_WIDTH = N_KV_HEADS * HEAD_DIM
QK_WIDTH = ATTN_WIDTH + KV_WIDTH
N_SGU_GROUPS = 8
SGU_GROUP_DIM = 64
SGU_WIDTH = N_SGU_GROUPS * SGU_GROUP_DIM
IN_PROJ_WIDTH = ATTN_WIDTH + 2 * KV_WIDTH + 2 * SGU_WIDTH
BLOCK = 128
ROPE_THETA = 500000.0
ROPE_DIM = HEAD_DIM // 4
N_EXPERTS = 32
TOP_K = 4
D_FF = D_MODEL
SWIGLU_LIMIT = 7.0
SWIGLU_ALPHA = 1.702
EPS = 1e-6

LANES = 128
IN_PROJ_ROWS = 512
MIXER_ROWS = 512
ROUTER_ROWS = 256
MOVE_ROWS = 256
COMBINE_ROWS = 512
EXPERT_ROWS = 512
VMEM_LIMIT_BYTES = 56 * 1024 * 1024

F32 = jnp.float32
BF16 = jnp.bfloat16


def _cparams(*semantics):
    return pltpu.CompilerParams(dimension_semantics=semantics, vmem_limit_bytes=VMEM_LIMIT_BYTES)


def _in_proj_kernel(x_ref, g1_ref, w_ref, qkg_ref, cos_ref, sina_ref, sinb_ref, seg_ref, lng_ref, lnb_ref,
                    qk_ref, v_ref, u_ref, vn_ref):
    x = x_ref[...]
    h = x * lax.rsqrt(jnp.mean(x * x, axis=-1, keepdims=True) + EPS) * g1_ref[...]
    z = jnp.dot(h.astype(BF16), w_ref[...], preferred_element_type=F32)

    qk = z[:, :QK_WIDTH]
    ss = jnp.dot((qk * qk).astype(BF16), seg_ref[...], preferred_element_type=F32)
    qkn = qk * lax.rsqrt(ss * (1.0 / HEAD_DIM) + EPS) * qkg_ref[...]
    cos, sina, sinb = cos_ref[...], sina_ref[...], sinb_ref[...]
    for c in range(QK_WIDTH // LANES):
        xc = qkn[:, c * LANES:(c + 1) * LANES]
        up = pltpu.roll(xc, LANES - ROPE_DIM // 2, axis=1)
        dn = pltpu.roll(xc, ROPE_DIM // 2, axis=1)
        rc = xc * cos + up * sina + dn * sinb
        if c < ATTN_WIDTH // LANES:
            rc = rc * (HEAD_DIM ** -0.5)
        qk_ref[:, c * LANES:(c + 1) * LANES] = rc.astype(BF16)

    v_ref[...] = z[:, QK_WIDTH:QK_WIDTH + KV_WIDTH].astype(BF16)
    su = z[:, QK_WIDTH + KV_WIDTH:QK_WIDTH + KV_WIDTH + SGU_WIDTH]
    sv = z[:, QK_WIDTH + KV_WIDTH + SGU_WIDTH:]
    u_ref[...] = jax.nn.gelu(su).astype(BF16)
    gv = jax.nn.gelu(sv)
    mu = jnp.mean(gv, axis=-1, keepdims=True)
    gc = gv - mu
    ln = gc * lax.rsqrt(jnp.mean(gc * gc, axis=-1, keepdims=True) + EPS) * lng_ref[...] + lnb_ref[...]
    vn_ref[...] = ln.astype(BF16)


def _rope_tables(seq):
    half = ROPE_DIM // 2
    inv_freq = ROPE_THETA ** (-(jnp.arange(half, dtype=F32) * 2.0) / ROPE_DIM)
    ang = jnp.arange(seq).astype(F32)[:, None] * inv_freq[None, :]
    cos, sin = jnp.cos(ang), jnp.sin(ang)
    j = jnp.arange(LANES) % HEAD_DIM
    f = j % half
    cos_t = jnp.where(j[None, :] < ROPE_DIM, cos[:, f], 1.0)
    sina_t = jnp.where(j[None, :] < half, -sin[:, f], 0.0)
    sinb_t = jnp.where((j[None, :] >= half) & (j[None, :] < ROPE_DIM), sin[:, f], 0.0)
    return cos_t.astype(F32), sina_t.astype(F32), sinb_t.astype(F32)


def _in_proj(x2d, seq, g1, w_in, qkg, tables, seg, lng, lnb):
    t = x2d.shape[0]
    rows = IN_PROJ_ROWS
    n_seq = seq // rows
    const = lambda i: (0, 0)
    tab = pl.BlockSpec((rows, LANES), lambda i: (i % n_seq, 0))
    return pl.pallas_call(
        _in_proj_kernel,
        grid=(t // rows,),
        in_specs=[
            pl.BlockSpec((rows, D_MODEL), lambda i: (i, 0)),
            pl.BlockSpec((1, D_MODEL), const),
            pl.BlockSpec((D_MODEL, IN_PROJ_WIDTH), const),
            pl.BlockSpec((1, QK_WIDTH), const),
            tab, tab, tab,
            pl.BlockSpec((QK_WIDTH, QK_WIDTH), const),
            pl.BlockSpec((1, SGU_WIDTH), const),
            pl.BlockSpec((1, SGU_WIDTH), const),
        ],
        out_specs=[
            pl.BlockSpec((rows, QK_WIDTH), lambda i: (i, 0)),
            pl.BlockSpec((rows, KV_WIDTH), lambda i: (i, 0)),
            pl.BlockSpec((rows, SGU_WIDTH), lambda i: (i, 0)),
            pl.BlockSpec((rows, SGU_WIDTH), lambda i: (i, 0)),
        ],
        out_shape=[
            jax.ShapeDtypeStruct((t, QK_WIDTH), BF16),
            jax.ShapeDtypeStruct((t, KV_WIDTH), BF16),
            jax.ShapeDtypeStruct((t, SGU_WIDTH), BF16),
            jax.ShapeDtypeStruct((t, SGU_WIDTH), BF16),
        ],
        compiler_params=_cparams("parallel"),
        name="in_proj",
    )(x2d, g1, w_in, qkg, *tables, seg, lng, lnb)


def _mixer_kernel(sink_ref, q_ref, kp_ref, kc_ref, kn_ref, vp_ref, vc_ref, vx_ref, u_ref, g_ref, x_ref,
                  ws_ref, bs_ref, ag_ref, sg_ref, wo_ref, o_ref, mix_ref):
    i = pl.program_id(1)
    n_i = pl.num_programs(1)
    n_sub = MIXER_ROWS // BLOCK
    kwin = jnp.concatenate([kp_ref[...], kc_ref[...], kn_ref[...]], axis=0)
    vwin = jnp.concatenate([vp_ref[...], vc_ref[...], vx_ref[...]], axis=0)

    srows = Q_PER_KV * BLOCK
    r = lax.broadcasted_iota(jnp.int32, (srows, 3 * BLOCK), 0) & (BLOCK - 1)
    c = lax.broadcasted_iota(jnp.int32, (srows, 3 * BLOCK), 1)
    band = (c >= r) & (c <= r + 2 * BLOCK)
    hrow = lax.broadcasted_iota(jnp.int32, (srows, 1), 0) // BLOCK

    for j in range(n_sub):
        valid = band
        if j == 0:
            valid = valid & ((c >= BLOCK) | (i > 0))
        if j == n_sub - 1:
            valid = valid & ((c < 2 * BLOCK) | (i < n_i - 1))
        qj = q_ref[j * BLOCK:(j + 1) * BLOCK, :]
        kj = kwin[j * BLOCK:(j + 3) * BLOCK, :]
        vj = vwin[j * BLOCK:(j + 3) * BLOCK, :]
        pieces = []
        for hk in range(N_KV_HEADS):
            qs = jnp.concatenate(
                [qj[:, (hk * Q_PER_KV + g) * HEAD_DIM:(hk * Q_PER_KV + g + 1) * HEAD_DIM] for g in range(Q_PER_KV)],
                axis=0)
            kh = kj[:, hk * HEAD_DIM:(hk + 1) * HEAD_DIM]
            vh = vj[:, hk * HEAD_DIM:(hk + 1) * HEAD_DIM]
            s = lax.dot_general(qs, kh, (((1,), (1,)), ((), ())), preferred_element_type=F32)
            s = jnp.where(valid, s, -jnp.inf)
            sink = jnp.zeros((srows, 1), F32)
            for g in range(Q_PER_KV):
                sink = jnp.where(hrow == g, sink_ref[hk * Q_PER_KV + g], sink)
            m = jnp.maximum(jnp.max(s, axis=-1, keepdims=True), sink)
            p = jnp.exp(s - m)
            denom = jnp.sum(p, axis=-1, keepdims=True) + jnp.exp(sink - m)
            o = jnp.dot(p.astype(BF16), vh, preferred_element_type=F32) / denom
            pieces += [o[g * BLOCK:(g + 1) * BLOCK, :] for g in range(Q_PER_KV)]
        a = jnp.concatenate(pieces, axis=-1)
        a = a * lax.rsqrt(jnp.mean(a * a, axis=-1, keepdims=True) + EPS) * ag_ref[...]

        vn = g_ref[j * BLOCK:(j + 1) * BLOCK, :]
        mixed = jnp.concatenate(
            [jnp.dot(ws_ref[g], vn[:, g * SGU_GROUP_DIM:(g + 1) * SGU_GROUP_DIM], preferred_element_type=F32)
             for g in range(N_SGU_GROUPS)], axis=-1) + bs_ref[...]
        gated = u_ref[j * BLOCK:(j + 1) * BLOCK, :].astype(F32) * mixed
        gated = gated * lax.rsqrt(jnp.mean(gated * gated, axis=-1, keepdims=True) + EPS) * sg_ref[...]
        mix_ref[j * BLOCK:(j + 1) * BLOCK, :] = jnp.concatenate([a, gated], axis=-1).astype(BF16)

    o_ref[...] = x_ref[...] + jnp.dot(mix_ref[...], wo_ref[...], preferred_element_type=F32)


def _mixer(batch, seq, sink, qk, v, u, vn, x2d, ws, bs, ag, sg, wo):
    rows = MIXER_ROWS
    n_i = seq // rows
    sub = rows // BLOCK
    n_blk = batch * seq // BLOCK
    const2 = lambda b, i, s: (0, 0)
    cur = lambda col: (lambda b, i, s: (b * n_i + i, col))
    prv = lambda col: (lambda b, i, s: (jnp.maximum((b * n_i + i) * sub - 1, 0), col))
    nxt = lambda col: (lambda b, i, s: (jnp.minimum((b * n_i + i + 1) * sub, n_blk - 1), col))
    kcol = ATTN_WIDTH // KV_WIDTH
    grid_spec = pltpu.PrefetchScalarGridSpec(
        num_scalar_prefetch=1,
        grid=(batch, n_i),
        in_specs=[
            pl.BlockSpec((rows, ATTN_WIDTH), cur(0)),
            pl.BlockSpec((BLOCK, KV_WIDTH), prv(kcol)),
            pl.BlockSpec((rows, KV_WIDTH), cur(kcol)),
            pl.BlockSpec((BLOCK, KV_WIDTH), nxt(kcol)),
            pl.BlockSpec((BLOCK, KV_WIDTH), prv(0)),
            pl.BlockSpec((rows, KV_WIDTH), cur(0)),
            pl.BlockSpec((BLOCK, KV_WIDTH), nxt(0)),
            pl.BlockSpec((rows, SGU_WIDTH), cur(0)),
            pl.BlockSpec((rows, SGU_WIDTH), cur(0)),
            pl.BlockSpec((rows, D_MODEL), cur(0)),
            pl.BlockSpec((N_SGU_GROUPS, BLOCK, BLOCK), lambda b, i, s: (0, 0, 0)),
            pl.BlockSpec((BLOCK, SGU_WIDTH), const2),
            pl.BlockSpec((1, ATTN_WIDTH), const2),
            pl.BlockSpec((1, SGU_WIDTH), const2),
            pl.BlockSpec((D_MODEL, D_MODEL), const2),
        ],
        out_specs=pl.BlockSpec((rows, D_MODEL), cur(0)),
        scratch_shapes=[pltpu.VMEM((rows, D_MODEL), BF16)],
    )
    return pl.pallas_call(
        _mixer_kernel,
        grid_spec=grid_spec,
        out_shape=jax.ShapeDtypeStruct((batch * seq, D_MODEL), F32),
        compiler_params=_cparams("parallel", "parallel"),
        name="mixer",
    )(sink, qk, qk, qk, qk, v, v, v, u, vn, x2d, ws, bs, ag, sg, wo)


def _mix_half(x, p):
    batch, seq, _ = x.shape
    x2d = x.reshape(batch * seq, D_MODEL)
    qk, v, u, vn = _in_proj(x2d, seq, p["g1"], p["w_in"], p["qkg"], _rope_tables(seq), p["seg"], p["lng"], p["lnb"])
    return _mixer(batch, seq, p["sink"], qk, v, u, vn, x2d, p["ws"], p["bs"], p["ag"], p["sg"], p["wo"])


def _prep_params(norm1_g, w_in, q_norm_g, k_norm_g, attn_sink, sgu_ln_g, sgu_ln_b, w_spatial, b_spatial,
                 attn_out_g, sgu_out_g, w_out):
    head = jnp.arange(QK_WIDTH) // HEAD_DIM
    return dict(
        g1=norm1_g.reshape(1, D_MODEL),
        w_in=w_in.astype(BF16),
        qkg=jnp.concatenate([jnp.tile(q_norm_g, N_Q_HEADS), jnp.tile(k_norm_g, N_KV_HEADS)]).reshape(1, QK_WIDTH),
        seg=(head[:, None] == head[None, :]).astype(BF16),
        lng=sgu_ln_g.reshape(1, SGU_WIDTH),
        lnb=sgu_ln_b.reshape(1, SGU_WIDTH),
        sink=attn_sink.astype(F32),
        ws=w_spatial.astype(BF16),
        bs=jnp.repeat(b_spatial.T, SGU_GROUP_DIM, axis=1),
        ag=attn_out_g.reshape(1, ATTN_WIDTH),
        sg=sgu_out_g.reshape(1, SGU_WIDTH),
        wo=w_out.astype(BF16),
    )


def _pack_bf16_pair(lo, hi):
    lo_b = lax.bitcast_convert_type(lo.astype(BF16).astype(F32), jnp.uint32) >> 16
    hi_b = lax.bitcast_convert_type(hi.astype(BF16).astype(F32), jnp.uint32) & jnp.uint32(0xFFFF0000)
    return hi_b | lo_b


def _unpack_bf16_pair(packed):
    lo = lax.bitcast_convert_type(packed << 16, F32).astype(BF16)
    hi = lax.bitcast_convert_type(packed & jnp.uint32(0xFFFF0000), F32).astype(BF16)
    return lo, hi


def _router_kernel(x_ref, g2_ref, wr_ref, br_ref, tri_ref, xn_ref, idx_ref, rank_ref, gate_ref, cnt_ref, run_ref):
    @pl.when(pl.program_id(0) == 0)
    def _():
        run_ref[...] = jnp.zeros_like(run_ref)

    x = x_ref[...]
    xn = x * lax.rsqrt(jnp.mean(x * x, axis=-1, keepdims=True) + EPS) * g2_ref[...]
    xn_ref[...] = _pack_bf16_pair(xn[:, :D_MODEL // 2], xn[:, D_MODEL // 2:])

    logits = lax.dot_general(wr_ref[...], xn, (((1,), (1,)), ((), ())), precision=lax.Precision.HIGHEST,
                             preferred_element_type=F32) + br_ref[...]
    rows = logits.shape[1]
    erow = lax.broadcasted_iota(jnp.int32, (N_EXPERTS, rows), 0)
    work = logits
    vals, sels = [], []
    for k in range(TOP_K):
        m = jnp.max(work, axis=0, keepdims=True)
        ik = jnp.min(jnp.where(work == m, erow, N_EXPERTS), axis=0, keepdims=True)
        sel = erow == ik
        idx_ref[k:k + 1, :] = ik
        vals.append(m)
        sels.append(sel)
        work = jnp.where(sel, -jnp.inf, work)

    exps = [jnp.exp(v - vals[0]) for v in vals]
    den = exps[0] + exps[1] + exps[2] + exps[3]
    gate_ref[...] = jnp.zeros_like(gate_ref)
    for k in range(TOP_K):
        gate_ref[k:k + 1, :] = exps[k] / den

    onehot = jnp.zeros((N_EXPERTS, rows), F32)
    for sel in sels:
        onehot = onehot + sel.astype(F32)
    before = jnp.dot(onehot.astype(BF16), tri_ref[...], preferred_element_type=F32) + run_ref[:, :1]
    for k in range(TOP_K):
        rank_ref[k:k + 1, :] = jnp.sum(jnp.where(sels[k], before, 0.0), axis=0, keepdims=True).astype(jnp.int32)
    run_ref[...] = run_ref[...] + jnp.sum(onehot, axis=1, keepdims=True)
    cnt_ref[...] = run_ref[...]


def _router(x2d, g2, wr_t, br, tri):
    t = x2d.shape[0]
    rows = ROUTER_ROWS
    const = lambda i: (0, 0)
    return pl.pallas_call(
        _router_kernel,
        grid=(t // rows,),
        in_specs=[
            pl.BlockSpec((rows, D_MODEL), lambda i: (i, 0)),
            pl.BlockSpec((1, D_MODEL), const),
            pl.BlockSpec((N_EXPERTS, D_MODEL), const),
            pl.BlockSpec((N_EXPERTS, 1), const),
            pl.BlockSpec((rows, rows), const),
        ],
        out_specs=[
            pl.BlockSpec((rows, D_MODEL // 2), lambda i: (i, 0)),
            pl.BlockSpec((TOP_K, rows), lambda i: (0, i)),
            pl.BlockSpec((TOP_K, rows), lambda i: (0, i)),
            pl.BlockSpec((2 * TOP_K, rows), lambda i: (0, i)),
            pl.BlockSpec((N_EXPERTS, LANES), const),
        ],
        out_shape=[
            jax.ShapeDtypeStruct((t, D_MODEL // 2), jnp.uint32),
            jax.ShapeDtypeStruct((TOP_K, t), jnp.int32),
            jax.ShapeDtypeStruct((TOP_K, t), jnp.int32),
            jax.ShapeDtypeStruct((2 * TOP_K, t), F32),
            jax.ShapeDtypeStruct((N_EXPERTS, LANES), F32),
        ],
        scratch_shapes=[pltpu.VMEM((N_EXPERTS, LANES), F32)],
        compiler_params=_cparams("arbitrary"),
        name="router",
    )(x2d, g2, wr_t, br, tri)


def _dest_kernel(pstart_ref, idx_ref, rank_ref, dest_ref, flat_ref):
    idx = idx_ref[...]
    dest = rank_ref[...]
    for e in range(N_EXPERTS):
        dest = dest + jnp.where(idx == e, pstart_ref[e], 0)
    dest_ref[...] = dest
    flat_ref[...] = dest


def _dest(pstart, idx, rank):
    t = idx.shape[1]
    rows = MOVE_ROWS
    grid_spec = pltpu.PrefetchScalarGridSpec(
        num_scalar_prefetch=1,
        grid=(t // rows,),
        in_specs=[pl.BlockSpec((TOP_K, rows), lambda i, s: (0, i)), pl.BlockSpec((TOP_K, rows), lambda i, s: (0, i))],
        out_specs=[pl.BlockSpec((None, TOP_K, rows), lambda i, s: (i, 0, 0)),
                   pl.BlockSpec((TOP_K, rows), lambda i, s: (0, i))],
    )
    return pl.pallas_call(
        _dest_kernel,
        grid_spec=grid_spec,
        out_shape=[jax.ShapeDtypeStruct((t // rows, TOP_K, rows), jnp.int32),
                   jax.ShapeDtypeStruct((TOP_K, t), jnp.int32)],
        compiler_params=_cparams("parallel"),
        name="dest",
    )(pstart, idx, rank)


def _dispatch_kernel(dest_ref, xn_ref, init_ref, sorted_ref, sem):
    del init_ref
    rows = xn_ref.shape[0]

    def copy(j, k):
        return pltpu.make_async_copy(xn_ref.at[pl.ds(j, 1), :], sorted_ref.at[pl.ds(dest_ref[k, j], 1), :], sem)

    def start(j, carry):
        for k in range(TOP_K):
            copy(j, k).start()
        return carry

    def wait(j, carry):
        for k in range(TOP_K):
            copy(j, k).wait()
        return carry

    lax.fori_loop(0, rows, start, 0, unroll=8)
    lax.fori_loop(0, rows, wait, 0, unroll=8)


def _dispatch(dest, xn, cap):
    t = xn.shape[0]
    rows = MOVE_ROWS
    init = jnp.zeros((cap, D_MODEL // 2), jnp.uint32)
    return pl.pallas_call(
        _dispatch_kernel,
        grid=(t // rows,),
        in_specs=[
            pl.BlockSpec((None, TOP_K, rows), lambda i: (i, 0, 0), memory_space=pltpu.SMEM),
            pl.BlockSpec((rows, D_MODEL // 2), lambda i: (i, 0)),
            pl.BlockSpec(memory_space=pl.ANY),
        ],
        out_specs=pl.BlockSpec(memory_space=pl.ANY),
        out_shape=jax.ShapeDtypeStruct((cap, D_MODEL // 2), jnp.uint32),
        scratch_shapes=[pltpu.SemaphoreType.DMA(())],
        input_output_aliases={2: 0},
        compiler_params=_cparams("arbitrary"),
        name="dispatch",
    )(dest, xn, init)


def _experts_kernel(blk_e_ref, blk_src_ref, n_used_ref, x_ref, w1_ref, b1_ref, w2_ref, b2_ref, o_ref):
    del blk_e_ref, blk_src_ref

    @pl.when(pl.program_id(0) < n_used_ref[0])
    def _():
        lo, hi = _unpack_bf16_pair(x_ref[...])
        half = D_MODEL // 2
        h = (jnp.dot(lo, w1_ref[:half, :], preferred_element_type=F32)
             + jnp.dot(hi, w1_ref[half:, :], preferred_element_type=F32) + b1_ref[...])
        gate = jnp.minimum(h[:, :D_FF], SWIGLU_LIMIT)
        up = jnp.clip(h[:, D_FF:], -SWIGLU_LIMIT, SWIGLU_LIMIT)
        act = (up + 1.0) * (gate * jax.nn.sigmoid(gate * SWIGLU_ALPHA))
        o = jnp.dot(act.astype(BF16), w2_ref[...], preferred_element_type=F32) + b2_ref[...]
        o_ref[...] = lax.bitcast_convert_type(_pack_bf16_pair(o[:, :half], o[:, half:]), jnp.int32)

    @pl.when(pl.program_id(0) >= n_used_ref[0])
    def _():
        o_ref[...] = jnp.zeros_like(o_ref)


def _experts(blk_e, blk_src, n_used, xs, w1, b1, w2, b2):
    cap = xs.shape[0]
    rows = EXPERT_ROWS
    grid_spec = pltpu.PrefetchScalarGridSpec(
        num_scalar_prefetch=3,
        grid=(cap // rows,),
        in_specs=[
            pl.BlockSpec((rows, D_MODEL // 2), lambda b, be, bs, nu: (bs[b], 0)),
            pl.BlockSpec((None, D_MODEL, 2 * D_FF), lambda b, be, bs, nu: (be[b], 0, 0)),
            pl.BlockSpec((None, 1, 2 * D_FF), lambda b, be, bs, nu: (be[b], 0, 0)),
            pl.BlockSpec((None, D_FF, D_MODEL), lambda b, be, bs, nu: (be[b], 0, 0)),
            pl.BlockSpec((None, 1, D_MODEL), lambda b, be, bs, nu: (be[b], 0, 0)),
        ],
        out_specs=pl.BlockSpec((rows, D_MODEL // 2), lambda b, be, bs, nu: (b, 0)),
    )
    return pl.pallas_call(
        _experts_kernel,
        grid_spec=grid_spec,
        out_shape=jax.ShapeDtypeStruct((cap, D_MODEL // 2), jnp.int32),
        compiler_params=_cparams("arbitrary"),
        name="experts",
    )(blk_e, blk_src, n_used, xs, w1, b1, w2, b2)


SC_CORES = 2
SC_SUBCORES = 16
SC_WINDOW = 128


def _sc_gather(table, idx):
    n = idx.shape[0]
    width = table.shape[1]
    workers = SC_CORES * SC_SUBCORES
    per_worker = n // workers
    assert per_worker * workers == n and per_worker % SC_WINDOW == 0
    mesh = plsc.VectorSubcoreMesh(core_axis_name="c", subcore_axis_name="s", num_cores=SC_CORES,
                                  num_subcores=SC_SUBCORES)

    @functools.partial(
        pl.kernel,
        mesh=mesh,
        out_type=jax.ShapeDtypeStruct((n, width), table.dtype),
        scratch_types=[
            pltpu.VMEM((SC_WINDOW,), jnp.int32),
            pltpu.VMEM((SC_WINDOW, width), table.dtype),
            pltpu.SemaphoreType.DMA,
        ],
        name="sc_gather",
    )
    def gather(table_hbm, idx_hbm, out_hbm, idx_v, rows_v, sem):
        base = (lax.axis_index("s") * SC_CORES + lax.axis_index("c")) * per_worker

        @pl.loop(0, per_worker // SC_WINDOW)
        def _(step):
            off = pl.multiple_of(base + step * SC_WINDOW, SC_WINDOW)
            pltpu.sync_copy(idx_hbm.at[pl.ds(off, SC_WINDOW)], idx_v)
            pltpu.async_copy(table_hbm.at[idx_v], rows_v, sem).wait()
            pltpu.sync_copy(rows_v, out_hbm.at[pl.ds(off, SC_WINDOW)])

    return gather(table, idx)


def _combine_kernel(gate_ref, x_ref, rows_ref, y_ref):
    gate_t = gate_ref[...].T
    half = D_MODEL // 2
    lo_sum = x_ref[:, :half]
    hi_sum = x_ref[:, half:]
    for k in range(TOP_K):
        packed = lax.bitcast_convert_type(rows_ref[k], jnp.uint32)
        g = gate_t[:, k:k + 1]
        lo_sum = lo_sum + g * lax.bitcast_convert_type(packed << 16, F32)
        hi_sum = hi_sum + g * lax.bitcast_convert_type(packed & jnp.uint32(0xFFFF0000), F32)
    y_ref[:, :half] = lo_sum
    y_ref[:, half:] = hi_sum


def _combine(gate, x2d, rows4):
    t = x2d.shape[0]
    rows = COMBINE_ROWS
    return pl.pallas_call(
        _combine_kernel,
        grid=(t // rows,),
        in_specs=[
            pl.BlockSpec((2 * TOP_K, rows), lambda i: (0, i)),
            pl.BlockSpec((rows, D_MODEL), lambda i: (i, 0)),
            pl.BlockSpec((TOP_K, rows, D_MODEL // 2), lambda i: (0, i, 0)),
        ],
        out_specs=pl.BlockSpec((rows, D_MODEL), lambda i: (i, 0)),
        out_shape=jax.ShapeDtypeStruct((t, D_MODEL), F32),
        compiler_params=_cparams("parallel"),
        name="combine",
    )(gate, x2d, rows4)


def _moe_half(x2d, m):
    t = x2d.shape[0]
    rows = EXPERT_ROWS
    cap = t * TOP_K + N_EXPERTS * rows
    n_blk = cap // rows
    xn, idx, rank, gate, cnt = _router(x2d, m["g2"], m["wr_t"], m["br"], m["tri"])

    counts = cnt[:, 0].astype(jnp.int32)
    padded = (counts + rows - 1) // rows * rows
    pends = jnp.cumsum(padded)
    pstart = pends - padded
    n_used = pends[-1:] // rows
    blk_src = jnp.minimum(jnp.arange(n_blk, dtype=jnp.int32), n_used - 1)
    blk_e = jnp.minimum(jnp.sum(pends[None, :] <= (blk_src * rows)[:, None], axis=1), N_EXPERTS - 1).astype(jnp.int32)

    dest, dest_flat = _dest(pstart.astype(jnp.int32), idx, rank)
    xs = _dispatch(dest, xn, cap)
    out_sorted = _experts(blk_e, blk_src, n_used.astype(jnp.int32), xs, m["w1"], m["b1"], m["w2"], m["b2"])
    rows4 = _sc_gather(out_sorted, dest_flat.reshape(TOP_K * t))
    return _combine(gate, x2d, rows4.reshape(TOP_K, t, D_MODEL // 2))


def _prep_moe(norm2_g, w_router, b_router, w_moe_in, b_moe_in, w_moe_out, b_moe_out):
    r = jnp.arange(ROUTER_ROWS)
    return dict(
        g2=norm2_g.reshape(1, D_MODEL),
        wr_t=w_router.T,
        br=b_router.reshape(N_EXPERTS, 1),
        tri=(r[:, None] < r[None, :]).astype(BF16),
        w1=w_moe_in.astype(BF16),
        b1=b_moe_in.reshape(N_EXPERTS, 1, 2 * D_FF),
        w2=w_moe_out.astype(BF16),
        b2=b_moe_out.reshape(N_EXPERTS, 1, D_MODEL),
    )


def kernel(x_prompt, x_sample, norm1_g, w_in, q_norm_g, k_norm_g, attn_sink, sgu_ln_g, sgu_ln_b, w_spatial,
           b_spatial, attn_out_g, sgu_out_g, w_out, norm2_g, w_router, b_router, w_moe_in, b_moe_in, w_moe_out,
           b_moe_out):
    p = _prep_params(norm1_g[0], w_in[0], q_norm_g[0], k_norm_g[0], attn_sink[0], sgu_ln_g[0], sgu_ln_b[0],
                     w_spatial[0], b_spatial[0], attn_out_g[0], sgu_out_g[0], w_out[0])
    m = _prep_moe(norm2_g[0], w_router[0], b_router[0], w_moe_in[0], b_moe_in[0], w_moe_out[0], b_moe_out[0])
    outs = []
    for x in (x_prompt, x_sample):
        x2 = _mix_half(x, p)
        outs.append(_moe_half(x2, m).reshape(x.shape))
    return tuple(outs)
```

```python
import functools

import jax
import jax.numpy as jnp
from jax import lax
from jax.experimental import pallas as pl
from jax.experimental.pallas import tpu as pltpu
from jax.experimental.pallas import tpu_sc as plsc

D_MODEL = 1024
HEAD_DIM = 64
N_Q_HEADS = 8
N_KV_HEADS = 2
Q_PER_KV = N_Q_HEADS // N_KV_HEADS
ATTN_WIDTH = N_Q_HEADS * HEAD_DIM
KV_WIDTH = N_KV_HEADS * HEAD_DIM
QK_WIDTH = ATTN_WIDTH + KV_WIDTH
N_SGU_GROUPS = 8
SGU_GROUP_DIM = 64
SGU_WIDTH = N_SGU_GROUPS * SGU_GROUP_DIM
IN_PROJ_WIDTH = ATTN_WIDTH + 2 * KV_WIDTH + 2 * SGU_WIDTH
BLOCK = 128
ROPE_THETA = 500000.0
ROPE_DIM = HEAD_DIM // 4
N_EXPERTS = 32
TOP_K = 4
D_FF = D_MODEL
SWIGLU_LIMIT = 7.0
SWIGLU_ALPHA = 1.702
EPS = 1e-6

LANES = 128
IN_PROJ_ROWS = 512
MIXER_ROWS = 512
ROUTER_ROWS = 256
DEST_ROWS = 2048
COMBINE_ROWS = 512
EXPERT_ROWS = 512
VMEM_LIMIT_BYTES = 56 * 1024 * 1024

F32 = jnp.float32
BF16 = jnp.bfloat16


def _cparams(*semantics):
    return pltpu.CompilerParams(dimension_semantics=semantics, vmem_limit_bytes=VMEM_LIMIT_BYTES)


def _in_proj_kernel(x_ref, g1_ref, w_ref, qkg_ref, cos_ref, sina_ref, sinb_ref, seg_ref, lng_ref, lnb_ref,
                    qk_ref, v_ref, u_ref, vn_ref):
    x = x_ref[...]
    h = x * lax.rsqrt(jnp.mean(x * x, axis=-1, keepdims=True) + EPS) * g1_ref[...]
    z = jnp.dot(h.astype(BF16), w_ref[...], preferred_element_type=F32)

    qk = z[:, :QK_WIDTH]
    ss = jnp.dot((qk * qk).astype(BF16), seg_ref[...], preferred_element_type=F32)
    qkn = qk * lax.rsqrt(ss * (1.0 / HEAD_DIM) + EPS) * qkg_ref[...]
    cos, sina, sinb = cos_ref[...], sina_ref[...], sinb_ref[...]
    for c in range(QK_WIDTH // LANES):
        xc = qkn[:, c * LANES:(c + 1) * LANES]
        up = pltpu.roll(xc, LANES - ROPE_DIM // 2, axis=1)
        dn = pltpu.roll(xc, ROPE_DIM // 2, axis=1)
        rc = xc * cos + up * sina + dn * sinb
        if c < ATTN_WIDTH // LANES:
            rc = rc * (HEAD_DIM ** -0.5)
        qk_ref[:, c * LANES:(c + 1) * LANES] = rc.astype(BF16)

    v_ref[...] = z[:, QK_WIDTH:QK_WIDTH + KV_WIDTH].astype(BF16)
    su = z[:, QK_WIDTH + KV_WIDTH:QK_WIDTH + KV_WIDTH + SGU_WIDTH]
    sv = z[:, QK_WIDTH + KV_WIDTH + SGU_WIDTH:]
    u_ref[...] = jax.nn.gelu(su).astype(BF16)
    gv = jax.nn.gelu(sv)
    mu = jnp.mean(gv, axis=-1, keepdims=True)
    gc = gv - mu
    ln = gc * lax.rsqrt(jnp.mean(gc * gc, axis=-1, keepdims=True) + EPS) * lng_ref[...] + lnb_ref[...]
    vn_ref[...] = ln.astype(BF16)


def _rope_tables(seq):
    half = ROPE_DIM // 2
    inv_freq = ROPE_THETA ** (-(jnp.arange(half, dtype=F32) * 2.0) / ROPE_DIM)
    ang = jnp.arange(seq).astype(F32)[:, None] * inv_freq[None, :]
    cos, sin = jnp.cos(ang), jnp.sin(ang)
    j = jnp.arange(LANES) % HEAD_DIM
    f = j % half
    cos_t = jnp.where(j[None, :] < ROPE_DIM, cos[:, f], 1.0)
    sina_t = jnp.where(j[None, :] < half, -sin[:, f], 0.0)
    sinb_t = jnp.where((j[None, :] >= half) & (j[None, :] < ROPE_DIM), sin[:, f], 0.0)
    return cos_t.astype(F32), sina_t.astype(F32), sinb_t.astype(F32)


def _in_proj(x2d, seq, g1, w_in, qkg, tables, seg, lng, lnb):
    t = x2d.shape[0]
    rows = IN_PROJ_ROWS
    n_seq = seq // rows
    const = lambda i: (0, 0)
    tab = pl.BlockSpec((rows, LANES), lambda i: (i % n_seq, 0))
    return pl.pallas_call(
        _in_proj_kernel,
        grid=(t // rows,),
        in_specs=[
            pl.BlockSpec((rows, D_MODEL), lambda i: (i, 0)),
            pl.BlockSpec((1, D_MODEL), const),
            pl.BlockSpec((D_MODEL, IN_PROJ_WIDTH), const),
            pl.BlockSpec((1, QK_WIDTH), const),
            tab, tab, tab,
            pl.BlockSpec((QK_WIDTH, QK_WIDTH), const),
            pl.BlockSpec((1, SGU_WIDTH), const),
            pl.BlockSpec((1, SGU_WIDTH), const),
        ],
        out_specs=[
            pl.BlockSpec((rows, QK_WIDTH), lambda i: (i, 0)),
            pl.BlockSpec((rows, KV_WIDTH), lambda i: (i, 0)),
            pl.BlockSpec((rows, SGU_WIDTH), lambda i: (i, 0)),
            pl.BlockSpec((rows, SGU_WIDTH), lambda i: (i, 0)),
        ],
        out_shape=[
            jax.ShapeDtypeStruct((t, QK_WIDTH), BF16),
            jax.ShapeDtypeStruct((t, KV_WIDTH), BF16),
            jax.ShapeDtypeStruct((t, SGU_WIDTH), BF16),
            jax.ShapeDtypeStruct((t, SGU_WIDTH), BF16),
        ],
        compiler_params=_cparams("parallel"),
        name="in_proj",
    )(x2d, g1, w_in, qkg, *tables, seg, lng, lnb)


def _mixer_kernel(sink_ref, q_ref, kp_ref, kc_ref, kn_ref, vp_ref, vc_ref, vx_ref, u_ref, g_ref, x_ref,
                  ws_ref, bs_ref, ag_ref, sg_ref, wo_ref, o_ref, mix_ref):
    i = pl.program_id(1)
    n_i = pl.num_programs(1)
    n_sub = MIXER_ROWS // BLOCK
    kwin = jnp.concatenate([kp_ref[...], kc_ref[...], kn_ref[...]], axis=0)
    vwin = jnp.concatenate([vp_ref[...], vc_ref[...], vx_ref[...]], axis=0)

    srows = Q_PER_KV * BLOCK
    r = lax.broadcasted_iota(jnp.int32, (srows, 3 * BLOCK), 0) & (BLOCK - 1)
    c = lax.broadcasted_iota(jnp.int32, (srows, 3 * BLOCK), 1)
    band = (c >= r) & (c <= r + 2 * BLOCK)
    hrow = lax.broadcasted_iota(jnp.int32, (srows, 1), 0) // BLOCK

    for j in range(n_sub):
        valid = band
        if j == 0:
            valid = valid & ((c >= BLOCK) | (i > 0))
        if j == n_sub - 1:
            valid = valid & ((c < 2 * BLOCK) | (i < n_i - 1))
        qj = q_ref[j * BLOCK:(j + 1) * BLOCK, :]
        kj = kwin[j * BLOCK:(j + 3) * BLOCK, :]
        vj = vwin[j * BLOCK:(j + 3) * BLOCK, :]
        pieces = []
        for hk in range(N_KV_HEADS):
            qs = jnp.concatenate(
                [qj[:, (hk * Q_PER_KV + g) * HEAD_DIM:(hk * Q_PER_KV + g + 1) * HEAD_DIM] for g in range(Q_PER_KV)],
                axis=0)
            kh = kj[:, hk * HEAD_DIM:(hk + 1) * HEAD_DIM]
            vh = vj[:, hk * HEAD_DIM:(hk + 1) * HEAD_DIM]
            s = lax.dot_general(qs, kh, (((1,), (1,)), ((), ())), preferred_element_type=F32)
            s = jnp.where(valid, s, -jnp.inf)
            sink = jnp.zeros((srows, 1), F32)
            for g in range(Q_PER_KV):
                sink = jnp.where(hrow == g, sink_ref[hk * Q_PER_KV + g], sink)
            m = jnp.maximum(jnp.max(s, axis=-1, keepdims=True), sink)
            p = jnp.exp(s - m)
            denom = jnp.sum(p, axis=-1, keepdims=True) + jnp.exp(sink - m)
            o = jnp.dot(p.astype(BF16), vh, preferred_element_type=F32) / denom
            pieces += [o[g * BLOCK:(g + 1) * BLOCK, :] for g in range(Q_PER_KV)]
        a = jnp.concatenate(pieces, axis=-1)
        a = a * lax.rsqrt(jnp.mean(a * a, axis=-1, keepdims=True) + EPS) * ag_ref[...]

        vn = g_ref[j * BLOCK:(j + 1) * BLOCK, :]
        mixed = jnp.concatenate(
            [jnp.dot(ws_ref[g], vn[:, g * SGU_GROUP_DIM:(g + 1) * SGU_GROUP_DIM], preferred_element_type=F32)
             for g in range(N_SGU_GROUPS)], axis=-1) + bs_ref[...]
        gated = u_ref[j * BLOCK:(j + 1) * BLOCK, :].astype(F32) * mixed
        gated = gated * lax.rsqrt(jnp.mean(gated * gated, axis=-1, keepdims=True) + EPS) * sg_ref[...]
        mix_ref[j * BLOCK:(j + 1) * BLOCK, :] = jnp.concatenate([a, gated], axis=-1).astype(BF16)

    o_ref[...] = x_ref[...] + jnp.dot(mix_ref[...], wo_ref[...], preferred_element_type=F32)


def _mixer(batch, seq, sink, qk, v, u, vn, x2d, ws, bs, ag, sg, wo):
    rows = MIXER_ROWS
    n_i = seq // rows
    sub = rows // BLOCK
    n_blk = batch * seq // BLOCK
    const2 = lambda b, i, s: (0, 0)
    cur = lambda col: (lambda b, i, s: (b * n_i + i, col))
    prv = lambda col: (lambda b, i, s: (jnp.maximum((b * n_i + i) * sub - 1, 0), col))
    nxt = lambda col: (lambda b, i, s: (jnp.minimum((b * n_i + i + 1) * sub, n_blk - 1), col))
    kcol = ATTN_WIDTH // KV_WIDTH
    grid_spec = pltpu.PrefetchScalarGridSpec(
        num_scalar_prefetch=1,
        grid=(batch, n_i),
        in_specs=[
            pl.BlockSpec((rows, ATTN_WIDTH), cur(0)),
            pl.BlockSpec((BLOCK, KV_WIDTH), prv(kcol)),
            pl.BlockSpec((rows, KV_WIDTH), cur(kcol)),
            pl.BlockSpec((BLOCK, KV_WIDTH), nxt(kcol)),
            pl.BlockSpec((BLOCK, KV_WIDTH), prv(0)),
            pl.BlockSpec((rows, KV_WIDTH), cur(0)),
            pl.BlockSpec((BLOCK, KV_WIDTH), nxt(0)),
            pl.BlockSpec((rows, SGU_WIDTH), cur(0)),
            pl.BlockSpec((rows, SGU_WIDTH), cur(0)),
            pl.BlockSpec((rows, D_MODEL), cur(0)),
            pl.BlockSpec((N_SGU_GROUPS, BLOCK, BLOCK), lambda b, i, s: (0, 0, 0)),
            pl.BlockSpec((BLOCK, SGU_WIDTH), const2),
            pl.BlockSpec((1, ATTN_WIDTH), const2),
            pl.BlockSpec((1, SGU_WIDTH), const2),
            pl.BlockSpec((D_MODEL, D_MODEL), const2),
        ],
        out_specs=pl.BlockSpec((rows, D_MODEL), cur(0)),
        scratch_shapes=[pltpu.VMEM((rows, D_MODEL), BF16)],
    )
    return pl.pallas_call(
        _mixer_kernel,
        grid_spec=grid_spec,
        out_shape=jax.ShapeDtypeStruct((batch * seq, D_MODEL), F32),
        compiler_params=_cparams("parallel", "parallel"),
        name="mixer",
    )(sink, qk, qk, qk, qk, v, v, v, u, vn, x2d, ws, bs, ag, sg, wo)


def _mix_half(x, p):
    batch, seq, _ = x.shape
    x2d = x.reshape(batch * seq, D_MODEL)
    qk, v, u, vn = _in_proj(x2d, seq, p["g1"], p["w_in"], p["qkg"], _rope_tables(seq), p["seg"], p["lng"], p["lnb"])
    return _mixer(batch, seq, p["sink"], qk, v, u, vn, x2d, p["ws"], p["bs"], p["ag"], p["sg"], p["wo"])


def _prep_params(norm1_g, w_in, q_norm_g, k_norm_g, attn_sink, sgu_ln_g, sgu_ln_b, w_spatial, b_spatial,
                 attn_out_g, sgu_out_g, w_out):
    head = jnp.arange(QK_WIDTH) // HEAD_DIM
    return dict(
        g1=norm1_g.reshape(1, D_MODEL),
        w_in=w_in.astype(BF16),
        qkg=jnp.concatenate([jnp.tile(q_norm_g, N_Q_HEADS), jnp.tile(k_norm_g, N_KV_HEADS)]).reshape(1, QK_WIDTH),
        seg=(head[:, None] == head[None, :]).astype(BF16),
        lng=sgu_ln_g.reshape(1, SGU_WIDTH),
        lnb=sgu_ln_b.reshape(1, SGU_WIDTH),
        sink=attn_sink.astype(F32),
        ws=w_spatial.astype(BF16),
        bs=jnp.repeat(b_spatial.T, SGU_GROUP_DIM, axis=1),
        ag=attn_out_g.reshape(1, ATTN_WIDTH),
        sg=sgu_out_g.reshape(1, SGU_WIDTH),
        wo=w_out.astype(BF16),
    )


def _pack_bf16_pair(lo, hi):
    lo_b = lax.bitcast_convert_type(lo.astype(BF16).astype(F32), jnp.uint32) >> 16
    hi_b = lax.bitcast_convert_type(hi.astype(BF16).astype(F32), jnp.uint32) & jnp.uint32(0xFFFF0000)
    return hi_b | lo_b


def _unpack_bf16_pair(packed):
    lo = lax.bitcast_convert_type(packed << 16, F32).astype(BF16)
    hi = lax.bitcast_convert_type(packed & jnp.uint32(0xFFFF0000), F32).astype(BF16)
    return lo, hi


def _router_kernel(x_ref, g2_ref, wr_ref, br_ref, tri_ref, xn_ref, idx_ref, rank_ref, gate_ref, cnt_ref, run_ref):
    @pl.when(pl.program_id(0) == 0)
    def _():
        run_ref[...] = jnp.zeros_like(run_ref)

    x = x_ref[...]
    xn = x * lax.rsqrt(jnp.mean(x * x, axis=-1, keepdims=True) + EPS) * g2_ref[...]
    xn_ref[...] = lax.bitcast_convert_type(_pack_bf16_pair(xn[:, :D_MODEL // 2], xn[:, D_MODEL // 2:]), jnp.int32)

    logits = lax.dot_general(wr_ref[...], xn, (((1,), (1,)), ((), ())), precision=lax.Precision.HIGHEST,
                             preferred_element_type=F32) + br_ref[...]
    rows = logits.shape[1]
    erow = lax.broadcasted_iota(jnp.int32, (N_EXPERTS, rows), 0)
    work = logits
    vals, sels = [], []
    for k in range(TOP_K):
        m = jnp.max(work, axis=0, keepdims=True)
        ik = jnp.min(jnp.where(work == m, erow, N_EXPERTS), axis=0, keepdims=True)
        sel = erow == ik
        idx_ref[k:k + 1, :] = ik
        vals.append(m)
        sels.append(sel)
        work = jnp.where(sel, -jnp.inf, work)

    exps = [jnp.exp(v - vals[0]) for v in vals]
    den = exps[0] + exps[1] + exps[2] + exps[3]
    gate_ref[...] = jnp.zeros_like(gate_ref)
    for k in range(TOP_K):
        gate_ref[k:k + 1, :] = exps[k] / den

    onehot = jnp.zeros((N_EXPERTS, rows), F32)
    for sel in sels:
        onehot = onehot + sel.astype(F32)
    before = jnp.dot(onehot.astype(BF16), tri_ref[...], preferred_element_type=F32) + run_ref[:, :1]
    for k in range(TOP_K):
        rank_ref[k:k + 1, :] = jnp.sum(jnp.where(sels[k], before, 0.0), axis=0, keepdims=True).astype(jnp.int32)
    run_ref[...] = run_ref[...] + jnp.sum(onehot, axis=1, keepdims=True)
    cnt_ref[...] = run_ref[...]


def _router(x2d, g2, wr_t, br, tri):
    t = x2d.shape[0]
    rows = ROUTER_ROWS
    const = lambda i: (0, 0)
    return pl.pallas_call(
        _router_kernel,
        grid=(t // rows,),
        in_specs=[
            pl.BlockSpec((rows, D_MODEL), lambda i: (i, 0)),
            pl.BlockSpec((1, D_MODEL), const),
            pl.BlockSpec((N_EXPERTS, D_MODEL), const),
            pl.BlockSpec((N_EXPERTS, 1), const),
            pl.BlockSpec((rows, rows), const),
        ],
        out_specs=[
            pl.BlockSpec((rows, D_MODEL // 2), lambda i: (i, 0)),
            pl.BlockSpec((TOP_K, rows), lambda i: (0, i)),
            pl.BlockSpec((TOP_K, rows), lambda i: (0, i)),
            pl.BlockSpec((2 * TOP_K, rows), lambda i: (0, i)),
            pl.BlockSpec((N_EXPERTS, LANES), const),
        ],
        out_shape=[
            jax.ShapeDtypeStruct((t, D_MODEL // 2), jnp.int32),
            jax.ShapeDtypeStruct((TOP_K, t), jnp.int32),
            jax.ShapeDtypeStruct((TOP_K, t), jnp.int32),
            jax.ShapeDtypeStruct((2 * TOP_K, t), F32),
            jax.ShapeDtypeStruct((N_EXPERTS, LANES), F32),
        ],
        scratch_shapes=[pltpu.VMEM((N_EXPERTS, LANES), F32)],
        compiler_params=_cparams("arbitrary"),
        name="router",
    )(x2d, g2, wr_t, br, tri)


def _dest_kernel(pstart_ref, idx_ref, rank_ref, dest_ref):
    idx = idx_ref[...]
    dest = rank_ref[...]
    for e in range(N_EXPERTS):
        dest = dest + jnp.where(idx == e, pstart_ref[e], 0)
    dest_ref[...] = dest


def _dest(pstart, idx, rank):
    t = idx.shape[1]
    rows = min(DEST_ROWS, t)
    blk =pl.BlockSpec((TOP_K, rows), lambda i, s: (0, i))
    grid_spec = pltpu.PrefetchScalarGridSpec(num_scalar_prefetch=1, grid=(t // rows,), in_specs=[blk, blk],
                                             out_specs=blk)
    return pl.pallas_call(
        _dest_kernel,
        grid_spec=grid_spec,
        out_shape=jax.ShapeDtypeStruct((TOP_K, t), jnp.int32),
        compiler_params=_cparams("parallel"),
        name="dest",
    )(pstart, idx, rank)


SC_CORES = 2
SC_SUBCORES = 16
SC_WORKERS = SC_CORES * SC_SUBCORES
SC_WINDOW = 128


def _sc_mesh():
    return plsc.VectorSubcoreMesh(core_axis_name="c", subcore_axis_name="s", num_cores=SC_CORES,
                                  num_subcores=SC_SUBCORES)


def _sc_worker():
    return lax.axis_index("s") * SC_CORES + lax.axis_index("c")


def _sc_scatter(rows, idx, cap):
    t, width = rows.shape
    n_idx = idx.shape[0]
    per_worker = t // SC_WORKERS
    assert per_worker * SC_WORKERS == t and per_worker % SC_WINDOW == 0
    idx_flat = idx.reshape(n_idx * t)

    @functools.partial(
        pl.kernel,
        mesh=_sc_mesh(),
        out_type=jax.ShapeDtypeStruct((cap, width), rows.dtype),
        scratch_types=[
            pltpu.VMEM((SC_WINDOW,), jnp.int32),
            pltpu.VMEM((SC_WINDOW, width), rows.dtype),
            pltpu.SemaphoreType.DMA,
        ],
        name="sc_scatter",
    )
    def scatter(rows_hbm, idx_hbm, out_hbm, idx_v, rows_v, sem):
        base = _sc_worker() * per_worker

        @pl.loop(0, per_worker // SC_WINDOW)
        def _(step):
            off = pl.multiple_of(base + step * SC_WINDOW, SC_WINDOW)
            pltpu.sync_copy(rows_hbm.at[pl.ds(off, SC_WINDOW)], rows_v)
            for k in range(n_idx):
                pltpu.sync_copy(idx_hbm.at[pl.ds(pl.multiple_of(k * t + off, SC_WINDOW), SC_WINDOW)], idx_v)
                pltpu.async_copy(rows_v, out_hbm.at[idx_v], sem).wait()

    return scatter(rows, idx_flat)


def _experts_kernel(blk_e_ref, blk_src_ref, blk_valid_ref, n_used_ref, x_ref, w1_ref, b1_ref, w2_ref, b2_ref, o_ref):
    del blk_e_ref, blk_src_ref
    b = pl.program_id(0)

    @pl.when(b < n_used_ref[0])
    def _():
        row = lax.broadcasted_iota(jnp.int32, x_ref.shape, 0)
        x = jnp.where(row < blk_valid_ref[b], x_ref[...], 0)
        lo, hi = _unpack_bf16_pair(lax.bitcast_convert_type(x, jnp.uint32))
        half = D_MODEL // 2
        h = (jnp.dot(lo, w1_ref[:half, :], preferred_element_type=F32)
             + jnp.dot(hi, w1_ref[half:, :], preferred_element_type=F32) + b1_ref[...])
        gate = jnp.minimum(h[:, :D_FF], SWIGLU_LIMIT)
        up = jnp.clip(h[:, D_FF:], -SWIGLU_LIMIT, SWIGLU_LIMIT)
        act = (up + 1.0) * (gate * jax.nn.sigmoid(gate * SWIGLU_ALPHA))
        o = jnp.dot(act.astype(BF16), w2_ref[...], preferred_element_type=F32) + b2_ref[...]
        o_ref[...] = lax.bitcast_convert_type(_pack_bf16_pair(o[:, :half], o[:, half:]), jnp.int32)

    @pl.when(b >= n_used_ref[0])
    def _():
        o_ref[...] = jnp.zeros_like(o_ref)


def _experts(blk_e, blk_src, blk_valid, n_used, xs, w1, b1, w2, b2):
    cap = xs.shape[0]
    rows = EXPERT_ROWS
    grid_spec = pltpu.PrefetchScalarGridSpec(
        num_scalar_prefetch=4,
        grid=(cap // rows,),
        in_specs=[
            pl.BlockSpec((rows, D_MODEL // 2), lambda b, be, bs, bv, nu: (bs[b], 0)),
            pl.BlockSpec((None, D_MODEL, 2 * D_FF), lambda b, be, bs, bv, nu: (be[b], 0, 0)),
            pl.BlockSpec((None, 1, 2 * D_FF), lambda b, be, bs, bv, nu: (be[b], 0, 0)),
            pl.BlockSpec((None, D_FF, D_MODEL), lambda b, be, bs, bv, nu: (be[b], 0, 0)),
            pl.BlockSpec((None, 1, D_MODEL), lambda b, be, bs, bv, nu: (be[b], 0, 0)),
        ],
        out_specs=pl.BlockSpec((rows, D_MODEL // 2), lambda b, be, bs, bv, nu: (b, 0)),
    )
    return pl.pallas_call(
        _experts_kernel,
        grid_spec=grid_spec,
        out_shape=jax.ShapeDtypeStruct((cap, D_MODEL // 2), jnp.int32),
        compiler_params=_cparams("arbitrary"),
        name="experts",
    )(blk_e, blk_src, blk_valid, n_used, xs, w1, b1, w2, b2)


def _sc_gather(table, idx):
    n = idx.shape[0]
    width = table.shape[1]
    per_worker = n // SC_WORKERS
    assert per_worker * SC_WORKERS == n and per_worker % SC_WINDOW == 0

    @functools.partial(
        pl.kernel,
        mesh=_sc_mesh(),
        out_type=jax.ShapeDtypeStruct((n, width), table.dtype),
        scratch_types=[
            pltpu.VMEM((SC_WINDOW,), jnp.int32),
            pltpu.VMEM((SC_WINDOW, width), table.dtype),
            pltpu.SemaphoreType.DMA,
        ],
        name="sc_gather",
    )
    def gather(table_hbm, idx_hbm, out_hbm, idx_v, rows_v, sem):
        base = _sc_worker() * per_worker

        @pl.loop(0, per_worker // SC_WINDOW)
        def _(step):
            off = pl.multiple_of(base + step * SC_WINDOW, SC_WINDOW)
            pltpu.sync_copy(idx_hbm.at[pl.ds(off, SC_WINDOW)], idx_v)
            pltpu.async_copy(table_hbm.at[idx_v], rows_v, sem).wait()
            pltpu.sync_copy(rows_v, out_hbm.at[pl.ds(off, SC_WINDOW)])

    return gather(table, idx)


def _combine_kernel(gate_ref, x_ref, rows_ref, y_ref):
    gate_t = gate_ref[...].T
    half = D_MODEL // 2
    lo_sum = x_ref[:, :half]
    hi_sum = x_ref[:, half:]
    for k in range(TOP_K):
        packed = lax.bitcast_convert_type(rows_ref[k], jnp.uint32)
        g = gate_t[:, k:k + 1]
        lo_sum = lo_sum + g * lax.bitcast_convert_type(packed << 16, F32)
        hi_sum = hi_sum + g * lax.bitcast_convert_type(packed & jnp.uint32(0xFFFF0000), F32)
    y_ref[:, :half] = lo_sum
    y_ref[:, half:] = hi_sum


def _combine(gate, x2d, rows4):
    t = x2d.shape[0]
    rows = COMBINE_ROWS
    return pl.pallas_call(
        _combine_kernel,
        grid=(t // rows,),
        in_specs=[
            pl.BlockSpec((2 * TOP_K, rows), lambda i: (0, i)),
            pl.BlockSpec((rows, D_MODEL), lambda i: (i, 0)),
            pl.BlockSpec((TOP_K, rows, D_MODEL // 2), lambda i: (0, i, 0)),
        ],
        out_specs=pl.BlockSpec((rows, D_MODEL), lambda i: (i, 0)),
        out_shape=jax.ShapeDtypeStruct((t, D_MODEL), F32),
        compiler_params=_cparams("parallel"),
        name="combine",
    )(gate, x2d, rows4)


def _moe_half(x2d, m):
    t = x2d.shape[0]
    rows = EXPERT_ROWS
    cap = t * TOP_K + N_EXPERTS * rows
    n_blk = cap // rows
    xn, idx, rank, gate, cnt = _router(x2d, m["g2"], m["wr_t"], m["br"], m["tri"])

    counts = cnt[:, 0].astype(jnp.int32)
    padded = (counts + rows - 1) // rows * rows
    pends = jnp.cumsum(padded)
    pstart = pends - padded
    n_used = pends[-1:] // rows
    blk_src = jnp.minimum(jnp.arange(n_blk, dtype=jnp.int32), n_used - 1)
    blk_e = jnp.minimum(jnp.sum(pends[None, :] <= (blk_src * rows)[:, None], axis=1), N_EXPERTS - 1).astype(jnp.int32)
    blk_valid = jnp.clip(pstart[blk_e] + counts[blk_e] - blk_src * rows, 0, rows).astype(jnp.int32)

    dest = _dest(pstart.astype(jnp.int32), idx, rank)
    xs = _sc_scatter(xn, dest, cap)
    out_sorted = _experts(blk_e, blk_src, blk_valid, n_used.astype(jnp.int32), xs, m["w1"], m["b1"], m["w2"], m["b2"])
    rows4 = _sc_gather(out_sorted, dest.reshape(TOP_K * t))
    return _combine(gate, x2d, rows4.reshape(TOP_K, t, D_MODEL // 2))


def _prep_moe(norm2_g, w_router, b_router, w_moe_in, b_moe_in, w_moe_out, b_moe_out):
    r = jnp.arange(ROUTER_ROWS)
    return dict(
        g2=norm2_g.reshape(1, D_MODEL),
        wr_t=w_router.T,
        br=b_router.reshape(N_EXPERTS, 1),
        tri=(r[:, None] < r[None, :]).astype(BF16),
        w1=w_moe_in.astype(BF16),
        b1=b_moe_in.reshape(N_EXPERTS, 1, 2 * D_FF),
        w2=w_moe_out.astype(BF16),
        b2=b_moe_out.reshape(N_EXPERTS, 1, D_MODEL),
    )


def kernel(x_prompt, x_sample, norm1_g, w_in, q_norm_g, k_norm_g, attn_sink, sgu_ln_g, sgu_ln_b, w_spatial,
           b_spatial, attn_out_g, sgu_out_g, w_out, norm2_g, w_router, b_router, w_moe_in, b_moe_in, w_moe_out,
           b_moe_out):
    p = _prep_params(norm1_g[0], w_in[0], q_norm_g[0], k_norm_g[0], attn_sink[0], sgu_ln_g[0], sgu_ln_b[0],
                     w_spatial[0], b_spatial[0], attn_out_g[0], sgu_out_g[0], w_out[0])
    m = _prep_moe(norm2_g[0], w_router[0], b_router[0], w_moe_in[0], b_moe_in[0], w_moe_out[0], b_moe_out[0])
    outs = []
    for x in (x_prompt, x_sample):
        x2 = _mix_half(x, p)
        outs.append(_moe_half(x2, m).reshape(x.shape))
    return tuple(outs)
```

```python
import functools

import jax
import jax.numpy as jnp
from jax import lax
from jax.experimental import pallas as pl
from jax.experimental.pallas import tpu as pltpu
from jax.experimental.pallas import tpu_sc as plsc

D_MODEL = 1024
HEAD_DIM = 64
N_Q_HEADS = 8
N_KV_HEADS = 2
Q_PER_KV = N_Q_HEADS // N_KV_HEADS
ATTN_WIDTH = N_Q_HEADS * HEAD_DIM
KV_WIDTH = N_KV_HEADS * HEAD_DIM
QK_WIDTH = ATTN_WIDTH + KV_WIDTH
N_SGU_GROUPS = 8
SGU_GROUP_DIM = 64
SGU_WIDTH = N_SGU_GROUPS * SGU_GROUP_DIM
IN_PROJ_WIDTH = ATTN_WIDTH + 2 * KV_WIDTH + 2 * SGU_WIDTH
BLOCK = 128
ROPE_THETA = 500000.0
ROPE_DIM = HEAD_DIM // 4
N_EXPERTS = 32
TOP_K = 4
D_FF = D_MODEL
SWIGLU_LIMIT = 7.0
SWIGLU_ALPHA = 1.702
EPS = 1e-6

LANES = 128
IN_PROJ_ROWS = 512
MIXER_ROWS = 512
ATTN_STACK = 4
ROUTER_ROWS = 1024
DEST_ROWS = 2048
COMBINE_ROWS = 512
EXPERT_ROWS = 512
VMEM_LIMIT_BYTES = 56 * 1024 * 1024

F32 = jnp.float32
BF16 = jnp.bfloat16


def _cparams(*semantics):
    return pltpu.CompilerParams(dimension_semantics=semantics, vmem_limit_bytes=VMEM_LIMIT_BYTES)


def _in_proj_kernel(x_ref, g1_ref, w_ref, qkg_ref, cos_ref, sina_ref, sinb_ref, seg_ref, lng_ref, lnb_ref,
                    qk_ref, v_ref, u_ref, vn_ref):
    x = x_ref[...]
    h = x * lax.rsqrt(jnp.mean(x * x, axis=-1, keepdims=True) + EPS) * g1_ref[...]
    z = jnp.dot(h.astype(BF16), w_ref[...], preferred_element_type=F32)

    qk = z[:, :QK_WIDTH]
    ss = jnp.dot((qk * qk).astype(BF16), seg_ref[...], preferred_element_type=F32)
    qkn = qk * lax.rsqrt(ss * (1.0 / HEAD_DIM) + EPS) * qkg_ref[...]
    cos, sina, sinb = cos_ref[...], sina_ref[...], sinb_ref[...]
    for c in range(QK_WIDTH // LANES):
        xc = qkn[:, c * LANES:(c + 1) * LANES]
        up = pltpu.roll(xc, LANES - ROPE_DIM // 2, axis=1)
        dn = pltpu.roll(xc, ROPE_DIM // 2, axis=1)
        rc = xc * cos + up * sina + dn * sinb
        if c < ATTN_WIDTH // LANES:
            rc = rc * (HEAD_DIM ** -0.5)
        qk_ref[:, c * LANES:(c + 1) * LANES] = rc.astype(BF16)

    v_ref[...] = z[:, QK_WIDTH:QK_WIDTH + KV_WIDTH].astype(BF16)
    su = z[:, QK_WIDTH + KV_WIDTH:QK_WIDTH + KV_WIDTH + SGU_WIDTH]
    sv = z[:, QK_WIDTH + KV_WIDTH + SGU_WIDTH:]
    u_ref[...] = jax.nn.gelu(su).astype(BF16)
    gv = jax.nn.gelu(sv)
    mu = jnp.mean(gv, axis=-1, keepdims=True)
    gc = gv - mu
    ln = gc * lax.rsqrt(jnp.mean(gc * gc, axis=-1, keepdims=True) + EPS) * lng_ref[...] + lnb_ref[...]
    vn_ref[...] = ln.astype(BF16)


def _rope_tables(seq):
    half = ROPE_DIM // 2
    inv_freq = ROPE_THETA ** (-(jnp.arange(half, dtype=F32) * 2.0) / ROPE_DIM)
    ang = jnp.arange(seq).astype(F32)[:, None] * inv_freq[None, :]
    cos, sin = jnp.cos(ang), jnp.sin(ang)
    j = jnp.arange(LANES) % HEAD_DIM
    f = j % half
    cos_t = jnp.where(j[None, :] < ROPE_DIM, cos[:, f], 1.0)
    sina_t = jnp.where(j[None, :] < half, -sin[:, f], 0.0)
    sinb_t = jnp.where((j[None, :] >= half) & (j[None, :] < ROPE_DIM), sin[:, f], 0.0)
    return cos_t.astype(F32), sina_t.astype(F32), sinb_t.astype(F32)


def _in_proj(x2d, seq, g1, w_in, qkg, tables, seg, lng, lnb):
    t = x2d.shape[0]
    rows = IN_PROJ_ROWS
    n_seq = seq // rows
    const = lambda i: (0, 0)
    tab = pl.BlockSpec((rows, LANES), lambda i: (i % n_seq, 0))
    return pl.pallas_call(
        _in_proj_kernel,
        grid=(t // rows,),
        in_specs=[
            pl.BlockSpec((rows, D_MODEL), lambda i: (i, 0)),
            pl.BlockSpec((1, D_MODEL), const),
            pl.BlockSpec((D_MODEL, IN_PROJ_WIDTH), const),
            pl.BlockSpec((1, QK_WIDTH), const),
            tab, tab, tab,
            pl.BlockSpec((QK_WIDTH, QK_WIDTH), const),
            pl.BlockSpec((1, SGU_WIDTH), const),
            pl.BlockSpec((1, SGU_WIDTH), const),
        ],
        out_specs=[
            pl.BlockSpec((rows, QK_WIDTH), lambda i: (i, 0)),
            pl.BlockSpec((rows, KV_WIDTH), lambda i: (i, 0)),
            pl.BlockSpec((rows, SGU_WIDTH), lambda i: (i, 0)),
            pl.BlockSpec((rows, SGU_WIDTH), lambda i: (i, 0)),
        ],
        out_shape=[
            jax.ShapeDtypeStruct((t, QK_WIDTH), BF16),
            jax.ShapeDtypeStruct((t, KV_WIDTH), BF16),
            jax.ShapeDtypeStruct((t, SGU_WIDTH), BF16),
            jax.ShapeDtypeStruct((t, SGU_WIDTH), BF16),
        ],
        compiler_params=_cparams("parallel"),
        name="in_proj",
    )(x2d, g1, w_in, qkg, *tables, seg, lng, lnb)


def _mixer_kernel(sink_ref, q_ref, kp_ref, kc_ref, kn_ref, vp_ref, vc_ref, vx_ref, u_ref, g_ref, x_ref,
                  ws_ref, bs_ref, ag_ref, sg_ref, wo_ref, o_ref, mix_ref):
    i = pl.program_id(1)
    n_i = pl.num_programs(1)
    n_sub = MIXER_ROWS // BLOCK
    kwin = jnp.concatenate([kp_ref[...], kc_ref[...], kn_ref[...]], axis=0)
    vwin = jnp.concatenate([vp_ref[...], vc_ref[...], vx_ref[...]], axis=0)

    srows = ATTN_STACK * BLOCK
    r = lax.broadcasted_iota(jnp.int32, (srows, 3 * BLOCK), 0) & (BLOCK - 1)
    c = lax.broadcasted_iota(jnp.int32, (srows, 3 * BLOCK), 1)
    band = (c >= r) & (c <= r + 2 * BLOCK)
    hrow = lax.broadcasted_iota(jnp.int32, (srows, 1), 0) // BLOCK

    for j in range(n_sub):
        valid = band
        if j == 0:
            valid = valid & ((c >= BLOCK) | (i > 0))
        if j == n_sub - 1:
            valid = valid & ((c < 2 * BLOCK) | (i < n_i - 1))
        qj = q_ref[j * BLOCK:(j + 1) * BLOCK, :]
        kj = kwin[j * BLOCK:(j + 3) * BLOCK, :]
        vj = vwin[j * BLOCK:(j + 3) * BLOCK, :]
        pieces = []
        for h0 in range(0, N_Q_HEADS, ATTN_STACK):
            hk = h0 // Q_PER_KV
            qs = jnp.concatenate([qj[:, (h0 + g) * HEAD_DIM:(h0 + g + 1) * HEAD_DIM] for g in range(ATTN_STACK)],
                                 axis=0)
            kh = kj[:, hk * HEAD_DIM:(hk + 1) * HEAD_DIM]
            vh = vj[:, hk * HEAD_DIM:(hk + 1) * HEAD_DIM]
            s = lax.dot_general(qs, kh, (((1,), (1,)), ((), ())), preferred_element_type=F32)
            s = jnp.where(valid, s, -jnp.inf)
            sink = jnp.zeros((srows, 1), F32)
            for g in range(ATTN_STACK):
                sink = jnp.where(hrow == g, sink_ref[h0 + g], sink)
            m = jnp.maximum(jnp.max(s, axis=-1, keepdims=True), sink)
            p = jnp.exp(s - m)
            denom = jnp.sum(p, axis=-1, keepdims=True) + jnp.exp(sink - m)
            o = jnp.dot(p.astype(BF16), vh, preferred_element_type=F32) / denom
            pieces += [o[g * BLOCK:(g + 1) * BLOCK, :] for g in range(ATTN_STACK)]
        a = jnp.concatenate(pieces, axis=-1)
        a = a * lax.rsqrt(jnp.mean(a * a, axis=-1, keepdims=True) + EPS) * ag_ref[...]

        vn = g_ref[j * BLOCK:(j + 1) * BLOCK, :]
        mixed = jnp.concatenate(
            [jnp.dot(ws_ref[g], vn[:, g * SGU_GROUP_DIM:(g + 1) * SGU_GROUP_DIM], preferred_element_type=F32)
             for g in range(N_SGU_GROUPS)], axis=-1) + bs_ref[...]
        gated = u_ref[j * BLOCK:(j + 1) * BLOCK, :].astype(F32) * mixed
        gated = gated * lax.rsqrt(jnp.mean(gated * gated, axis=-1, keepdims=True) + EPS) * sg_ref[...]
        mix_ref[j * BLOCK:(j + 1) * BLOCK, :] = jnp.concatenate([a, gated], axis=-1).astype(BF16)

    o_ref[...] = x_ref[...] + jnp.dot(mix_ref[...], wo_ref[...], preferred_element_type=F32)


def _mixer(batch, seq, sink, qk, v, u, vn, x2d, ws, bs, ag, sg, wo):
    rows = MIXER_ROWS
    n_i = seq // rows
    sub = rows // BLOCK
    n_blk = batch * seq // BLOCK
    const2 = lambda b, i, s: (0, 0)
    cur = lambda col: (lambda b, i, s: (b * n_i + i, col))
    prv = lambda col: (lambda b, i, s: (jnp.maximum((b * n_i + i) * sub - 1, 0), col))
    nxt = lambda col: (lambda b, i, s: (jnp.minimum((b * n_i + i + 1) * sub, n_blk - 1), col))
    kcol = ATTN_WIDTH // KV_WIDTH
    grid_spec = pltpu.PrefetchScalarGridSpec(
        num_scalar_prefetch=1,
        grid=(batch, n_i),
        in_specs=[
            pl.BlockSpec((rows, ATTN_WIDTH), cur(0)),
            pl.BlockSpec((BLOCK, KV_WIDTH), prv(kcol)),
            pl.BlockSpec((rows, KV_WIDTH), cur(kcol)),
            pl.BlockSpec((BLOCK, KV_WIDTH), nxt(kcol)),
            pl.BlockSpec((BLOCK, KV_WIDTH), prv(0)),
            pl.BlockSpec((rows, KV_WIDTH), cur(0)),
            pl.BlockSpec((BLOCK, KV_WIDTH), nxt(0)),
            pl.BlockSpec((rows, SGU_WIDTH), cur(0)),
            pl.BlockSpec((rows, SGU_WIDTH), cur(0)),
            pl.BlockSpec((rows, D_MODEL), cur(0)),
            pl.BlockSpec((N_SGU_GROUPS, BLOCK, BLOCK), lambda b, i, s: (0, 0, 0)),
            pl.BlockSpec((BLOCK, SGU_WIDTH), const2),
            pl.BlockSpec((1, ATTN_WIDTH), const2),
            pl.BlockSpec((1, SGU_WIDTH), const2),
            pl.BlockSpec((D_MODEL, D_MODEL), const2),
        ],
        out_specs=pl.BlockSpec((rows, D_MODEL), cur(0)),
        scratch_shapes=[pltpu.VMEM((rows, D_MODEL), BF16)],
    )
    return pl.pallas_call(
        _mixer_kernel,
        grid_spec=grid_spec,
        out_shape=jax.ShapeDtypeStruct((batch * seq, D_MODEL), F32),
        compiler_params=_cparams("parallel", "parallel"),
        name="mixer",
    )(sink, qk, qk, qk, qk, v, v, v, u, vn, x2d, ws, bs, ag, sg, wo)


def _mix_half(x, p):
    batch, seq, _ = x.shape
    x2d = x.reshape(batch * seq, D_MODEL)
    qk, v, u, vn = _in_proj(x2d, seq, p["g1"], p["w_in"], p["qkg"], _rope_tables(seq), p["seg"], p["lng"], p["lnb"])
    return _mixer(batch, seq, p["sink"], qk, v, u, vn, x2d, p["ws"], p["bs"], p["ag"], p["sg"], p["wo"])


def _prep_params(norm1_g, w_in, q_norm_g, k_norm_g, attn_sink, sgu_ln_g, sgu_ln_b, w_spatial, b_spatial,
                 attn_out_g, sgu_out_g, w_out):
    head = jnp.arange(QK_WIDTH) // HEAD_DIM
    return dict(
        g1=norm1_g.reshape(1, D_MODEL),
        w_in=w_in.astype(BF16),
        qkg=jnp.concatenate([jnp.tile(q_norm_g, N_Q_HEADS), jnp.tile(k_norm_g, N_KV_HEADS)]).reshape(1, QK_WIDTH),
        seg=(head[:, None] == head[None, :]).astype(BF16),
        lng=sgu_ln_g.reshape(1, SGU_WIDTH),
        lnb=sgu_ln_b.reshape(1, SGU_WIDTH),
        sink=attn_sink.astype(F32),
        ws=w_spatial.astype(BF16),
        bs=jnp.repeat(b_spatial.T, SGU_GROUP_DIM, axis=1),
        ag=attn_out_g.reshape(1, ATTN_WIDTH),
        sg=sgu_out_g.reshape(1, SGU_WIDTH),
        wo=w_out.astype(BF16),
    )


def _pack_bf16_pair(lo, hi):
    lo_b = lax.bitcast_convert_type(lo.astype(BF16).astype(F32), jnp.uint32) >> 16
    hi_b = lax.bitcast_convert_type(hi.astype(BF16).astype(F32), jnp.uint32) & jnp.uint32(0xFFFF0000)
    return hi_b | lo_b


def _unpack_bf16_pair(packed):
    lo = lax.bitcast_convert_type(packed << 16, F32).astype(BF16)
    hi = lax.bitcast_convert_type(packed & jnp.uint32(0xFFFF0000), F32).astype(BF16)
    return lo, hi


def _router_kernel(x_ref, g2_ref, wh_ref, wl_ref, br_ref, tri_ref, xn_ref, idx_ref, rank_ref, gate_ref, cnt_ref,
                   run_ref):
    @pl.when(pl.program_id(0) == 0)
    def _():
        run_ref[...] = jnp.zeros_like(run_ref)

    x = x_ref[...]
    xn = x * lax.rsqrt(jnp.mean(x * x, axis=-1, keepdims=True) + EPS) * g2_ref[...]
    xn_ref[...] = lax.bitcast_convert_type(_pack_bf16_pair(xn[:, :D_MODEL // 2], xn[:, D_MODEL // 2:]), jnp.int32)

    xh = xn.astype(BF16)
    xl = (xn - xh.astype(F32)).astype(BF16)
    nt = (((1,), (1,)), ((), ()))
    logits = (lax.dot_general(wh_ref[...], xh, nt, preferred_element_type=F32)
              + lax.dot_general(wh_ref[...], xl, nt, preferred_element_type=F32)
              + lax.dot_general(wl_ref[...], xh, nt, preferred_element_type=F32)) + br_ref[...]
    rows = logits.shape[1]
    erow = lax.broadcasted_iota(jnp.int32, (N_EXPERTS, rows), 0)
    work = logits
    vals, sels = [], []
    for k in range(TOP_K):
        m = jnp.max(work, axis=0, keepdims=True)
        ik = jnp.min(jnp.where(work == m, erow, N_EXPERTS), axis=0, keepdims=True)
        sel = erow == ik
        idx_ref[k:k + 1, :] = ik
        vals.append(m)
        sels.append(sel)
        work = jnp.where(sel, -jnp.inf, work)

    exps = [jnp.exp(v - vals[0]) for v in vals]
    den = exps[0] + exps[1] + exps[2] + exps[3]
    gate_ref[...] = jnp.zeros_like(gate_ref)
    for k in range(TOP_K):
        gate_ref[k:k + 1, :] = exps[k] / den

    onehot = jnp.zeros((N_EXPERTS, rows), F32)
    for sel in sels:
        onehot = onehot + sel.astype(F32)
    before = jnp.dot(onehot.astype(BF16), tri_ref[...], preferred_element_type=F32) + run_ref[:, :1]
    for k in range(TOP_K):
        rank_ref[k:k + 1, :] = jnp.sum(jnp.where(sels[k], before, 0.0), axis=0, keepdims=True).astype(jnp.int32)
    run_ref[...] = run_ref[...] + jnp.sum(onehot, axis=1, keepdims=True)
    cnt_ref[...] = run_ref[...]


def _router(x2d, g2, wr_hi, wr_lo, br, tri):
    t = x2d.shape[0]
    rows = ROUTER_ROWS
    const = lambda i: (0, 0)
    return pl.pallas_call(
        _router_kernel,
        grid=(t // rows,),
        in_specs=[
            pl.BlockSpec((rows, D_MODEL), lambda i: (i, 0)),
            pl.BlockSpec((1, D_MODEL), const),
            pl.BlockSpec((N_EXPERTS, D_MODEL), const),
            pl.BlockSpec((N_EXPERTS, D_MODEL), const),
            pl.BlockSpec((N_EXPERTS, 1), const),
            pl.BlockSpec((rows, rows), const),
        ],
        out_specs=[
            pl.BlockSpec((rows, D_MODEL // 2), lambda i: (i, 0)),
            pl.BlockSpec((TOP_K, rows), lambda i: (0, i)),
            pl.BlockSpec((TOP_K, rows), lambda i: (0, i)),
            pl.BlockSpec((2 * TOP_K, rows), lambda i: (0, i)),
            pl.BlockSpec((N_EXPERTS, LANES), const),
        ],
        out_shape=[
            jax.ShapeDtypeStruct((t, D_MODEL // 2), jnp.int32),
            jax.ShapeDtypeStruct((TOP_K, t), jnp.int32),
            jax.ShapeDtypeStruct((TOP_K, t), jnp.int32),
            jax.ShapeDtypeStruct((2 * TOP_K, t), F32),
            jax.ShapeDtypeStruct((N_EXPERTS, LANES), F32),
        ],
        scratch_shapes=[pltpu.VMEM((N_EXPERTS, LANES), F32)],
        compiler_params=_cparams("arbitrary"),
        name="router",
    )(x2d, g2, wr_hi, wr_lo, br, tri)


def _dest_kernel(pstart_ref, idx_ref, rank_ref, dest_ref):
    idx = idx_ref[...]
    dest = rank_ref[...]
    for e in range(N_EXPERTS):
        dest = dest + jnp.where(idx == e, pstart_ref[e], 0)
    dest_ref[...] = dest


def _dest(pstart, idx, rank):
    t = idx.shape[1]
    rows = min(DEST_ROWS, t)
    blk =pl.BlockSpec((TOP_K, rows), lambda i, s: (0, i))
    grid_spec = pltpu.PrefetchScalarGridSpec(num_scalar_prefetch=1, grid=(t // rows,), in_specs=[blk, blk],
                                             out_specs=blk)
    return pl.pallas_call(
        _dest_kernel,
        grid_spec=grid_spec,
        out_shape=jax.ShapeDtypeStruct((TOP_K, t), jnp.int32),
        compiler_params=_cparams("parallel"),
        name="dest",
    )(pstart, idx, rank)


SC_CORES = 2
SC_SUBCORES = 16
SC_WORKERS = SC_CORES * SC_SUBCORES
SC_WINDOW = 128


def _sc_mesh():
    return plsc.VectorSubcoreMesh(core_axis_name="c", subcore_axis_name="s", num_cores=SC_CORES,
                                  num_subcores=SC_SUBCORES)


def _sc_worker():
    return lax.axis_index("s") * SC_CORES + lax.axis_index("c")


def _sc_scatter(rows, idx, cap):
    t, width = rows.shape
    n_idx = idx.shape[0]
    per_worker = t // SC_WORKERS
    assert per_worker * SC_WORKERS == t and per_worker % SC_WINDOW == 0
    idx_flat = idx.reshape(n_idx * t)

    @functools.partial(
        pl.kernel,
        mesh=_sc_mesh(),
        out_type=jax.ShapeDtypeStruct((cap, width), rows.dtype),
        scratch_types=[
            pltpu.VMEM((SC_WINDOW,), jnp.int32),
            pltpu.VMEM((SC_WINDOW, width), rows.dtype),
            pltpu.SemaphoreType.DMA,
        ],
        name="sc_scatter",
    )
    def scatter(rows_hbm, idx_hbm, out_hbm, idx_v, rows_v, sem):
        base = _sc_worker() * per_worker

        @pl.loop(0, per_worker // SC_WINDOW)
        def _(step):
            off = pl.multiple_of(base + step * SC_WINDOW, SC_WINDOW)
            pltpu.sync_copy(rows_hbm.at[pl.ds(off, SC_WINDOW)], rows_v)
            for k in range(n_idx):
                pltpu.sync_copy(idx_hbm.at[pl.ds(pl.multiple_of(k * t + off, SC_WINDOW), SC_WINDOW)], idx_v)
                pltpu.async_copy(rows_v, out_hbm.at[idx_v], sem).wait()

    return scatter(rows, idx_flat)


def _experts_kernel(blk_e_ref, blk_src_ref, blk_valid_ref, n_used_ref, x_ref, w1_ref, b1_ref, w2_ref, b2_ref, o_ref):
    del blk_e_ref, blk_src_ref
    b = pl.program_id(0)

    @pl.when(b < n_used_ref[0])
    def _():
        row = lax.broadcasted_iota(jnp.int32, x_ref.shape, 0)
        x = jnp.where(row < blk_valid_ref[b], x_ref[...], 0)
        lo, hi = _unpack_bf16_pair(lax.bitcast_convert_type(x, jnp.uint32))
        half = D_MODEL // 2
        h = (jnp.dot(lo, w1_ref[:half, :], preferred_element_type=F32)
             + jnp.dot(hi, w1_ref[half:, :], preferred_element_type=F32) + b1_ref[...])
        gate = jnp.minimum(h[:, :D_FF], SWIGLU_LIMIT)
        up = jnp.clip(h[:, D_FF:], -SWIGLU_LIMIT, SWIGLU_LIMIT)
        act = (up + 1.0) * (gate * jax.nn.sigmoid(gate * SWIGLU_ALPHA))
        o = jnp.dot(act.astype(BF16), w2_ref[...], preferred_element_type=F32) + b2_ref[...]
        o_ref[...] = lax.bitcast_convert_type(_pack_bf16_pair(o[:, :half], o[:, half:]), jnp.int32)

    @pl.when(b >= n_used_ref[0])
    def _():
        o_ref[...] = jnp.zeros_like(o_ref)


def _experts(blk_e, blk_src, blk_valid, n_used, xs, w1, b1, w2, b2):
    cap = xs.shape[0]
    rows = EXPERT_ROWS
    grid_spec = pltpu.PrefetchScalarGridSpec(
        num_scalar_prefetch=4,
        grid=(cap // rows,),
        in_specs=[
            pl.BlockSpec((rows, D_MODEL // 2), lambda b, be, bs, bv, nu: (bs[b], 0)),
            pl.BlockSpec((None, D_MODEL, 2 * D_FF), lambda b, be, bs, bv, nu: (be[b], 0, 0)),
            pl.BlockSpec((None, 1, 2 * D_FF), lambda b, be, bs, bv, nu: (be[b], 0, 0)),
            pl.BlockSpec((None, D_FF, D_MODEL), lambda b, be, bs, bv, nu: (be[b], 0, 0)),
            pl.BlockSpec((None, 1, D_MODEL), lambda b, be, bs, bv, nu: (be[b], 0, 0)),
        ],
        out_specs=pl.BlockSpec((rows, D_MODEL // 2), lambda b, be, bs, bv, nu: (b, 0)),
    )
    return pl.pallas_call(
        _experts_kernel,
        grid_spec=grid_spec,
        out_shape=jax.ShapeDtypeStruct((cap, D_MODEL // 2), jnp.int32),
        compiler_params=_cparams("arbitrary"),
        name="experts",
    )(blk_e, blk_src, blk_valid, n_used, xs, w1, b1, w2, b2)


def _sc_gather(table, idx):
    n = idx.shape[0]
    width = table.shape[1]
    per_worker = n // SC_WORKERS
    assert per_worker * SC_WORKERS == n and per_worker % SC_WINDOW == 0

    @functools.partial(
        pl.kernel,
        mesh=_sc_mesh(),
        out_type=jax.ShapeDtypeStruct((n, width), table.dtype),
        scratch_types=[
            pltpu.VMEM((SC_WINDOW,), jnp.int32),
            pltpu.VMEM((SC_WINDOW, width), table.dtype),
            pltpu.SemaphoreType.DMA,
        ],
        name="sc_gather",
    )
    def gather(table_hbm, idx_hbm, out_hbm, idx_v, rows_v, sem):
        base = _sc_worker() * per_worker

        @pl.loop(0, per_worker // SC_WINDOW)
        def _(step):
            off = pl.multiple_of(base + step * SC_WINDOW, SC_WINDOW)
            pltpu.sync_copy(idx_hbm.at[pl.ds(off, SC_WINDOW)], idx_v)
            pltpu.async_copy(table_hbm.at[idx_v], rows_v, sem).wait()
            pltpu.sync_copy(rows_v, out_hbm.at[pl.ds(off, SC_WINDOW)])

    return gather(table, idx)


def _combine_kernel(gate_ref, x_ref, rows_ref, y_ref):
    gate_t = gate_ref[...].T
    half = D_MODEL // 2
    lo_sum = x_ref[:, :half]
    hi_sum = x_ref[:, half:]
    for k in range(TOP_K):
        packed = lax.bitcast_convert_type(rows_ref[k], jnp.uint32)
        g = gate_t[:, k:k + 1]
        lo_sum = lo_sum + g * lax.bitcast_convert_type(packed << 16, F32)
        hi_sum = hi_sum + g * lax.bitcast_convert_type(packed & jnp.uint32(0xFFFF0000), F32)
    y_ref[:, :half] = lo_sum
    y_ref[:, half:] = hi_sum


def _combine(gate, x2d, rows4):
    t = x2d.shape[0]
    rows = COMBINE_ROWS
    return pl.pallas_call(
        _combine_kernel,
        grid=(t // rows,),
        in_specs=[
            pl.BlockSpec((2 * TOP_K, rows), lambda i: (0, i)),
            pl.BlockSpec((rows, D_MODEL), lambda i: (i, 0)),
            pl.BlockSpec((TOP_K, rows, D_MODEL // 2), lambda i: (0, i, 0)),
        ],
        out_specs=pl.BlockSpec((rows, D_MODEL), lambda i: (i, 0)),
        out_shape=jax.ShapeDtypeStruct((t, D_MODEL), F32),
        compiler_params=_cparams("parallel"),
        name="combine",
    )(gate, x2d, rows4)


def _moe_half(x2d, m):
    t = x2d.shape[0]
    rows = EXPERT_ROWS
    cap = t * TOP_K + N_EXPERTS * rows
    n_blk = cap // rows
    xn, idx, rank, gate, cnt = _router(x2d, m["g2"], m["wr_hi"], m["wr_lo"], m["br"], m["tri"])

    counts = cnt[:, 0].astype(jnp.int32)
    padded = (counts + rows - 1) // rows * rows
    pends = jnp.cumsum(padded)
    pstart = pends - padded
    n_used = pends[-1:] // rows
    blk_src = jnp.minimum(jnp.arange(n_blk, dtype=jnp.int32), n_used - 1)
    blk_e = jnp.minimum(jnp.sum(pends[None, :] <= (blk_src * rows)[:, None], axis=1), N_EXPERTS - 1).astype(jnp.int32)
    blk_valid = jnp.clip(pstart[blk_e] + counts[blk_e] - blk_src * rows, 0, rows).astype(jnp.int32)

    dest = _dest(pstart.astype(jnp.int32), idx, rank)
    xs = _sc_scatter(xn, dest, cap)
    out_sorted = _experts(blk_e, blk_src, blk_valid, n_used.astype(jnp.int32), xs, m["w1"], m["b1"], m["w2"], m["b2"])
    rows4 = _sc_gather(out_sorted, dest.reshape(TOP_K * t))
    return _combine(gate, x2d, rows4.reshape(TOP_K, t, D_MODEL // 2))


def _prep_moe(norm2_g, w_router, b_router, w_moe_in, b_moe_in, w_moe_out, b_moe_out):
    r = jnp.arange(ROUTER_ROWS)
    wr_hi = w_router.T.astype(BF16)
    return dict(
        g2=norm2_g.reshape(1, D_MODEL),
        wr_hi=wr_hi,
        wr_lo=(w_router.T - wr_hi.astype(F32)).astype(BF16),
        br=b_router.reshape(N_EXPERTS, 1),
        tri=(r[:, None] < r[None, :]).astype(BF16),
        w1=w_moe_in.astype(BF16),
        b1=b_moe_in.reshape(N_EXPERTS, 1, 2 * D_FF),
        w2=w_moe_out.astype(BF16),
        b2=b_moe_out.reshape(N_EXPERTS, 1, D_MODEL),
    )


def kernel(x_prompt, x_sample, norm1_g, w_in, q_norm_g, k_norm_g, attn_sink, sgu_ln_g, sgu_ln_b, w_spatial,
           b_spatial, attn_out_g, sgu_out_g, w_out, norm2_g, w_router, b_router, w_moe_in, b_moe_in, w_moe_out,
           b_moe_out):
    p = _prep_params(norm1_g[0], w_in[0], q_norm_g[0], k_norm_g[0], attn_sink[0], sgu_ln_g[0], sgu_ln_b[0],
                     w_spatial[0], b_spatial[0], attn_out_g[0], sgu_out_g[0], w_out[0])
    m = _prep_moe(norm2_g[0], w_router[0], b_router[0], w_moe_in[0], b_moe_in[0], w_moe_out[0], b_moe_out[0])
    outs = []
    for x in (x_prompt, x_sample):
        x2 = _mix_half(x, p)
        outs.append(_moe_half(x2, m).reshape(x.shape))
    return tuple(outs)
```

```python
import functools

import jax
import jax.numpy as jnp
from jax import lax
from jax.experimental import pallas as pl
from jax.experimental.pallas import tpu as pltpu
from jax.experimental.pallas import tpu_sc as plsc

D_MODEL = 1024
HEAD_DIM = 64
N_Q_HEADS = 8
N_KV_HEADS = 2
Q_PER_KV = N_Q_HEADS // N_KV_HEADS
ATTN_WIDTH = N_Q_HEADS * HEAD_DIM
KV_WIDTH = N_KV_HEADS * HEAD_DIM
QK_WIDTH = ATTN_WIDTH + KV_WIDTH
KV_DUP_WIDTH = 2 * KV_WIDTH
N_SGU_GROUPS = 8
SGU_GROUP_DIM = 64
SGU_WIDTH = N_SGU_GROUPS * SGU_GROUP_DIM
IN_PROJ_WIDTH = ATTN_WIDTH + 2 * KV_WIDTH + 2 * SGU_WIDTH
BLOCK = 128
ROPE_THETA = 500000.0
ROPE_DIM = HEAD_DIM // 4
N_EXPERTS = 32
TOP_K = 4
D_FF = D_MODEL
SWIGLU_LIMIT = 7.0
SWIGLU_ALPHA = 1.702
EPS = 1e-6

LANES = 128
IN_PROJ_ROWS = 1024
IN_PROJ_CHUNK = 256
MIXER_ROWS = 512
ROUTER_ROWS = 1024
DEST_ROWS = 2048
COMBINE_ROWS = 512
EXPERT_ROWS = 512
EXPERT_CHUNK = 512
VMEM_LIMIT_BYTES = 56 * 1024 * 1024

F32 = jnp.float32
BF16 = jnp.bfloat16


def _cparams(*semantics):
    return pltpu.CompilerParams(dimension_semantics=semantics, vmem_limit_bytes=VMEM_LIMIT_BYTES)


def _dup_heads(tile):
    low = lax.broadcasted_iota(jnp.int32, tile.shape, 1) < HEAD_DIM
    swapped = pltpu.roll(tile, HEAD_DIM, axis=1)
    return jnp.where(low, tile, swapped), jnp.where(low, swapped, tile)


def _in_proj_kernel(x_ref, g1_ref, w_ref, qkg_ref, cos_ref, sina_ref, sinb_ref, seg_ref, lng_ref, lnb_ref,
                    q_ref, k_ref, v_ref, u_ref, vn_ref):
    for r0 in range(0, IN_PROJ_ROWS, IN_PROJ_CHUNK):
        rs = slice(r0, r0 + IN_PROJ_CHUNK)
        x = x_ref[rs, :]
        h = x * lax.rsqrt(jnp.mean(x * x, axis=-1, keepdims=True) + EPS) * g1_ref[...]
        z = jnp.dot(h.astype(BF16), w_ref[...], preferred_element_type=F32)

        qk = z[:, :QK_WIDTH]
        ss = jnp.dot((qk * qk).astype(BF16), seg_ref[...], preferred_element_type=F32)
        qkn = qk * lax.rsqrt(ss * (1.0 / HEAD_DIM) + EPS) * qkg_ref[...]
        cos, sina, sinb = cos_ref[rs, :], sina_ref[rs, :], sinb_ref[rs, :]
        for c in range(QK_WIDTH // LANES):
            xc = qkn[:, c * LANES:(c + 1) * LANES]
            up = pltpu.roll(xc, LANES - ROPE_DIM // 2, axis=1)
            dn = pltpu.roll(xc, ROPE_DIM // 2, axis=1)
            rc = xc * cos + up * sina + dn * sinb
            if c < ATTN_WIDTH // LANES:
                q_ref[rs, c * LANES:(c + 1) * LANES] = (rc * (HEAD_DIM ** -0.5)).astype(BF16)
            else:
                k0, k1 = _dup_heads(rc)
                k_ref[rs, :LANES] = k0.astype(BF16)
                k_ref[rs, LANES:] = k1.astype(BF16)

        v0, v1 = _dup_heads(z[:, QK_WIDTH:QK_WIDTH + KV_WIDTH])
        v_ref[rs, :LANES] = v0.astype(BF16)
        v_ref[rs, LANES:] = v1.astype(BF16)
        su = z[:, QK_WIDTH + KV_WIDTH:QK_WIDTH + KV_WIDTH + SGU_WIDTH]
        sv = z[:, QK_WIDTH + KV_WIDTH + SGU_WIDTH:]
        u_ref[rs, :] = jax.nn.gelu(su).astype(BF16)
        gv = jax.nn.gelu(sv)
        mu = jnp.mean(gv, axis=-1, keepdims=True)
        gc = gv - mu
        ln = gc * lax.rsqrt(jnp.mean(gc * gc, axis=-1, keepdims=True) + EPS) * lng_ref[...] + lnb_ref[...]
        vn_ref[rs, :] = ln.astype(BF16)


def _rope_tables(seq):
    half = ROPE_DIM // 2
    inv_freq = ROPE_THETA ** (-(jnp.arange(half, dtype=F32) * 2.0) / ROPE_DIM)
    ang = jnp.arange(seq).astype(F32)[:, None] * inv_freq[None, :]
    cos, sin = jnp.cos(ang), jnp.sin(ang)
    j = jnp.arange(LANES) % HEAD_DIM
    f = j % half
    cos_t = jnp.where(j[None, :] < ROPE_DIM, cos[:, f], 1.0)
    sina_t = jnp.where(j[None, :] < half, -sin[:, f], 0.0)
    sinb_t = jnp.where((j[None, :] >= half) & (j[None, :] < ROPE_DIM), sin[:, f], 0.0)
    return cos_t.astype(F32), sina_t.astype(F32), sinb_t.astype(F32)


def _in_proj(x2d, seq, g1, w_in, qkg, tables, seg, lng, lnb):
    t = x2d.shape[0]
    rows = IN_PROJ_ROWS
    n_seq = seq // rows
    const = lambda i: (0, 0)
    tab = pl.BlockSpec((rows, LANES), lambda i: (i % n_seq, 0))
    return pl.pallas_call(
        _in_proj_kernel,
        grid=(t // rows,),
        in_specs=[
            pl.BlockSpec((rows, D_MODEL), lambda i: (i, 0)),
            pl.BlockSpec((1, D_MODEL), const),
            pl.BlockSpec((D_MODEL, IN_PROJ_WIDTH), const),
            pl.BlockSpec((1, QK_WIDTH), const),
            tab, tab, tab,
            pl.BlockSpec((QK_WIDTH, QK_WIDTH), const),
            pl.BlockSpec((1, SGU_WIDTH), const),
            pl.BlockSpec((1, SGU_WIDTH), const),
        ],
        out_specs=[
            pl.BlockSpec((rows, ATTN_WIDTH), lambda i: (i, 0)),
            pl.BlockSpec((rows, KV_DUP_WIDTH), lambda i: (i, 0)),
            pl.BlockSpec((rows, KV_DUP_WIDTH), lambda i: (i, 0)),
            pl.BlockSpec((rows, SGU_WIDTH), lambda i: (i, 0)),
            pl.BlockSpec((rows, SGU_WIDTH), lambda i: (i, 0)),
        ],
        out_shape=[
            jax.ShapeDtypeStruct((t, ATTN_WIDTH), BF16),
            jax.ShapeDtypeStruct((t, KV_DUP_WIDTH), BF16),
            jax.ShapeDtypeStruct((t, KV_DUP_WIDTH), BF16),
            jax.ShapeDtypeStruct((t, SGU_WIDTH), BF16),
            jax.ShapeDtypeStruct((t, SGU_WIDTH), BF16),
        ],
        compiler_params=_cparams("parallel"),
        name="in_proj",
    )(x2d, g1, w_in, qkg, *tables, seg, lng, lnb)


def _mixer_kernel(sink_ref, q_ref, kp_ref, kc_ref, kn_ref, vp_ref, vc_ref, vx_ref, u_ref, g_ref, x_ref,
                  ws_ref, bs_ref, ag_ref, sg_ref, wo_ref, o_ref, mix_ref):
    i = pl.program_id(1)
    n_i = pl.num_programs(1)
    n_sub = MIXER_ROWS // BLOCK
    kwin = jnp.concatenate([kp_ref[...], kc_ref[...], kn_ref[...]], axis=0)
    vwin = jnp.concatenate([vp_ref[...], vc_ref[...], vx_ref[...]], axis=0)

    srows = Q_PER_KV * BLOCK
    r = lax.broadcasted_iota(jnp.int32, (srows, 3 * BLOCK), 0) & (BLOCK - 1)
    c = lax.broadcasted_iota(jnp.int32, (srows, 3 * BLOCK), 1)
    band = (c >= r) & (c <= r + 2 * BLOCK)
    hrow = lax.broadcasted_iota(jnp.int32, (srows, 1), 0) // BLOCK
    low = lax.broadcasted_iota(jnp.int32, (BLOCK, LANES), 1) < HEAD_DIM
    keep = (low.astype(BF16), (~low).astype(BF16))
    ones = jnp.ones((3 * BLOCK, LANES), BF16)

    for j in range(n_sub):
        valid = band
        if j == 0:
            valid = valid & ((c >= BLOCK) | (i > 0))
        if j == n_sub - 1:
            valid = valid & ((c < 2 * BLOCK) | (i < n_i - 1))
        kj = kwin[j * BLOCK:(j + 3) * BLOCK, :]
        vj = vwin[j * BLOCK:(j + 3) * BLOCK, :]
        a_tiles = []
        for hk in range(N_KV_HEADS):
            qs = jnp.concatenate(
                [q_ref[j * BLOCK:(j + 1) * BLOCK, (h // 2) * LANES:(h // 2 + 1) * LANES] * keep[h % 2]
                 for h in range(hk * Q_PER_KV, (hk + 1) * Q_PER_KV)], axis=0)
            kh = kj[:, hk * LANES:(hk + 1) * LANES]
            vh = vj[:, hk * LANES:(hk + 1) * LANES]
            s = lax.dot_general(qs, kh, (((1,), (1,)), ((), ())), preferred_element_type=F32)
            s = jnp.where(valid, s, -jnp.inf)
            sink = jnp.zeros((srows, 1), F32)
            for g in range(Q_PER_KV):
                sink = jnp.where(hrow == g, sink_ref[hk * Q_PER_KV + g], sink)
            m = jnp.maximum(jnp.max(s, axis=-1, keepdims=True), sink)
            p = jnp.exp(s - m).astype(BF16)
            ov = jnp.dot(p, jnp.concatenate([vh, ones], axis=-1), preferred_element_type=F32)
            o = ov[:, :LANES] / (ov[:, LANES:] + jnp.exp(sink - m))
            for g in range(0, Q_PER_KV, 2):
                a_tiles.append(jnp.where(low, o[g * BLOCK:(g + 1) * BLOCK, :], o[(g + 1) * BLOCK:(g + 2) * BLOCK, :]))
        a = jnp.concatenate(a_tiles, axis=-1)
        a = a * lax.rsqrt(jnp.mean(a * a, axis=-1, keepdims=True) + EPS) * ag_ref[...]

        mixed_tiles = []
        for t in range(SGU_WIDTH // LANES):
            vt = g_ref[j * BLOCK:(j + 1) * BLOCK, t * LANES:(t + 1) * LANES]
            mixed_tiles.append(jnp.where(low, jnp.dot(ws_ref[2 * t], vt, preferred_element_type=F32),
                                         jnp.dot(ws_ref[2 * t + 1], vt, preferred_element_type=F32)))
        mixed = jnp.concatenate(mixed_tiles, axis=-1) + bs_ref[...]
        gated = u_ref[j * BLOCK:(j + 1) * BLOCK, :].astype(F32) * mixed
        gated = gated * lax.rsqrt(jnp.mean(gated * gated, axis=-1, keepdims=True) + EPS) * sg_ref[...]
        mix_ref[j * BLOCK:(j + 1) * BLOCK, :] = jnp.concatenate([a, gated], axis=-1).astype(BF16)

    o_ref[...] = x_ref[...] + jnp.dot(mix_ref[...], wo_ref[...], preferred_element_type=F32)


def _mixer(batch, seq, sink, q, k, v, u, vn, x2d, ws, bs, ag, sg, wo):
    rows = MIXER_ROWS
    n_i = seq // rows
    sub = rows // BLOCK
    n_blk = batch * seq // BLOCK
    const2 = lambda b, i, s: (0, 0)
    cur = lambda b, i, s: (b * n_i + i, 0)
    prv = lambda b, i, s: (jnp.maximum((b * n_i + i) * sub - 1, 0), 0)
    nxt = lambda b, i, s: (jnp.minimum((b * n_i + i + 1) * sub, n_blk - 1), 0)
    grid_spec = pltpu.PrefetchScalarGridSpec(
        num_scalar_prefetch=1,
        grid=(batch, n_i),
        in_specs=[
            pl.BlockSpec((rows, ATTN_WIDTH), cur),
            pl.BlockSpec((BLOCK, KV_DUP_WIDTH), prv),
            pl.BlockSpec((rows, KV_DUP_WIDTH), cur),
            pl.BlockSpec((BLOCK, KV_DUP_WIDTH), nxt),
            pl.BlockSpec((BLOCK, KV_DUP_WIDTH), prv),
            pl.BlockSpec((rows, KV_DUP_WIDTH), cur),
            pl.BlockSpec((BLOCK, KV_DUP_WIDTH), nxt),
            pl.BlockSpec((rows, SGU_WIDTH), cur),
            pl.BlockSpec((rows, SGU_WIDTH), cur),
            pl.BlockSpec((rows, D_MODEL), cur),
            pl.BlockSpec((N_SGU_GROUPS, BLOCK, BLOCK), lambda b, i, s: (0, 0, 0)),
            pl.BlockSpec((BLOCK, SGU_WIDTH), const2),
            pl.BlockSpec((1, ATTN_WIDTH), const2),
            pl.BlockSpec((1, SGU_WIDTH), const2),
            pl.BlockSpec((D_MODEL, D_MODEL), const2),
        ],
        out_specs=pl.BlockSpec((rows, D_MODEL), cur),
        scratch_shapes=[pltpu.VMEM((rows, D_MODEL), BF16)],
    )
    return pl.pallas_call(
        _mixer_kernel,
        grid_spec=grid_spec,
        out_shape=jax.ShapeDtypeStruct((batch * seq, D_MODEL), F32),
        compiler_params=_cparams("parallel", "parallel"),
        name="mixer",
    )(sink, q, k, k, k, v, v, v, u, vn, x2d, ws, bs, ag, sg, wo)


def _mix_half(x, p):
    batch, seq, _ = x.shape
    x2d = x.reshape(batch * seq, D_MODEL)
    q, k, v, u, vn = _in_proj(x2d, seq, p["g1"], p["w_in"], p["qkg"], _rope_tables(seq), p["seg"], p["lng"],
                              p["lnb"])
    return _mixer(batch, seq, p["sink"], q, k, v, u, vn, x2d, p["ws"], p["bs"], p["ag"], p["sg"], p["wo"])


def _prep_params(norm1_g, w_in, q_norm_g, k_norm_g, attn_sink, sgu_ln_g, sgu_ln_b, w_spatial, b_spatial,
                 attn_out_g, sgu_out_g, w_out):
    head = jnp.arange(QK_WIDTH) // HEAD_DIM
    return dict(
        g1=norm1_g.reshape(1, D_MODEL),
        w_in=w_in.astype(BF16),
        qkg=jnp.concatenate([jnp.tile(q_norm_g, N_Q_HEADS), jnp.tile(k_norm_g, N_KV_HEADS)]).reshape(1, QK_WIDTH),
        seg=(head[:, None] == head[None, :]).astype(BF16),
        lng=sgu_ln_g.reshape(1, SGU_WIDTH),
        lnb=sgu_ln_b.reshape(1, SGU_WIDTH),
        sink=attn_sink.astype(F32),
        ws=w_spatial.astype(BF16),
        bs=jnp.repeat(b_spatial.T, SGU_GROUP_DIM, axis=1),
        ag=attn_out_g.reshape(1, ATTN_WIDTH),
        sg=sgu_out_g.reshape(1, SGU_WIDTH),
        wo=w_out.astype(BF16),
    )


def _pack_bf16_pair(lo, hi):
    lo_b = lax.bitcast_convert_type(lo.astype(BF16).astype(F32), jnp.uint32) >> 16
    hi_b = lax.bitcast_convert_type(hi.astype(BF16).astype(F32), jnp.uint32) & jnp.uint32(0xFFFF0000)
    return hi_b | lo_b


def _unpack_bf16_pair(packed):
    lo = lax.bitcast_convert_type(packed << 16, F32).astype(BF16)
    hi = lax.bitcast_convert_type(packed & jnp.uint32(0xFFFF0000), F32).astype(BF16)
    return lo, hi


def _router_kernel(x_ref, g2_ref, wh_ref, wl_ref, br_ref, tri_ref, xn_ref, idx_ref, rank_ref, gate_ref, cnt_ref,
                   run_ref):
    @pl.when(pl.program_id(0) == 0)
    def _():
        run_ref[...] = jnp.zeros_like(run_ref)

    x = x_ref[...]
    xn = x * lax.rsqrt(jnp.mean(x * x, axis=-1, keepdims=True) + EPS) * g2_ref[...]
    xn_ref[...] = lax.bitcast_convert_type(_pack_bf16_pair(xn[:, :D_MODEL // 2], xn[:, D_MODEL // 2:]), jnp.int32)

    xh = xn.astype(BF16)
    xl = (xn - xh.astype(F32)).astype(BF16)
    nt = (((1,), (1,)), ((), ()))
    logits = (lax.dot_general(wh_ref[...], xh, nt, preferred_element_type=F32)
              + lax.dot_general(wh_ref[...], xl, nt, preferred_element_type=F32)
              + lax.dot_general(wl_ref[...], xh, nt, preferred_element_type=F32)) + br_ref[...]
    rows = logits.shape[1]
    erow = lax.broadcasted_iota(jnp.int32, (N_EXPERTS, rows), 0)
    work = logits
    vals, sels = [], []
    for k in range(TOP_K):
        m = jnp.max(work, axis=0, keepdims=True)
        ik = jnp.min(jnp.where(work == m, erow, N_EXPERTS), axis=0, keepdims=True)
        sel = erow == ik
        idx_ref[k:k + 1, :] = ik
        vals.append(m)
        sels.append(sel)
        work = jnp.where(sel, -jnp.inf, work)

    exps = [jnp.exp(v - vals[0]) for v in vals]
    den = exps[0] + exps[1] + exps[2] + exps[3]
    gate_ref[...] = jnp.zeros_like(gate_ref)
    for k in range(TOP_K):
        gate_ref[k:k + 1, :] = exps[k] / den

    onehot = jnp.zeros((N_EXPERTS, rows), F32)
    for sel in sels:
        onehot = onehot + sel.astype(F32)
    before = jnp.dot(onehot.astype(BF16), tri_ref[...], preferred_element_type=F32) + run_ref[:, :1]
    for k in range(TOP_K):
        rank_ref[k:k + 1, :] = jnp.sum(jnp.where(sels[k], before, 0.0), axis=0, keepdims=True).astype(jnp.int32)
    run_ref[...] = run_ref[...] + jnp.sum(onehot, axis=1, keepdims=True)
    cnt_ref[...] = run_ref[...]


def _router(x2d, g2, wr_hi, wr_lo, br, tri):
    t = x2d.shape[0]
    rows = ROUTER_ROWS
    const = lambda i: (0, 0)
    return pl.pallas_call(
        _router_kernel,
        grid=(t // rows,),
        in_specs=[
            pl.BlockSpec((rows, D_MODEL), lambda i: (i, 0)),
            pl.BlockSpec((1, D_MODEL), const),
            pl.BlockSpec((N_EXPERTS, D_MODEL), const),
            pl.BlockSpec((N_EXPERTS, D_MODEL), const),
            pl.BlockSpec((N_EXPERTS, 1), const),
            pl.BlockSpec((rows, rows), const),
        ],
        out_specs=[
            pl.BlockSpec((rows, D_MODEL // 2), lambda i: (i, 0)),
            pl.BlockSpec((TOP_K, rows), lambda i: (0, i)),
            pl.BlockSpec((TOP_K, rows), lambda i: (0, i)),
            pl.BlockSpec((2 * TOP_K, rows), lambda i: (0, i)),
            pl.BlockSpec((N_EXPERTS, LANES), const),
        ],
        out_shape=[
            jax.ShapeDtypeStruct((t, D_MODEL // 2), jnp.int32),
            jax.ShapeDtypeStruct((TOP_K, t), jnp.int32),
            jax.ShapeDtypeStruct((TOP_K, t), jnp.int32),
            jax.ShapeDtypeStruct((2 * TOP_K, t), F32),
            jax.ShapeDtypeStruct((N_EXPERTS, LANES), F32),
        ],
        scratch_shapes=[pltpu.VMEM((N_EXPERTS, LANES), F32)],
        compiler_params=_cparams("arbitrary"),
        name="router",
    )(x2d, g2, wr_hi, wr_lo, br, tri)


def _dest_kernel(pstart_ref, idx_ref, rank_ref, dest_ref):
    idx = idx_ref[...]
    dest = rank_ref[...]
    for e in range(N_EXPERTS):
        dest = dest + jnp.where(idx == e, pstart_ref[e], 0)
    dest_ref[...] = dest


def _dest(pstart, idx, rank):
    t = idx.shape[1]
    rows = min(DEST_ROWS, t)
    blk =pl.BlockSpec((TOP_K, rows), lambda i, s: (0, i))
    grid_spec = pltpu.PrefetchScalarGridSpec(num_scalar_prefetch=1, grid=(t // rows,), in_specs=[blk, blk],
                                             out_specs=blk)
    return pl.pallas_call(
        _dest_kernel,
        grid_spec=grid_spec,
        out_shape=jax.ShapeDtypeStruct((TOP_K, t), jnp.int32),
        compiler_params=_cparams("parallel"),
        name="dest",
    )(pstart, idx, rank)


SC_CORES = 2
SC_SUBCORES = 16
SC_WORKERS = SC_CORES * SC_SUBCORES
SC_WINDOW = 128


def _sc_mesh():
    return plsc.VectorSubcoreMesh(core_axis_name="c", subcore_axis_name="s", num_cores=SC_CORES,
                                  num_subcores=SC_SUBCORES)


def _sc_worker():
    return lax.axis_index("s") * SC_CORES + lax.axis_index("c")


def _sc_scatter(rows, idx, cap):
    t, width = rows.shape
    n_idx = idx.shape[0]
    per_worker = t // SC_WORKERS
    assert per_worker * SC_WORKERS == t and per_worker % SC_WINDOW == 0
    idx_flat = idx.reshape(n_idx * t)

    @functools.partial(
        pl.kernel,
        mesh=_sc_mesh(),
        out_type=jax.ShapeDtypeStruct((cap, width), rows.dtype),
        scratch_types=[
            pltpu.VMEM((SC_WINDOW,), jnp.int32),
            pltpu.VMEM((SC_WINDOW, width), rows.dtype),
            pltpu.SemaphoreType.DMA,
        ],
        name="sc_scatter",
    )
    def scatter(rows_hbm, idx_hbm, out_hbm, idx_v, rows_v, sem):
        base = _sc_worker() * per_worker

        @pl.loop(0, per_worker // SC_WINDOW)
        def _(step):
            off = pl.multiple_of(base + step * SC_WINDOW, SC_WINDOW)
            pltpu.sync_copy(rows_hbm.at[pl.ds(off, SC_WINDOW)], rows_v)
            for k in range(n_idx):
                pltpu.sync_copy(idx_hbm.at[pl.ds(pl.multiple_of(k * t + off, SC_WINDOW), SC_WINDOW)], idx_v)
                pltpu.async_copy(rows_v, out_hbm.at[idx_v], sem).wait()

    return scatter(rows, idx_flat)


def _experts_kernel(blk_e_ref, blk_src_ref, blk_valid_ref, n_used_ref, x_ref, w1_ref, b1_ref, w2_ref, b2_ref, o_ref):
    del blk_e_ref, blk_src_ref
    b = pl.program_id(0)

    @pl.when(b < n_used_ref[0])
    def _():
        half = D_MODEL // 2
        for r0 in range(0, EXPERT_ROWS, EXPERT_CHUNK):
            rs = slice(r0, r0 + EXPERT_CHUNK)
            row = r0 + lax.broadcasted_iota(jnp.int32, (EXPERT_CHUNK, half), 0)
            x = jnp.where(row < blk_valid_ref[b], x_ref[rs, :], 0)
            lo, hi = _unpack_bf16_pair(lax.bitcast_convert_type(x, jnp.uint32))
            h = (jnp.dot(lo, w1_ref[:half, :], preferred_element_type=F32)
                 + jnp.dot(hi, w1_ref[half:, :], preferred_element_type=F32) + b1_ref[...])
            gate = jnp.minimum(h[:, :D_FF], SWIGLU_LIMIT)
            up = jnp.clip(h[:, D_FF:], -SWIGLU_LIMIT, SWIGLU_LIMIT)
            act = (up + 1.0) * (gate * jax.nn.sigmoid(gate * SWIGLU_ALPHA))
            o = jnp.dot(act.astype(BF16), w2_ref[...], preferred_element_type=F32) + b2_ref[...]
            o_ref[rs, :] = lax.bitcast_convert_type(_pack_bf16_pair(o[:, :half], o[:, half:]), jnp.int32)

    @pl.when(b >= n_used_ref[0])
    def _():
        o_ref[...] = jnp.zeros_like(o_ref)


def _experts(blk_e, blk_src, blk_valid, n_used, xs, w1, b1, w2, b2):
    cap = xs.shape[0]
    rows = EXPERT_ROWS
    grid_spec = pltpu.PrefetchScalarGridSpec(
        num_scalar_prefetch=4,
        grid=(cap // rows,),
        in_specs=[
            pl.BlockSpec((rows, D_MODEL // 2), lambda b, be, bs, bv, nu: (bs[b], 0)),
            pl.BlockSpec((None, D_MODEL, 2 * D_FF), lambda b, be, bs, bv, nu: (be[b], 0, 0)),
            pl.BlockSpec((None, 1, 2 * D_FF), lambda b, be, bs, bv, nu: (be[b], 0, 0)),
            pl.BlockSpec((None, D_FF, D_MODEL), lambda b, be, bs, bv, nu: (be[b], 0, 0)),
            pl.BlockSpec((None, 1, D_MODEL), lambda b, be, bs, bv, nu: (be[b], 0, 0)),
        ],
        out_specs=pl.BlockSpec((rows, D_MODEL // 2), lambda b, be, bs, bv, nu: (b, 0)),
    )
    return pl.pallas_call(
        _experts_kernel,
        grid_spec=grid_spec,
        out_shape=jax.ShapeDtypeStruct((cap, D_MODEL // 2), jnp.int32),
        compiler_params=_cparams("arbitrary"),
        name="experts",
    )(blk_e, blk_src, blk_valid, n_used, xs, w1, b1, w2, b2)


def _sc_gather(table, idx):
    n = idx.shape[0]
    width = table.shape[1]
    per_worker = n // SC_WORKERS
    assert per_worker * SC_WORKERS == n and per_worker % SC_WINDOW == 0

    @functools.partial(
        pl.kernel,
        mesh=_sc_mesh(),
        out_type=jax.ShapeDtypeStruct((n, width), table.dtype),
        scratch_types=[
            pltpu.VMEM((SC_WINDOW,), jnp.int32),
            pltpu.VMEM((SC_WINDOW, width), table.dtype),
            pltpu.SemaphoreType.DMA,
        ],
        name="sc_gather",
    )
    def gather(table_hbm, idx_hbm, out_hbm, idx_v, rows_v, sem):
        base = _sc_worker() * per_worker

        @pl.loop(0, per_worker // SC_WINDOW)
        def _(step):
            off = pl.multiple_of(base + step * SC_WINDOW, SC_WINDOW)
            pltpu.sync_copy(idx_hbm.at[pl.ds(off, SC_WINDOW)], idx_v)
            pltpu.async_copy(table_hbm.at[idx_v], rows_v, sem).wait()
            pltpu.sync_copy(rows_v, out_hbm.at[pl.ds(off, SC_WINDOW)])

    return gather(table, idx)


def _combine_kernel(gate_ref, x_ref, rows_ref, y_ref):
    gate_t = gate_ref[...].T
    half = D_MODEL // 2
    lo_sum = x_ref[:, :half]
    hi_sum = x_ref[:, half:]
    for k in range(TOP_K):
        packed = lax.bitcast_convert_type(rows_ref[k], jnp.uint32)
        g = gate_t[:, k:k + 1]
        lo_sum = lo_sum + g * lax.bitcast_convert_type(packed << 16, F32)
        hi_sum = hi_sum + g * lax.bitcast_convert_type(packed & jnp.uint32(0xFFFF0000), F32)
    y_ref[:, :half] = lo_sum
    y_ref[:, half:] = hi_sum


def _combine(gate, x2d, rows4):
    t = x2d.shape[0]
    rows = COMBINE_ROWS
    return pl.pallas_call(
        _combine_kernel,
        grid=(t // rows,),
        in_specs=[
            pl.BlockSpec((2 * TOP_K, rows), lambda i: (0, i)),
            pl.BlockSpec((rows, D_MODEL), lambda i: (i, 0)),
            pl.BlockSpec((TOP_K, rows, D_MODEL // 2), lambda i: (0, i, 0)),
        ],
        out_specs=pl.BlockSpec((rows, D_MODEL), lambda i: (i, 0)),
        out_shape=jax.ShapeDtypeStruct((t, D_MODEL), F32),
        compiler_params=_cparams("parallel"),
        name="combine",
    )(gate, x2d, rows4)


def _moe_half(x2d, m):
    t = x2d.shape[0]
    rows = EXPERT_ROWS
    cap = t * TOP_K + N_EXPERTS * rows
    n_blk = cap // rows
    xn, idx, rank, gate, cnt = _router(x2d, m["g2"], m["wr_hi"], m["wr_lo"], m["br"], m["tri"])

    counts = cnt[:, 0].astype(jnp.int32)
    padded = (counts + rows - 1) // rows * rows
    pends = jnp.cumsum(padded)
    pstart = pends - padded
    n_used = pends[-1:] // rows
    blk_src = jnp.minimum(jnp.arange(n_blk, dtype=jnp.int32), n_used - 1)
    blk_e = jnp.minimum(jnp.sum(pends[None, :] <= (blk_src * rows)[:, None], axis=1), N_EXPERTS - 1).astype(jnp.int32)
    blk_valid = jnp.clip(pstart[blk_e] + counts[blk_e] - blk_src * rows, 0, rows).astype(jnp.int32)

    dest = _dest(pstart.astype(jnp.int32), idx, rank)
    xs = _sc_scatter(xn, dest, cap)
    out_sorted = _experts(blk_e, blk_src, blk_valid, n_used.astype(jnp.int32), xs, m["w1"], m["b1"], m["w2"], m["b2"])
    rows4 = _sc_gather(out_sorted, dest.reshape(TOP_K * t))
    return _combine(gate, x2d, rows4.reshape(TOP_K, t, D_MODEL // 2))


def _prep_moe(norm2_g, w_router, b_router, w_moe_in, b_moe_in, w_moe_out, b_moe_out):
    r = jnp.arange(ROUTER_ROWS)
    wr_hi = w_router.T.astype(BF16)
    return dict(
        g2=norm2_g.reshape(1, D_MODEL),
        wr_hi=wr_hi,
        wr_lo=(w_router.T - wr_hi.astype(F32)).astype(BF16),
        br=b_router.reshape(N_EXPERTS, 1),
        tri=(r[:, None] < r[None, :]).astype(BF16),
        w1=w_moe_in.astype(BF16),
        b1=b_moe_in.reshape(N_EXPERTS, 1, 2 * D_FF),
        w2=w_moe_out.astype(BF16),
        b2=b_moe_out.reshape(N_EXPERTS, 1, D_MODEL),
    )


def kernel(x_prompt, x_sample, norm1_g, w_in, q_norm_g, k_norm_g, attn_sink, sgu_ln_g, sgu_ln_b, w_spatial,
           b_spatial, attn_out_g, sgu_out_g, w_out, norm2_g, w_router, b_router, w_moe_in, b_moe_in, w_moe_out,
           b_moe_out):
    p = _prep_params(norm1_g[0], w_in[0], q_norm_g[0], k_norm_g[0], attn_sink[0], sgu_ln_g[0], sgu_ln_b[0],
                     w_spatial[0], b_spatial[0], attn_out_g[0], sgu_out_g[0], w_out[0])
    m = _prep_moe(norm2_g[0], w_router[0], b_router[0], w_moe_in[0], b_moe_in[0], w_moe_out[0], b_moe_out[0])
    outs = []
    for x in (x_prompt, x_sample):
        x2 = _mix_half(x, p)
        outs.append(_moe_half(x2, m).reshape(x.shape))
    return tuple(outs)
```

```python
import functools

import jax
import jax.numpy as jnp
from jax import lax
from jax.experimental import pallas as pl
from jax.experimental.pallas import tpu as pltpu
from jax.experimental.pallas import tpu_sc as plsc

D_MODEL = 1024
HEAD_DIM = 64
N_Q_HEADS = 8
N_KV_HEADS = 2
Q_PER_KV = N_Q_HEADS // N_KV_HEADS
ATTN_WIDTH = N_Q_HEADS * HEAD_DIM
KV_WIDTH = N_KV_HEADS * HEAD_DIM
QK_WIDTH = ATTN_WIDTH + KV_WIDTH
KV_DUP_WIDTH = 2 * KV_WIDTH
N_SGU_GROUPS = 8
SGU_GROUP_DIM = 64
SGU_WIDTH = N_SGU_GROUPS * SGU_GROUP_DIM
IN_PROJ_WIDTH = ATTN_WIDTH + 2 * KV_WIDTH + 2 * SGU_WIDTH
BLOCK = 128
ROPE_THETA = 500000.0
ROPE_DIM = HEAD_DIM // 4
N_EXPERTS = 32
TOP_K = 4
D_FF = D_MODEL
SWIGLU_LIMIT = 7.0
SWIGLU_ALPHA = 1.702
EPS = 1e-6

LANES = 128
IN_PROJ_ROWS = 1024
IN_PROJ_CHUNK = 256
MIXER_ROWS = 512
ROUTER_ROWS = 1024
DEST_ROWS = 2048
COMBINE_ROWS = 512
EXPERT_ROWS = 512
EXPERT_CHUNK = 512
VMEM_LIMIT_BYTES = 56 * 1024 * 1024

F32 = jnp.float32
BF16 = jnp.bfloat16


def _cparams(*semantics):
    return pltpu.CompilerParams(dimension_semantics=semantics, vmem_limit_bytes=VMEM_LIMIT_BYTES)


def _dup_heads(tile):
    low = lax.broadcasted_iota(jnp.int32, tile.shape, 1) < HEAD_DIM
    swapped = pltpu.roll(tile, HEAD_DIM, axis=1)
    return jnp.where(low, tile, swapped), jnp.where(low, swapped, tile)


def _in_proj_kernel(x_ref, g1_ref, w_ref, qkg_ref, cos_ref, sina_ref, sinb_ref, seg_ref, lng_ref, lnb_ref,
                    q_ref, k_ref, v_ref, u_ref, vn_ref):
    for r0 in range(0, IN_PROJ_ROWS, IN_PROJ_CHUNK):
        rs = slice(r0, r0 + IN_PROJ_CHUNK)
        x = x_ref[rs, :]
        h = x * lax.rsqrt(jnp.mean(x * x, axis=-1, keepdims=True) + EPS) * g1_ref[...]
        z = jnp.dot(h.astype(BF16), w_ref[...], preferred_element_type=F32)

        qk = z[:, :QK_WIDTH]
        ss = jnp.dot((qk * qk).astype(BF16), seg_ref[...], preferred_element_type=F32)
        qkn = qk * lax.rsqrt(ss * (1.0 / HEAD_DIM) + EPS) * qkg_ref[...]
        cos, sina, sinb = cos_ref[rs, :], sina_ref[rs, :], sinb_ref[rs, :]
        for c in range(QK_WIDTH // LANES):
            xc = qkn[:, c * LANES:(c + 1) * LANES]
            up = pltpu.roll(xc, LANES - ROPE_DIM // 2, axis=1)
            dn = pltpu.roll(xc, ROPE_DIM // 2, axis=1)
            rc = xc * cos + up * sina + dn * sinb
            if c < ATTN_WIDTH // LANES:
                q_ref[rs, c * LANES:(c + 1) * LANES] = (rc * (HEAD_DIM ** -0.5)).astype(BF16)
            else:
                k0, k1 = _dup_heads(rc)
                k_ref[rs, :LANES] = k0.astype(BF16)
                k_ref[rs, LANES:] = k1.astype(BF16)

        v0, v1 = _dup_heads(z[:, QK_WIDTH:QK_WIDTH + KV_WIDTH])
        v_ref[rs, :LANES] = v0.astype(BF16)
        v_ref[rs, LANES:] = v1.astype(BF16)
        su = z[:, QK_WIDTH + KV_WIDTH:QK_WIDTH + KV_WIDTH + SGU_WIDTH]
        sv = z[:, QK_WIDTH + KV_WIDTH + SGU_WIDTH:]
        u_ref[rs, :] = jax.nn.gelu(su).astype(BF16)
        gv = jax.nn.gelu(sv)
        mu = jnp.mean(gv, axis=-1, keepdims=True)
        gc = gv - mu
        ln = gc * lax.rsqrt(jnp.mean(gc * gc, axis=-1, keepdims=True) + EPS) * lng_ref[...] + lnb_ref[...]
        vn_ref[rs, :] = ln.astype(BF16)


def _rope_tables(seq):
    half = ROPE_DIM // 2
    inv_freq = ROPE_THETA ** (-(jnp.arange(half, dtype=F32) * 2.0) / ROPE_DIM)
    ang = jnp.arange(seq).astype(F32)[:, None] * inv_freq[None, :]
    cos, sin = jnp.cos(ang), jnp.sin(ang)
    j = jnp.arange(LANES) % HEAD_DIM
    f = j % half
    cos_t = jnp.where(j[None, :] < ROPE_DIM, cos[:, f], 1.0)
    sina_t = jnp.where(j[None, :] < half, -sin[:, f], 0.0)
    sinb_t = jnp.where((j[None, :] >= half) & (j[None, :] < ROPE_DIM), sin[:, f], 0.0)
    return cos_t.astype(F32), sina_t.astype(F32), sinb_t.astype(F32)


def _in_proj(x2d, seq, g1, w_in, qkg, tables, seg, lng, lnb):
    t = x2d.shape[0]
    rows = IN_PROJ_ROWS
    n_seq = seq // rows
    const = lambda i: (0, 0)
    tab = pl.BlockSpec((rows, LANES), lambda i: (i % n_seq, 0))
    return pl.pallas_call(
        _in_proj_kernel,
        grid=(t // rows,),
        in_specs=[
            pl.BlockSpec((rows, D_MODEL), lambda i: (i, 0)),
            pl.BlockSpec((1, D_MODEL), const),
            pl.BlockSpec((D_MODEL, IN_PROJ_WIDTH), const),
            pl.BlockSpec((1, QK_WIDTH), const),
            tab, tab, tab,
            pl.BlockSpec((QK_WIDTH, QK_WIDTH), const),
            pl.BlockSpec((1, SGU_WIDTH), const),
            pl.BlockSpec((1, SGU_WIDTH), const),
        ],
        out_specs=[
            pl.BlockSpec((rows, ATTN_WIDTH), lambda i: (i, 0)),
            pl.BlockSpec((rows, KV_DUP_WIDTH), lambda i: (i, 0)),
            pl.BlockSpec((rows, KV_DUP_WIDTH), lambda i: (i, 0)),
            pl.BlockSpec((rows, SGU_WIDTH), lambda i: (i, 0)),
            pl.BlockSpec((rows, SGU_WIDTH), lambda i: (i, 0)),
        ],
        out_shape=[
            jax.ShapeDtypeStruct((t, ATTN_WIDTH), BF16),
            jax.ShapeDtypeStruct((t, KV_DUP_WIDTH), BF16),
            jax.ShapeDtypeStruct((t, KV_DUP_WIDTH), BF16),
            jax.ShapeDtypeStruct((t, SGU_WIDTH), BF16),
            jax.ShapeDtypeStruct((t, SGU_WIDTH), BF16),
        ],
        compiler_params=_cparams("parallel"),
        name="in_proj",
    )(x2d, g1, w_in, qkg, *tables, seg, lng, lnb)


def _mixer_kernel(sink_ref, q_ref, kp_ref, kc_ref, kn_ref, vp_ref, vc_ref, vx_ref, u_ref, g_ref, x_ref,
                  ws_ref, bs_ref, ag_ref, sg_ref, wo_ref, o_ref, mix_ref):
    i = pl.program_id(1)
    n_i = pl.num_programs(1)
    n_sub = MIXER_ROWS // BLOCK
    kwin = jnp.concatenate([kp_ref[...], kc_ref[...], kn_ref[...]], axis=0)
    vwin = jnp.concatenate([vp_ref[...], vc_ref[...], vx_ref[...]], axis=0)

    srows = Q_PER_KV * BLOCK
    r = lax.broadcasted_iota(jnp.int32, (srows, 3 * BLOCK), 0) & (BLOCK - 1)
    c = lax.broadcasted_iota(jnp.int32, (srows, 3 * BLOCK), 1)
    band = (c >= r) & (c <= r + 2 * BLOCK)
    hrow = lax.broadcasted_iota(jnp.int32, (srows, 1), 0) // BLOCK
    low = lax.broadcasted_iota(jnp.int32, (BLOCK, LANES), 1) < HEAD_DIM
    keep = (low.astype(BF16), (~low).astype(BF16))
    ones = jnp.ones((3 * BLOCK, LANES), BF16)

    for j in range(n_sub):
        valid = band
        if j == 0:
            valid = valid & ((c >= BLOCK) | (i > 0))
        if j == n_sub - 1:
            valid = valid & ((c < 2 * BLOCK) | (i < n_i - 1))
        kj = kwin[j * BLOCK:(j + 3) * BLOCK, :]
        vj = vwin[j * BLOCK:(j + 3) * BLOCK, :]
        a_tiles = []
        for hk in range(N_KV_HEADS):
            qs = jnp.concatenate(
                [q_ref[j * BLOCK:(j + 1) * BLOCK, (h // 2) * LANES:(h // 2 + 1) * LANES] * keep[h % 2]
                 for h in range(hk * Q_PER_KV, (hk + 1) * Q_PER_KV)], axis=0)
            kh = kj[:, hk * LANES:(hk + 1) * LANES]
            vh = vj[:, hk * LANES:(hk + 1) * LANES]
            s = lax.dot_general(qs, kh, (((1,), (1,)), ((), ())), preferred_element_type=F32)
            s = jnp.where(valid, s, -jnp.inf)
            sink = jnp.zeros((srows, 1), F32)
            for g in range(Q_PER_KV):
                sink = jnp.where(hrow == g, sink_ref[hk * Q_PER_KV + g], sink)
            m = jnp.maximum(jnp.max(s, axis=-1, keepdims=True), sink)
            p = jnp.exp(s - m).astype(BF16)
            ov = jnp.dot(p, jnp.concatenate([vh, ones], axis=-1), preferred_element_type=F32)
            o = ov[:, :LANES] / (ov[:, LANES:] + jnp.exp(sink - m))
            for g in range(0, Q_PER_KV, 2):
                a_tiles.append(jnp.where(low, o[g * BLOCK:(g + 1) * BLOCK, :], o[(g + 1) * BLOCK:(g + 2) * BLOCK, :]))
        a = jnp.concatenate(a_tiles, axis=-1)
        a = a * lax.rsqrt(jnp.mean(a * a, axis=-1, keepdims=True) + EPS) * ag_ref[...]

        mixed_tiles = []
        for t in range(SGU_WIDTH // LANES):
            vt = g_ref[j * BLOCK:(j + 1) * BLOCK, t * LANES:(t + 1) * LANES]
            mixed_tiles.append(jnp.where(low, jnp.dot(ws_ref[2 * t], vt, preferred_element_type=F32),
                                         jnp.dot(ws_ref[2 * t + 1], vt, preferred_element_type=F32)))
        mixed = jnp.concatenate(mixed_tiles, axis=-1) + bs_ref[...]
        gated = u_ref[j * BLOCK:(j + 1) * BLOCK, :].astype(F32) * mixed
        gated = gated * lax.rsqrt(jnp.mean(gated * gated, axis=-1, keepdims=True) + EPS) * sg_ref[...]
        mix_ref[j * BLOCK:(j + 1) * BLOCK, :] = jnp.concatenate([a, gated], axis=-1).astype(BF16)

    o_ref[...] = x_ref[...] + jnp.dot(mix_ref[...], wo_ref[...], preferred_element_type=F32)


def _mixer(batch, seq, sink, q, k, v, u, vn, x2d, ws, bs, ag, sg, wo):
    rows = MIXER_ROWS
    n_i = seq // rows
    sub = rows // BLOCK
    n_blk = batch * seq // BLOCK
    const2 = lambda b, i, s: (0, 0)
    cur = lambda b, i, s: (b * n_i + i, 0)
    prv = lambda b, i, s: (jnp.maximum((b * n_i + i) * sub - 1, 0), 0)
    nxt = lambda b, i, s: (jnp.minimum((b * n_i + i + 1) * sub, n_blk - 1), 0)
    grid_spec = pltpu.PrefetchScalarGridSpec(
        num_scalar_prefetch=1,
        grid=(batch, n_i),
        in_specs=[
            pl.BlockSpec((rows, ATTN_WIDTH), cur),
            pl.BlockSpec((BLOCK, KV_DUP_WIDTH), prv),
            pl.BlockSpec((rows, KV_DUP_WIDTH), cur),
            pl.BlockSpec((BLOCK, KV_DUP_WIDTH), nxt),
            pl.BlockSpec((BLOCK, KV_DUP_WIDTH), prv),
            pl.BlockSpec((rows, KV_DUP_WIDTH), cur),
            pl.BlockSpec((BLOCK, KV_DUP_WIDTH), nxt),
            pl.BlockSpec((rows, SGU_WIDTH), cur),
            pl.BlockSpec((rows, SGU_WIDTH), cur),
            pl.BlockSpec((rows, D_MODEL), cur),
            pl.BlockSpec((N_SGU_GROUPS, BLOCK, BLOCK), lambda b, i, s: (0, 0, 0)),
            pl.BlockSpec((BLOCK, SGU_WIDTH), const2),
            pl.BlockSpec((1, ATTN_WIDTH), const2),
            pl.BlockSpec((1, SGU_WIDTH), const2),
            pl.BlockSpec((D_MODEL, D_MODEL), const2),
        ],
        out_specs=pl.BlockSpec((rows, D_MODEL), cur),
        scratch_shapes=[pltpu.VMEM((rows, D_MODEL), BF16)],
    )
    return pl.pallas_call(
        _mixer_kernel,
        grid_spec=grid_spec,
        out_shape=jax.ShapeDtypeStruct((batch * seq, D_MODEL), F32),
        compiler_params=_cparams("parallel", "parallel"),
        name="mixer",
    )(sink, q, k, k, k, v, v, v, u, vn, x2d, ws, bs, ag, sg, wo)


def _mix_half(x, p):
    batch, seq, _ = x.shape
    x2d = x.reshape(batch * seq, D_MODEL)
    q, k, v, u, vn = _in_proj(x2d, seq, p["g1"], p["w_in"], p["qkg"], _rope_tables(seq), p["seg"], p["lng"],
                              p["lnb"])
    return _mixer(batch, seq, p["sink"], q, k, v, u, vn, x2d, p["ws"], p["bs"], p["ag"], p["sg"], p["wo"])


def _prep_params(norm1_g, w_in, q_norm_g, k_norm_g, attn_sink, sgu_ln_g, sgu_ln_b, w_spatial, b_spatial,
                 attn_out_g, sgu_out_g, w_out):
    head = jnp.arange(QK_WIDTH) // HEAD_DIM
    return dict(
        g1=norm1_g.reshape(1, D_MODEL),
        w_in=w_in.astype(BF16),
        qkg=jnp.concatenate([jnp.tile(q_norm_g, N_Q_HEADS), jnp.tile(k_norm_g, N_KV_HEADS)]).reshape(1, QK_WIDTH),
        seg=(head[:, None] == head[None, :]).astype(BF16),
        lng=sgu_ln_g.reshape(1, SGU_WIDTH),
        lnb=sgu_ln_b.reshape(1, SGU_WIDTH),
        sink=attn_sink.astype(F32),
        ws=w_spatial.astype(BF16),
        bs=jnp.repeat(b_spatial.T, SGU_GROUP_DIM, axis=1),
        ag=attn_out_g.reshape(1, ATTN_WIDTH),
        sg=sgu_out_g.reshape(1, SGU_WIDTH),
        wo=w_out.astype(BF16),
    )


def _pack_bf16_pair(lo, hi):
    lo_b = lax.bitcast_convert_type(lo.astype(BF16).astype(F32), jnp.uint32) >> 16
    hi_b = lax.bitcast_convert_type(hi.astype(BF16).astype(F32), jnp.uint32) & jnp.uint32(0xFFFF0000)
    return hi_b | lo_b


def _unpack_bf16_pair(packed):
    lo = lax.bitcast_convert_type(packed << 16, F32).astype(BF16)
    hi = lax.bitcast_convert_type(packed & jnp.uint32(0xFFFF0000), F32).astype(BF16)
    return lo, hi


def _router_kernel(x_ref, g2_ref, wh_ref, wl_ref, br_ref, tri_ref, xn_ref, idx_ref, rank_ref, gate_ref, cnt_ref,
                   run_ref):
    @pl.when(pl.program_id(0) == 0)
    def _():
        run_ref[...] = jnp.zeros_like(run_ref)

    x = x_ref[...]
    xn = x * lax.rsqrt(jnp.mean(x * x, axis=-1, keepdims=True) + EPS) * g2_ref[...]
    xn_ref[...] = lax.bitcast_convert_type(_pack_bf16_pair(xn[:, :D_MODEL // 2], xn[:, D_MODEL // 2:]), jnp.int32)

    xh = xn.astype(BF16)
    xl = (xn - xh.astype(F32)).astype(BF16)
    nt = (((1,), (1,)), ((), ()))
    logits = (lax.dot_general(wh_ref[...], xh, nt, preferred_element_type=F32)
              + lax.dot_general(wh_ref[...], xl, nt, preferred_element_type=F32)
              + lax.dot_general(wl_ref[...], xh, nt, preferred_element_type=F32)) + br_ref[...]
    rows = logits.shape[1]
    erow = lax.broadcasted_iota(jnp.int32, (N_EXPERTS, rows), 0)
    work = logits
    vals, sels = [], []
    for k in range(TOP_K):
        m = jnp.max(work, axis=0, keepdims=True)
        ik = jnp.min(jnp.where(work == m, erow, N_EXPERTS), axis=0, keepdims=True)
        sel = erow == ik
        idx_ref[k:k + 1, :] = ik
        vals.append(m)
        sels.append(sel)
        work = jnp.where(sel, -jnp.inf, work)

    exps = [jnp.exp(v - vals[0]) for v in vals]
    den = exps[0] + exps[1] + exps[2] + exps[3]
    gate_ref[...] = jnp.zeros_like(gate_ref)
    for k in range(TOP_K):
        gate_ref[k:k + 1, :] = exps[k] / den

    onehot = jnp.zeros((N_EXPERTS, rows), F32)
    for sel in sels:
        onehot = onehot + sel.astype(F32)
    before = jnp.dot(onehot.astype(BF16), tri_ref[...], preferred_element_type=F32) + run_ref[:, :1]
    for k in range(TOP_K):
        rank_ref[k:k + 1, :] = jnp.sum(jnp.where(sels[k], before, 0.0), axis=0, keepdims=True).astype(jnp.int32)
    run_ref[...] = run_ref[...] + jnp.sum(onehot, axis=1, keepdims=True)
    cnt_ref[...] = run_ref[...]


def _router(x2d, g2, wr_hi, wr_lo, br, tri):
    t = x2d.shape[0]
    rows = ROUTER_ROWS
    const = lambda i: (0, 0)
    return pl.pallas_call(
        _router_kernel,
        grid=(t // rows,),
        in_specs=[
            pl.BlockSpec((rows, D_MODEL), lambda i: (i, 0)),
            pl.BlockSpec((1, D_MODEL), const),
            pl.BlockSpec((N_EXPERTS, D_MODEL), const),
            pl.BlockSpec((N_EXPERTS, D_MODEL), const),
            pl.BlockSpec((N_EXPERTS, 1), const),
            pl.BlockSpec((rows, rows), const),
        ],
        out_specs=[
            pl.BlockSpec((rows, D_MODEL // 2), lambda i: (i, 0)),
            pl.BlockSpec((TOP_K, rows), lambda i: (0, i)),
            pl.BlockSpec((TOP_K, rows), lambda i: (0, i)),
            pl.BlockSpec((2 * TOP_K, rows), lambda i: (0, i)),
            pl.BlockSpec((N_EXPERTS, LANES), const),
        ],
        out_shape=[
            jax.ShapeDtypeStruct((t, D_MODEL // 2), jnp.int32),
            jax.ShapeDtypeStruct((TOP_K, t), jnp.int32),
            jax.ShapeDtypeStruct((TOP_K, t), jnp.int32),
            jax.ShapeDtypeStruct((2 * TOP_K, t), F32),
            jax.ShapeDtypeStruct((N_EXPERTS, LANES), F32),
        ],
        scratch_shapes=[pltpu.VMEM((N_EXPERTS, LANES), F32)],
        compiler_params=_cparams("arbitrary"),
        name="router",
    )(x2d, g2, wr_hi, wr_lo, br, tri)


def _dest_kernel(pstart_ref, idx_ref, rank_ref, dest_ref):
    idx = idx_ref[...]
    dest = rank_ref[...]
    for e in range(N_EXPERTS):
        dest = dest + jnp.where(idx == e, pstart_ref[e], 0)
    dest_ref[...] = dest


def _dest(pstart, idx, rank):
    t = idx.shape[1]
    rows = min(DEST_ROWS, t)
    blk =pl.BlockSpec((TOP_K, rows), lambda i, s: (0, i))
    grid_spec = pltpu.PrefetchScalarGridSpec(num_scalar_prefetch=1, grid=(t // rows,), in_specs=[blk, blk],
                                             out_specs=blk)
    return pl.pallas_call(
        _dest_kernel,
        grid_spec=grid_spec,
        out_shape=jax.ShapeDtypeStruct((TOP_K, t), jnp.int32),
        compiler_params=_cparams("parallel"),
        name="dest",
    )(pstart, idx, rank)


SC_CORES = 2
SC_SUBCORES = 16
SC_WORKERS = SC_CORES * SC_SUBCORES
SC_WINDOW = 128


def _sc_mesh():
    return plsc.VectorSubcoreMesh(core_axis_name="c", subcore_axis_name="s", num_cores=SC_CORES,
                                  num_subcores=SC_SUBCORES)


def _sc_worker():
    return lax.axis_index("s") * SC_CORES + lax.axis_index("c")


def _sc_scatter(rows, idx, cap):
    t, width = rows.shape
    n_idx = idx.shape[0]
    per_worker = t // SC_WORKERS
    assert per_worker * SC_WORKERS == t and per_worker % SC_WINDOW == 0
    idx_flat = idx.reshape(n_idx * t)

    @functools.partial(
        pl.kernel,
        mesh=_sc_mesh(),
        out_type=jax.ShapeDtypeStruct((cap, width), rows.dtype),
        scratch_types=[
            pltpu.VMEM((SC_WINDOW,), jnp.int32),
            pltpu.VMEM((SC_WINDOW, width), rows.dtype),
            pltpu.SemaphoreType.DMA,
        ],
        name="sc_scatter",
    )
    def scatter(rows_hbm, idx_hbm, out_hbm, idx_v, rows_v, sem):
        base = _sc_worker() * per_worker

        @pl.loop(0, per_worker // SC_WINDOW)
        def _(step):
            off = pl.multiple_of(base + step * SC_WINDOW, SC_WINDOW)
            pltpu.sync_copy(rows_hbm.at[pl.ds(off, SC_WINDOW)], rows_v)
            for k in range(n_idx):
                pltpu.sync_copy(idx_hbm.at[pl.ds(pl.multiple_of(k * t + off, SC_WINDOW), SC_WINDOW)], idx_v)
                pltpu.async_copy(rows_v, out_hbm.at[idx_v], sem).wait()

    return scatter(rows, idx_flat)


def _experts_kernel(blk_e_ref, blk_src_ref, blk_valid_ref, n_used_ref, x_ref, w1_ref, b1_ref, w2_ref, b2_ref, o_ref):
    del blk_e_ref, blk_src_ref
    b = pl.program_id(0)

    @pl.when(b < n_used_ref[0])
    def _():
        half = D_MODEL // 2
        for r0 in range(0, EXPERT_ROWS, EXPERT_CHUNK):
            rs = slice(r0, r0 + EXPERT_CHUNK)
            row = r0 + lax.broadcasted_iota(jnp.int32, (EXPERT_CHUNK, half), 0)
            x = jnp.where(row < blk_valid_ref[b], x_ref[rs, :], 0)
            lo, hi = _unpack_bf16_pair(lax.bitcast_convert_type(x, jnp.uint32))
            h = (jnp.dot(lo, w1_ref[:half, :].astype(BF16), preferred_element_type=F32)
                 + jnp.dot(hi, w1_ref[half:, :].astype(BF16), preferred_element_type=F32) + b1_ref[...])
            gate = jnp.minimum(h[:, :D_FF], SWIGLU_LIMIT)
            up = jnp.clip(h[:, D_FF:], -SWIGLU_LIMIT, SWIGLU_LIMIT)
            act = (up + 1.0) * (gate * jax.nn.sigmoid(gate * SWIGLU_ALPHA))
            o = jnp.dot(act.astype(BF16), w2_ref[...].astype(BF16), preferred_element_type=F32) + b2_ref[...]
            o_ref[rs, :] = lax.bitcast_convert_type(_pack_bf16_pair(o[:, :half], o[:, half:]), jnp.int32)

    @pl.when(b >= n_used_ref[0])
    def _():
        o_ref[...] = jnp.zeros_like(o_ref)


def _experts(blk_e, blk_src, blk_valid, n_used, xs, w1, b1, w2, b2):
    cap = xs.shape[0]
    rows = EXPERT_ROWS
    grid_spec = pltpu.PrefetchScalarGridSpec(
        num_scalar_prefetch=4,
        grid=(cap // rows,),
        in_specs=[
            pl.BlockSpec((rows, D_MODEL // 2), lambda b, be, bs, bv, nu: (bs[b], 0)),
            pl.BlockSpec((None, D_MODEL, 2 * D_FF), lambda b, be, bs, bv, nu: (be[b], 0, 0)),
            pl.BlockSpec((None, 1, 2 * D_FF), lambda b, be, bs, bv, nu: (be[b], 0, 0)),
            pl.BlockSpec((None, D_FF, D_MODEL), lambda b, be, bs, bv, nu: (be[b], 0, 0)),
            pl.BlockSpec((None, 1, D_MODEL), lambda b, be, bs, bv, nu: (be[b], 0, 0)),
        ],
        out_specs=pl.BlockSpec((rows, D_MODEL // 2), lambda b, be, bs, bv, nu: (b, 0)),
    )
    return pl.pallas_call(
        _experts_kernel,
        grid_spec=grid_spec,
        out_shape=jax.ShapeDtypeStruct((cap, D_MODEL // 2), jnp.int32),
        compiler_params=_cparams("arbitrary"),
        name="experts",
    )(blk_e, blk_src, blk_valid, n_used, xs, w1, b1, w2, b2)


def _sc_gather(table, idx):
    n = idx.shape[0]
    width = table.shape[1]
    per_worker = n // SC_WORKERS
    assert per_worker * SC_WORKERS == n and per_worker % SC_WINDOW == 0

    @functools.partial(
        pl.kernel,
        mesh=_sc_mesh(),
        out_type=jax.ShapeDtypeStruct((n, width), table.dtype),
        scratch_types=[
            pltpu.VMEM((SC_WINDOW,), jnp.int32),
            pltpu.VMEM((SC_WINDOW, width), table.dtype),
            pltpu.SemaphoreType.DMA,
        ],
        name="sc_gather",
    )
    def gather(table_hbm, idx_hbm, out_hbm, idx_v, rows_v, sem):
        base = _sc_worker() * per_worker

        @pl.loop(0, per_worker // SC_WINDOW)
        def _(step):
            off = pl.multiple_of(base + step * SC_WINDOW, SC_WINDOW)
            pltpu.sync_copy(idx_hbm.at[pl.ds(off, SC_WINDOW)], idx_v)
            pltpu.async_copy(table_hbm.at[idx_v], rows_v, sem).wait()
            pltpu.sync_copy(rows_v, out_hbm.at[pl.ds(off, SC_WINDOW)])

    return gather(table, idx)


def _combine_kernel(gate_ref, x_ref, rows_ref, y_ref):
    gate_t = gate_ref[...].T
    half = D_MODEL // 2
    lo_sum = x_ref[:, :half]
    hi_sum = x_ref[:, half:]
    for k in range(TOP_K):
        packed = lax.bitcast_convert_type(rows_ref[k], jnp.uint32)
        g = gate_t[:, k:k + 1]
        lo_sum = lo_sum + g * lax.bitcast_convert_type(packed << 16, F32)
        hi_sum = hi_sum + g * lax.bitcast_convert_type(packed & jnp.uint32(0xFFFF0000), F32)
    y_ref[:, :half] = lo_sum
    y_ref[:, half:] = hi_sum


def _combine(gate, x2d, rows4):
    t = x2d.shape[0]
    rows = COMBINE_ROWS
    return pl.pallas_call(
        _combine_kernel,
        grid=(t // rows,),
        in_specs=[
            pl.BlockSpec((2 * TOP_K, rows), lambda i: (0, i)),
            pl.BlockSpec((rows, D_MODEL), lambda i: (i, 0)),
            pl.BlockSpec((TOP_K, rows, D_MODEL // 2), lambda i: (0, i, 0)),
        ],
        out_specs=pl.BlockSpec((rows, D_MODEL), lambda i: (i, 0)),
        out_shape=jax.ShapeDtypeStruct((t, D_MODEL), F32),
        compiler_params=_cparams("parallel"),
        name="combine",
    )(gate, x2d, rows4)


def _moe_half(x2d, m):
    t = x2d.shape[0]
    rows = EXPERT_ROWS
    cap = t * TOP_K + N_EXPERTS * rows
    n_blk = cap // rows
    xn, idx, rank, gate, cnt = _router(x2d, m["g2"], m["wr_hi"], m["wr_lo"], m["br"], m["tri"])

    counts = cnt[:, 0].astype(jnp.int32)
    padded = (counts + rows - 1) // rows * rows
    pends = jnp.cumsum(padded)
    pstart = pends - padded
    n_used = pends[-1:] // rows
    blk_src = jnp.minimum(jnp.arange(n_blk, dtype=jnp.int32), n_used - 1)
    blk_e = jnp.minimum(jnp.sum(pends[None, :] <= (blk_src * rows)[:, None], axis=1), N_EXPERTS - 1).astype(jnp.int32)
    blk_valid = jnp.clip(pstart[blk_e] + counts[blk_e] - blk_src * rows, 0, rows).astype(jnp.int32)

    dest = _dest(pstart.astype(jnp.int32), idx, rank)
    xs = _sc_scatter(xn, dest, cap)
    out_sorted = _experts(blk_e, blk_src, blk_valid, n_used.astype(jnp.int32), xs, m["w1"], m["b1"], m["w2"], m["b2"])
    rows4 = _sc_gather(out_sorted, dest.reshape(TOP_K * t))
    return _combine(gate, x2d, rows4.reshape(TOP_K, t, D_MODEL // 2))


def _prep_moe(norm2_g, w_router, b_router, w_moe_in, b_moe_in, w_moe_out, b_moe_out):
    r = jnp.arange(ROUTER_ROWS)
    wr_hi = w_router.T.astype(BF16)
    return dict(
        g2=norm2_g.reshape(1, D_MODEL),
        wr_hi=wr_hi,
        wr_lo=(w_router.T - wr_hi.astype(F32)).astype(BF16),
        br=b_router.reshape(N_EXPERTS, 1),
        tri=(r[:, None] < r[None, :]).astype(BF16),
        w1=w_moe_in,
        b1=b_moe_in.reshape(N_EXPERTS, 1, 2 * D_FF),
        w2=w_moe_out,
        b2=b_moe_out.reshape(N_EXPERTS, 1, D_MODEL),
    )


def kernel(x_prompt, x_sample, norm1_g, w_in, q_norm_g, k_norm_g, attn_sink, sgu_ln_g, sgu_ln_b, w_spatial,
           b_spatial, attn_out_g, sgu_out_g, w_out, norm2_g, w_router, b_router, w_moe_in, b_moe_in, w_moe_out,
           b_moe_out):
    p = _prep_params(norm1_g[0], w_in[0], q_norm_g[0], k_norm_g[0], attn_sink[0], sgu_ln_g[0], sgu_ln_b[0],
                     w_spatial[0], b_spatial[0], attn_out_g[0], sgu_out_g[0], w_out[0])
    m = _prep_moe(norm2_g[0], w_router[0], b_router[0], w_moe_in[0], b_moe_in[0], w_moe_out[0], b_moe_out[0])
    outs = []
    for x in (x_prompt, x_sample):
        x2 = _mix_half(x, p)
        outs.append(_moe_half(x2, m).reshape(x.shape))
    return tuple(outs)
```

```python
import functools

import jax
import jax.numpy as jnp
from jax import lax
from jax.experimental import pallas as pl
from jax.experimental.pallas import tpu as pltpu
from jax.experimental.pallas import tpu_sc as plsc

D_MODEL = 1024
HEAD_DIM = 64
N_Q_HEADS = 8
N_KV_HEADS = 2
Q_PER_KV = N_Q_HEADS // N_KV_HEADS
ATTN_WIDTH = N_Q_HEADS * HEAD_DIM
KV_WIDTH = N_KV_HEADS * HEAD_DIM
QK_WIDTH = ATTN_WIDTH + KV_WIDTH
KV_DUP_WIDTH = 2 * KV_WIDTH
N_SGU_GROUPS = 8
SGU_GROUP_DIM = 64
SGU_WIDTH = N_SGU_GROUPS * SGU_GROUP_DIM
IN_PROJ_WIDTH = ATTN_WIDTH + 2 * KV_WIDTH + 2 * SGU_WIDTH
BLOCK = 128
ROPE_THETA = 500000.0
ROPE_DIM = HEAD_DIM // 4
N_EXPERTS = 32
TOP_K = 4
D_FF = D_MODEL
SWIGLU_LIMIT = 7.0
SWIGLU_ALPHA = 1.702
EPS = 1e-6

LANES = 128
IN_PROJ_ROWS = 1024
IN_PROJ_CHUNK = 256
MIXER_ROWS = 1024
ROUTER_ROWS = 1024
DEST_ROWS = 2048
COMBINE_ROWS = 1024
EXPERT_ROWS = 1024
EXPERT_CHUNK = 1024
VMEM_LIMIT_BYTES = 56 * 1024 * 1024

F32 = jnp.float32
BF16 = jnp.bfloat16


def _cparams(*semantics):
    return pltpu.CompilerParams(dimension_semantics=semantics, vmem_limit_bytes=VMEM_LIMIT_BYTES)


def _dup_heads(tile):
    low = lax.broadcasted_iota(jnp.int32, tile.shape, 1) < HEAD_DIM
    swapped = pltpu.roll(tile, HEAD_DIM, axis=1)
    return jnp.where(low, tile, swapped), jnp.where(low, swapped, tile)


def _in_proj_kernel(x_ref, g1_ref, w_ref, qkg_ref, cos_ref, sina_ref, sinb_ref, seg_ref, lng_ref, lnb_ref,
                    q_ref, k_ref, v_ref, u_ref, vn_ref):
    for r0 in range(0, IN_PROJ_ROWS, IN_PROJ_CHUNK):
        rs = slice(r0, r0 + IN_PROJ_CHUNK)
        x = x_ref[rs, :]
        h = x * lax.rsqrt(jnp.mean(x * x, axis=-1, keepdims=True) + EPS) * g1_ref[...]
        z = jnp.dot(h.astype(BF16), w_ref[...], preferred_element_type=F32)

        qk = z[:, :QK_WIDTH]
        ss = jnp.dot((qk * qk).astype(BF16), seg_ref[...], preferred_element_type=F32)
        qkn = qk * lax.rsqrt(ss * (1.0 / HEAD_DIM) + EPS) * qkg_ref[...]
        cos, sina, sinb = cos_ref[rs, :], sina_ref[rs, :], sinb_ref[rs, :]
        for c in range(QK_WIDTH // LANES):
            xc = qkn[:, c * LANES:(c + 1) * LANES]
            up = pltpu.roll(xc, LANES - ROPE_DIM // 2, axis=1)
            dn = pltpu.roll(xc, ROPE_DIM // 2, axis=1)
            rc = xc * cos + up * sina + dn * sinb
            if c < ATTN_WIDTH // LANES:
                q_ref[rs, c * LANES:(c + 1) * LANES] = (rc * (HEAD_DIM ** -0.5)).astype(BF16)
            else:
                k0, k1 = _dup_heads(rc)
                k_ref[rs, :LANES] = k0.astype(BF16)
                k_ref[rs, LANES:] = k1.astype(BF16)

        v0, v1 = _dup_heads(z[:, QK_WIDTH:QK_WIDTH + KV_WIDTH])
        v_ref[rs, :LANES] = v0.astype(BF16)
        v_ref[rs, LANES:] = v1.astype(BF16)
        su = z[:, QK_WIDTH + KV_WIDTH:QK_WIDTH + KV_WIDTH + SGU_WIDTH]
        sv = z[:, QK_WIDTH + KV_WIDTH + SGU_WIDTH:]
        u_ref[rs, :] = jax.nn.gelu(su).astype(BF16)
        gv = jax.nn.gelu(sv)
        mu = jnp.mean(gv, axis=-1, keepdims=True)
        gc = gv - mu
        ln = gc * lax.rsqrt(jnp.mean(gc * gc, axis=-1, keepdims=True) + EPS) * lng_ref[...] + lnb_ref[...]
        vn_ref[rs, :] = ln.astype(BF16)


def _rope_tables(seq):
    half = ROPE_DIM // 2
    inv_freq = ROPE_THETA ** (-(jnp.arange(half, dtype=F32) * 2.0) / ROPE_DIM)
    ang = jnp.arange(seq).astype(F32)[:, None] * inv_freq[None, :]
    cos, sin = jnp.cos(ang), jnp.sin(ang)
    j = jnp.arange(LANES) % HEAD_DIM
    f = j % half
    cos_t = jnp.where(j[None, :] < ROPE_DIM, cos[:, f], 1.0)
    sina_t = jnp.where(j[None, :] < half, -sin[:, f], 0.0)
    sinb_t = jnp.where((j[None, :] >= half) & (j[None, :] < ROPE_DIM), sin[:, f], 0.0)
    return cos_t.astype(F32), sina_t.astype(F32), sinb_t.astype(F32)


def _in_proj(x2d, seq, g1, w_in, qkg, tables, seg, lng, lnb):
    t = x2d.shape[0]
    rows = IN_PROJ_ROWS
    n_seq = seq // rows
    const = lambda i: (0, 0)
    tab = pl.BlockSpec((rows, LANES), lambda i: (i % n_seq, 0))
    return pl.pallas_call(
        _in_proj_kernel,
        grid=(t // rows,),
        in_specs=[
            pl.BlockSpec((rows, D_MODEL), lambda i: (i, 0)),
            pl.BlockSpec((1, D_MODEL), const),
            pl.BlockSpec((D_MODEL, IN_PROJ_WIDTH), const),
            pl.BlockSpec((1, QK_WIDTH), const),
            tab, tab, tab,
            pl.BlockSpec((QK_WIDTH, QK_WIDTH), const),
            pl.BlockSpec((1, SGU_WIDTH), const),
            pl.BlockSpec((1, SGU_WIDTH), const),
        ],
        out_specs=[
            pl.BlockSpec((rows, ATTN_WIDTH), lambda i: (i, 0)),
            pl.BlockSpec((rows, KV_DUP_WIDTH), lambda i: (i, 0)),
            pl.BlockSpec((rows, KV_DUP_WIDTH), lambda i: (i, 0)),
            pl.BlockSpec((rows, SGU_WIDTH), lambda i: (i, 0)),
            pl.BlockSpec((rows, SGU_WIDTH), lambda i: (i, 0)),
        ],
        out_shape=[
            jax.ShapeDtypeStruct((t, ATTN_WIDTH), BF16),
            jax.ShapeDtypeStruct((t, KV_DUP_WIDTH), BF16),
            jax.ShapeDtypeStruct((t, KV_DUP_WIDTH), BF16),
            jax.ShapeDtypeStruct((t, SGU_WIDTH), BF16),
            jax.ShapeDtypeStruct((t, SGU_WIDTH), BF16),
        ],
        compiler_params=_cparams("parallel"),
        name="in_proj",
    )(x2d, g1, w_in, qkg, *tables, seg, lng, lnb)


def _mixer_kernel(sink_ref, q_ref, kp_ref, kc_ref, kn_ref, vp_ref, vc_ref, vx_ref, u_ref, g_ref, x_ref,
                  ws_ref, bs_ref, ag_ref, sg_ref, wo_ref, o_ref, mix_ref):
    i = pl.program_id(1)
    n_i = pl.num_programs(1)
    n_sub = MIXER_ROWS // BLOCK
    kwin = jnp.concatenate([kp_ref[...], kc_ref[...], kn_ref[...]], axis=0)
    vwin = jnp.concatenate([vp_ref[...], vc_ref[...], vx_ref[...]], axis=0)

    srows = Q_PER_KV * BLOCK
    r = lax.broadcasted_iota(jnp.int32, (srows, 3 * BLOCK), 0) & (BLOCK - 1)
    c = lax.broadcasted_iota(jnp.int32, (srows, 3 * BLOCK), 1)
    band = (c >= r) & (c <= r + 2 * BLOCK)
    hrow = lax.broadcasted_iota(jnp.int32, (srows, 1), 0) // BLOCK
    low = lax.broadcasted_iota(jnp.int32, (BLOCK, LANES), 1) < HEAD_DIM
    keep = (low.astype(BF16), (~low).astype(BF16))
    ones = jnp.ones((3 * BLOCK, LANES), BF16)

    for j in range(n_sub):
        valid = band
        if j == 0:
            valid = valid & ((c >= BLOCK) | (i > 0))
        if j == n_sub - 1:
            valid = valid & ((c < 2 * BLOCK) | (i < n_i - 1))
        kj = kwin[j * BLOCK:(j + 3) * BLOCK, :]
        vj = vwin[j * BLOCK:(j + 3) * BLOCK, :]
        a_tiles = []
        for hk in range(N_KV_HEADS):
            qs = jnp.concatenate(
                [q_ref[j * BLOCK:(j + 1) * BLOCK, (h // 2) * LANES:(h // 2 + 1) * LANES] * keep[h % 2]
                 for h in range(hk * Q_PER_KV, (hk + 1) * Q_PER_KV)], axis=0)
            kh = kj[:, hk * LANES:(hk + 1) * LANES]
            vh = vj[:, hk * LANES:(hk + 1) * LANES]
            s = lax.dot_general(qs, kh, (((1,), (1,)), ((), ())), preferred_element_type=F32)
            s = jnp.where(valid, s, -jnp.inf)
            sink = jnp.zeros((srows, 1), F32)
            for g in range(Q_PER_KV):
                sink = jnp.where(hrow == g, sink_ref[hk * Q_PER_KV + g], sink)
            m = jnp.maximum(jnp.max(s, axis=-1, keepdims=True), sink)
            p = jnp.exp(s - m).astype(BF16)
            ov = jnp.dot(p, jnp.concatenate([vh, ones], axis=-1), preferred_element_type=F32)
            o = ov[:, :LANES] / (ov[:, LANES:] + jnp.exp(sink - m))
            for g in range(0, Q_PER_KV, 2):
                a_tiles.append(jnp.where(low, o[g * BLOCK:(g + 1) * BLOCK, :], o[(g + 1) * BLOCK:(g + 2) * BLOCK, :]))
        a = jnp.concatenate(a_tiles, axis=-1)
        a = a * lax.rsqrt(jnp.mean(a * a, axis=-1, keepdims=True) + EPS) * ag_ref[...]

        mixed_tiles = []
        for t in range(SGU_WIDTH // LANES):
            vt = g_ref[j * BLOCK:(j + 1) * BLOCK, t * LANES:(t + 1) * LANES]
            mixed_tiles.append(jnp.where(low, jnp.dot(ws_ref[2 * t], vt, preferred_element_type=F32),
                                         jnp.dot(ws_ref[2 * t + 1], vt, preferred_element_type=F32)))
        mixed = jnp.concatenate(mixed_tiles, axis=-1) + bs_ref[...]
        gated = u_ref[j * BLOCK:(j + 1) * BLOCK, :].astype(F32) * mixed
        gated = gated * lax.rsqrt(jnp.mean(gated * gated, axis=-1, keepdims=True) + EPS) * sg_ref[...]
        mix_ref[j * BLOCK:(j + 1) * BLOCK, :] = jnp.concatenate([a, gated], axis=-1).astype(BF16)

    o_ref[...] = x_ref[...] + jnp.dot(mix_ref[...], wo_ref[...], preferred_element_type=F32)


def _mixer(batch, seq, sink, q, k, v, u, vn, x2d, ws, bs, ag, sg, wo):
    rows = MIXER_ROWS
    n_i = seq // rows
    sub = rows // BLOCK
    n_blk = batch * seq // BLOCK
    const2 = lambda b, i, s: (0, 0)
    cur = lambda b, i, s: (b * n_i + i, 0)
    prv = lambda b, i, s: (jnp.maximum((b * n_i + i) * sub - 1, 0), 0)
    nxt = lambda b, i, s: (jnp.minimum((b * n_i + i + 1) * sub, n_blk - 1), 0)
    grid_spec = pltpu.PrefetchScalarGridSpec(
        num_scalar_prefetch=1,
        grid=(batch, n_i),
        in_specs=[
            pl.BlockSpec((rows, ATTN_WIDTH), cur),
            pl.BlockSpec((BLOCK, KV_DUP_WIDTH), prv),
            pl.BlockSpec((rows, KV_DUP_WIDTH), cur),
            pl.BlockSpec((BLOCK, KV_DUP_WIDTH), nxt),
            pl.BlockSpec((BLOCK, KV_DUP_WIDTH), prv),
            pl.BlockSpec((rows, KV_DUP_WIDTH), cur),
            pl.BlockSpec((BLOCK, KV_DUP_WIDTH), nxt),
            pl.BlockSpec((rows, SGU_WIDTH), cur),
            pl.BlockSpec((rows, SGU_WIDTH), cur),
            pl.BlockSpec((rows, D_MODEL), cur),
            pl.BlockSpec((N_SGU_GROUPS, BLOCK, BLOCK), lambda b, i, s: (0, 0, 0)),
            pl.BlockSpec((BLOCK, SGU_WIDTH), const2),
            pl.BlockSpec((1, ATTN_WIDTH), const2),
            pl.BlockSpec((1, SGU_WIDTH), const2),
            pl.BlockSpec((D_MODEL, D_MODEL), const2),
        ],
        out_specs=pl.BlockSpec((rows, D_MODEL), cur),
        scratch_shapes=[pltpu.VMEM((rows, D_MODEL), BF16)],
    )
    return pl.pallas_call(
        _mixer_kernel,
        grid_spec=grid_spec,
        out_shape=jax.ShapeDtypeStruct((batch * seq, D_MODEL), F32),
        compiler_params=_cparams("parallel", "parallel"),
        name="mixer",
    )(sink, q, k, k, k, v, v, v, u, vn, x2d, ws, bs, ag, sg, wo)


def _mix_half(x, p):
    batch, seq, _ = x.shape
    x2d = x.reshape(batch * seq, D_MODEL)
    q, k, v, u, vn = _in_proj(x2d, seq, p["g1"], p["w_in"], p["qkg"], _rope_tables(seq), p["seg"], p["lng"],
                              p["lnb"])
    return _mixer(batch, seq, p["sink"], q, k, v, u, vn, x2d, p["ws"], p["bs"], p["ag"], p["sg"], p["wo"])


def _prep_params(norm1_g, w_in, q_norm_g, k_norm_g, attn_sink, sgu_ln_g, sgu_ln_b, w_spatial, b_spatial,
                 attn_out_g, sgu_out_g, w_out):
    head = jnp.arange(QK_WIDTH) // HEAD_DIM
    return dict(
        g1=norm1_g.reshape(1, D_MODEL),
        w_in=w_in.astype(BF16),
        qkg=jnp.concatenate([jnp.tile(q_norm_g, N_Q_HEADS), jnp.tile(k_norm_g, N_KV_HEADS)]).reshape(1, QK_WIDTH),
        seg=(head[:, None] == head[None, :]).astype(BF16),
        lng=sgu_ln_g.reshape(1, SGU_WIDTH),
        lnb=sgu_ln_b.reshape(1, SGU_WIDTH),
        sink=attn_sink.astype(F32),
        ws=w_spatial.astype(BF16),
        bs=jnp.repeat(b_spatial.T, SGU_GROUP_DIM, axis=1),
        ag=attn_out_g.reshape(1, ATTN_WIDTH),
        sg=sgu_out_g.reshape(1, SGU_WIDTH),
        wo=w_out.astype(BF16),
    )


def _pack_bf16_pair(lo, hi):
    lo_b = lax.bitcast_convert_type(lo.astype(BF16).astype(F32), jnp.uint32) >> 16
    hi_b = lax.bitcast_convert_type(hi.astype(BF16).astype(F32), jnp.uint32) & jnp.uint32(0xFFFF0000)
    return hi_b | lo_b


def _unpack_bf16_pair(packed):
    lo = lax.bitcast_convert_type(packed << 16, F32).astype(BF16)
    hi = lax.bitcast_convert_type(packed & jnp.uint32(0xFFFF0000), F32).astype(BF16)
    return lo, hi


def _router_kernel(x_ref, g2_ref, wh_ref, wl_ref, br_ref, tri_ref, xn_ref, idx_ref, rank_ref, gate_ref, cnt_ref,
                   run_ref):
    @pl.when(pl.program_id(0) == 0)
    def _():
        run_ref[...] = jnp.zeros_like(run_ref)

    x = x_ref[...]
    xn = x * lax.rsqrt(jnp.mean(x * x, axis=-1, keepdims=True) + EPS) * g2_ref[...]
    xn_ref[...] = lax.bitcast_convert_type(_pack_bf16_pair(xn[:, :D_MODEL // 2], xn[:, D_MODEL // 2:]), jnp.int32)

    xh = xn.astype(BF16)
    xl = (xn - xh.astype(F32)).astype(BF16)
    nt = (((1,), (1,)), ((), ()))
    logits = (lax.dot_general(wh_ref[...], xh, nt, preferred_element_type=F32)
              + lax.dot_general(wh_ref[...], xl, nt, preferred_element_type=F32)
              + lax.dot_general(wl_ref[...], xh, nt, preferred_element_type=F32)) + br_ref[...]
    rows = logits.shape[1]
    erow = lax.broadcasted_iota(jnp.int32, (N_EXPERTS, rows), 0)
    work = logits
    vals, sels = [], []
    for k in range(TOP_K):
        m = jnp.max(work, axis=0, keepdims=True)
        ik = jnp.min(jnp.where(work == m, erow, N_EXPERTS), axis=0, keepdims=True)
        sel = erow == ik
        idx_ref[k:k + 1, :] = ik
        vals.append(m)
        sels.append(sel)
        work = jnp.where(sel, -jnp.inf, work)

    exps = [jnp.exp(v - vals[0]) for v in vals]
    den = exps[0] + exps[1] + exps[2] + exps[3]
    gate_ref[...] = jnp.zeros_like(gate_ref)
    for k in range(TOP_K):
        gate_ref[k:k + 1, :] = exps[k] / den

    onehot = jnp.zeros((N_EXPERTS, rows), F32)
    for sel in sels:
        onehot = onehot + sel.astype(F32)
    before = jnp.dot(onehot.astype(BF16), tri_ref[...], preferred_element_type=F32) + run_ref[:, :1]
    for k in range(TOP_K):
        rank_ref[k:k + 1, :] = jnp.sum(jnp.where(sels[k], before, 0.0), axis=0, keepdims=True).astype(jnp.int32)
    run_ref[...] = run_ref[...] + jnp.sum(onehot, axis=1, keepdims=True)
    cnt_ref[...] = run_ref[...]


def _router(x2d, g2, wr_hi, wr_lo, br, tri):
    t = x2d.shape[0]
    rows = ROUTER_ROWS
    const = lambda i: (0, 0)
    return pl.pallas_call(
        _router_kernel,
        grid=(t // rows,),
        in_specs=[
            pl.BlockSpec((rows, D_MODEL), lambda i: (i, 0)),
            pl.BlockSpec((1, D_MODEL), const),
            pl.BlockSpec((N_EXPERTS, D_MODEL), const),
            pl.BlockSpec((N_EXPERTS, D_MODEL), const),
            pl.BlockSpec((N_EXPERTS, 1), const),
            pl.BlockSpec((rows, rows), const),
        ],
        out_specs=[
            pl.BlockSpec((rows, D_MODEL // 2), lambda i: (i, 0)),
            pl.BlockSpec((TOP_K, rows), lambda i: (0, i)),
            pl.BlockSpec((TOP_K, rows), lambda i: (0, i)),
            pl.BlockSpec((2 * TOP_K, rows), lambda i: (0, i)),
            pl.BlockSpec((N_EXPERTS, LANES), const),
        ],
        out_shape=[
            jax.ShapeDtypeStruct((t, D_MODEL // 2), jnp.int32),
            jax.ShapeDtypeStruct((TOP_K, t), jnp.int32),
            jax.ShapeDtypeStruct((TOP_K, t), jnp.int32),
            jax.ShapeDtypeStruct((2 * TOP_K, t), F32),
            jax.ShapeDtypeStruct((N_EXPERTS, LANES), F32),
        ],
        scratch_shapes=[pltpu.VMEM((N_EXPERTS, LANES), F32)],
        compiler_params=_cparams("arbitrary"),
        name="router",
    )(x2d, g2, wr_hi, wr_lo, br, tri)


def _dest_kernel(pstart_ref, idx_ref, rank_ref, dest_ref):
    idx = idx_ref[...]
    dest = rank_ref[...]
    for e in range(N_EXPERTS):
        dest = dest + jnp.where(idx == e, pstart_ref[e], 0)
    dest_ref[...] = dest


def _dest(pstart, idx, rank):
    t = idx.shape[1]
    rows = min(DEST_ROWS, t)
    blk =pl.BlockSpec((TOP_K, rows), lambda i, s: (0, i))
    grid_spec = pltpu.PrefetchScalarGridSpec(num_scalar_prefetch=1, grid=(t // rows,), in_specs=[blk, blk],
                                             out_specs=blk)
    return pl.pallas_call(
        _dest_kernel,
        grid_spec=grid_spec,
        out_shape=jax.ShapeDtypeStruct((TOP_K, t), jnp.int32),
        compiler_params=_cparams("parallel"),
        name="dest",
    )(pstart, idx, rank)


SC_CORES = 2
SC_SUBCORES = 16
SC_WORKERS = SC_CORES * SC_SUBCORES
SC_WINDOW = 128


def _sc_mesh():
    return plsc.VectorSubcoreMesh(core_axis_name="c", subcore_axis_name="s", num_cores=SC_CORES,
                                  num_subcores=SC_SUBCORES)


def _sc_worker():
    return lax.axis_index("s") * SC_CORES + lax.axis_index("c")


def _sc_scatter(rows, idx, cap):
    t, width = rows.shape
    n_idx = idx.shape[0]
    per_worker = t // SC_WORKERS
    assert per_worker * SC_WORKERS == t and per_worker % SC_WINDOW == 0
    idx_flat = idx.reshape(n_idx * t)

    @functools.partial(
        pl.kernel,
        mesh=_sc_mesh(),
        out_type=jax.ShapeDtypeStruct((cap, width), rows.dtype),
        scratch_types=[
            pltpu.VMEM((SC_WINDOW,), jnp.int32),
            pltpu.VMEM((SC_WINDOW, width), rows.dtype),
            pltpu.SemaphoreType.DMA,
        ],
        name="sc_scatter",
    )
    def scatter(rows_hbm, idx_hbm, out_hbm, idx_v, rows_v, sem):
        base = _sc_worker() * per_worker

        @pl.loop(0, per_worker // SC_WINDOW)
        def _(step):
            off = pl.multiple_of(base + step * SC_WINDOW, SC_WINDOW)
            pltpu.sync_copy(rows_hbm.at[pl.ds(off, SC_WINDOW)], rows_v)
            for k in range(n_idx):
                pltpu.sync_copy(idx_hbm.at[pl.ds(pl.multiple_of(k * t + off, SC_WINDOW), SC_WINDOW)], idx_v)
                pltpu.async_copy(rows_v, out_hbm.at[idx_v], sem).wait()

    return scatter(rows, idx_flat)


def _experts_kernel(blk_e_ref, blk_src_ref, blk_valid_ref, n_used_ref, x_ref, w1_ref, b1_ref, w2_ref, b2_ref, o_ref):
    del blk_e_ref, blk_src_ref
    b = pl.program_id(0)

    @pl.when(b < n_used_ref[0])
    def _():
        half = D_MODEL // 2
        for r0 in range(0, EXPERT_ROWS, EXPERT_CHUNK):
            rs = slice(r0, r0 + EXPERT_CHUNK)
            row = r0 + lax.broadcasted_iota(jnp.int32, (EXPERT_CHUNK, half), 0)
            x = jnp.where(row < blk_valid_ref[b], x_ref[rs, :], 0)
            lo, hi = _unpack_bf16_pair(lax.bitcast_convert_type(x, jnp.uint32))
            h = (jnp.dot(lo, w1_ref[:half, :].astype(BF16), preferred_element_type=F32)
                 + jnp.dot(hi, w1_ref[half:, :].astype(BF16), preferred_element_type=F32) + b1_ref[...])
            gate = jnp.minimum(h[:, :D_FF], SWIGLU_LIMIT)
            up = jnp.clip(h[:, D_FF:], -SWIGLU_LIMIT, SWIGLU_LIMIT)
            act = (up + 1.0) * (gate * jax.nn.sigmoid(gate * SWIGLU_ALPHA))
            o = jnp.dot(act.astype(BF16), w2_ref[...].astype(BF16), preferred_element_type=F32) + b2_ref[...]
            o_ref[rs, :] = lax.bitcast_convert_type(_pack_bf16_pair(o[:, :half], o[:, half:]), jnp.int32)

    @pl.when(b >= n_used_ref[0])
    def _():
        o_ref[...] = jnp.zeros_like(o_ref)


def _experts(blk_e, blk_src, blk_valid, n_used, xs, w1, b1, w2, b2):
    cap = xs.shape[0]
    rows = EXPERT_ROWS
    grid_spec = pltpu.PrefetchScalarGridSpec(
        num_scalar_prefetch=4,
        grid=(cap // rows,),
        in_specs=[
            pl.BlockSpec((rows, D_MODEL // 2), lambda b, be, bs, bv, nu: (bs[b], 0)),
            pl.BlockSpec((None, D_MODEL, 2 * D_FF), lambda b, be, bs, bv, nu: (be[b], 0, 0)),
            pl.BlockSpec((None, 1, 2 * D_FF), lambda b, be, bs, bv, nu: (be[b], 0, 0)),
            pl.BlockSpec((None, D_FF, D_MODEL), lambda b, be, bs, bv, nu: (be[b], 0, 0)),
            pl.BlockSpec((None, 1, D_MODEL), lambda b, be, bs, bv, nu: (be[b], 0, 0)),
        ],
        out_specs=pl.BlockSpec((rows, D_MODEL // 2), lambda b, be, bs, bv, nu: (b, 0)),
    )
    return pl.pallas_call(
        _experts_kernel,
        grid_spec=grid_spec,
        out_shape=jax.ShapeDtypeStruct((cap, D_MODEL // 2), jnp.int32),
        compiler_params=_cparams("arbitrary"),
        name="experts",
    )(blk_e, blk_src, blk_valid, n_used, xs, w1, b1, w2, b2)


def _sc_gather(table, idx):
    n = idx.shape[0]
    width = table.shape[1]
    per_worker = n // SC_WORKERS
    assert per_worker * SC_WORKERS == n and per_worker % SC_WINDOW == 0

    @functools.partial(
        pl.kernel,
        mesh=_sc_mesh(),
        out_type=jax.ShapeDtypeStruct((n, width), table.dtype),
        scratch_types=[
            pltpu.VMEM((SC_WINDOW,), jnp.int32),
            pltpu.VMEM((SC_WINDOW, width), table.dtype),
            pltpu.SemaphoreType.DMA,
        ],
        name="sc_gather",
    )
    def gather(table_hbm, idx_hbm, out_hbm, idx_v, rows_v, sem):
        base = _sc_worker() * per_worker

        @pl.loop(0, per_worker // SC_WINDOW)
        def _(step):
            off = pl.multiple_of(base + step * SC_WINDOW, SC_WINDOW)
            pltpu.sync_copy(idx_hbm.at[pl.ds(off, SC_WINDOW)], idx_v)
            pltpu.async_copy(table_hbm.at[idx_v], rows_v, sem).wait()
            pltpu.sync_copy(rows_v, out_hbm.at[pl.ds(off, SC_WINDOW)])

    return gather(table, idx)


def _combine_kernel(gate_ref, x_ref, rows_ref, y_ref):
    gate_t = gate_ref[...].T
    half = D_MODEL // 2
    lo_sum = x_ref[:, :half]
    hi_sum = x_ref[:, half:]
    for k in range(TOP_K):
        packed = lax.bitcast_convert_type(rows_ref[k], jnp.uint32)
        g = gate_t[:, k:k + 1]
        lo_sum = lo_sum + g * lax.bitcast_convert_type(packed << 16, F32)
        hi_sum = hi_sum + g * lax.bitcast_convert_type(packed & jnp.uint32(0xFFFF0000), F32)
    y_ref[:, :half] = lo_sum
    y_ref[:, half:] = hi_sum


def _combine(gate, x2d, rows4):
    t = x2d.shape[0]
    rows = COMBINE_ROWS
    return pl.pallas_call(
        _combine_kernel,
        grid=(t // rows,),
        in_specs=[
            pl.BlockSpec((2 * TOP_K, rows), lambda i: (0, i)),
            pl.BlockSpec((rows, D_MODEL), lambda i: (i, 0)),
            pl.BlockSpec((TOP_K, rows, D_MODEL // 2), lambda i: (0, i, 0)),
        ],
        out_specs=pl.BlockSpec((rows, D_MODEL), lambda i: (i, 0)),
        out_shape=jax.ShapeDtypeStruct((t, D_MODEL), F32),
        compiler_params=_cparams("parallel"),
        name="combine",
    )(gate, x2d, rows4)


def _moe_half(x2d, m):
    t = x2d.shape[0]
    rows = EXPERT_ROWS
    cap = t * TOP_K + N_EXPERTS * rows
    n_blk = cap // rows
    xn, idx, rank, gate, cnt = _router(x2d, m["g2"], m["wr_hi"], m["wr_lo"], m["br"], m["tri"])

    counts = cnt[:, 0].astype(jnp.int32)
    padded = (counts + rows - 1) // rows * rows
    pends = jnp.cumsum(padded)
    pstart = pends - padded
    n_used = pends[-1:] // rows
    blk_src = jnp.minimum(jnp.arange(n_blk, dtype=jnp.int32), n_used - 1)
    blk_e = jnp.minimum(jnp.sum(pends[None, :] <= (blk_src * rows)[:, None], axis=1), N_EXPERTS - 1).astype(jnp.int32)
    blk_valid = jnp.clip(pstart[blk_e] + counts[blk_e] - blk_src * rows, 0, rows).astype(jnp.int32)

    dest = _dest(pstart.astype(jnp.int32), idx, rank)
    xs = _sc_scatter(xn, dest, cap)
    out_sorted = _experts(blk_e, blk_src, blk_valid, n_used.astype(jnp.int32), xs, m["w1"], m["b1"], m["w2"], m["b2"])
    rows4 = _sc_gather(out_sorted, dest.reshape(TOP_K * t))
    return _combine(gate, x2d, rows4.reshape(TOP_K, t, D_MODEL // 2))


def _prep_moe(norm2_g, w_router, b_router, w_moe_in, b_moe_in, w_moe_out, b_moe_out):
    r = jnp.arange(ROUTER_ROWS)
    wr_hi = w_router.T.astype(BF16)
    return dict(
        g2=norm2_g.reshape(1, D_MODEL),
        wr_hi=wr_hi,
        wr_lo=(w_router.T - wr_hi.astype(F32)).astype(BF16),
        br=b_router.reshape(N_EXPERTS, 1),
        tri=(r[:, None] < r[None, :]).astype(BF16),
        w1=w_moe_in,
        b1=b_moe_in.reshape(N_EXPERTS, 1, 2 * D_FF),
        w2=w_moe_out,
        b2=b_moe_out.reshape(N_EXPERTS, 1, D_MODEL),
    )


def kernel(x_prompt, x_sample, norm1_g, w_in, q_norm_g, k_norm_g, attn_sink, sgu_ln_g, sgu_ln_b, w_spatial,
           b_spatial, attn_out_g, sgu_out_g, w_out, norm2_g, w_router, b_router, w_moe_in, b_moe_in, w_moe_out,
           b_moe_out):
    p = _prep_params(norm1_g[0], w_in[0], q_norm_g[0], k_norm_g[0], attn_sink[0], sgu_ln_g[0], sgu_ln_b[0],
                     w_spatial[0], b_spatial[0], attn_out_g[0], sgu_out_g[0], w_out[0])
    m = _prep_moe(norm2_g[0], w_router[0], b_router[0], w_moe_in[0], b_moe_in[0], w_moe_out[0], b_moe_out[0])
    outs = []
    for x in (x_prompt, x_sample):
        x2 = _mix_half(x, p)
        outs.append(_moe_half(x2, m).reshape(x.shape))
    return tuple(outs)
```

```python
import functools

import jax
import jax.numpy as jnp
from jax import lax
from jax.experimental import pallas as pl
from jax.experimental.pallas import tpu as pltpu
from jax.experimental.pallas import tpu_sc as plsc

D_MODEL = 1024
HEAD_DIM = 64
N_Q_HEADS = 8
N_KV_HEADS = 2
Q_PER_KV = N_Q_HEADS // N_KV_HEADS
ATTN_WIDTH = N_Q_HEADS * HEAD_DIM
KV_WIDTH = N_KV_HEADS * HEAD_DIM
QK_WIDTH = ATTN_WIDTH + KV_WIDTH
KV_DUP_WIDTH = 2 * KV_WIDTH
N_SGU_GROUPS = 8
SGU_GROUP_DIM = 64
SGU_WIDTH = N_SGU_GROUPS * SGU_GROUP_DIM
IN_PROJ_WIDTH = ATTN_WIDTH + 2 * KV_WIDTH + 2 * SGU_WIDTH
BLOCK = 128
ROPE_THETA = 500000.0
ROPE_DIM = HEAD_DIM // 4
N_EXPERTS = 32
TOP_K = 4
D_FF = D_MODEL
SWIGLU_LIMIT = 7.0
SWIGLU_ALPHA = 1.702
EPS = 1e-6

LANES = 128
IN_PROJ_ROWS = 1024
IN_PROJ_CHUNK = 256
MIXER_ROWS = 1024
ROUTER_ROWS = 1024
DEST_ROWS = 2048
COMBINE_ROWS = 1024
EXPERT_ROWS = 1024
EXPERT_ROW_STEPS = (256, 512, 1024)
VMEM_LIMIT_BYTES = 56 * 1024 * 1024

F32 = jnp.float32
BF16 = jnp.bfloat16


def _cparams(*semantics):
    return pltpu.CompilerParams(dimension_semantics=semantics, vmem_limit_bytes=VMEM_LIMIT_BYTES)


def _dup_heads(tile):
    low = lax.broadcasted_iota(jnp.int32, tile.shape, 1) < HEAD_DIM
    swapped = pltpu.roll(tile, HEAD_DIM, axis=1)
    return jnp.where(low, tile, swapped), jnp.where(low, swapped, tile)


def _in_proj_kernel(x_ref, g1_ref, w_ref, qkg_ref, cos_ref, sina_ref, sinb_ref, seg_ref, lng_ref, lnb_ref,
                    q_ref, k_ref, v_ref, u_ref, vn_ref):
    for r0 in range(0, IN_PROJ_ROWS, IN_PROJ_CHUNK):
        rs = slice(r0, r0 + IN_PROJ_CHUNK)
        x = x_ref[rs, :]
        h = x * lax.rsqrt(jnp.mean(x * x, axis=-1, keepdims=True) + EPS) * g1_ref[...]
        z = jnp.dot(h.astype(BF16), w_ref[...], preferred_element_type=F32)

        qk = z[:, :QK_WIDTH]
        ss = jnp.dot((qk * qk).astype(BF16), seg_ref[...], preferred_element_type=F32)
        qkn = qk * lax.rsqrt(ss * (1.0 / HEAD_DIM) + EPS) * qkg_ref[...]
        cos, sina, sinb = cos_ref[rs, :], sina_ref[rs, :], sinb_ref[rs, :]
        for c in range(QK_WIDTH // LANES):
            xc = qkn[:, c * LANES:(c + 1) * LANES]
            up = pltpu.roll(xc, LANES - ROPE_DIM // 2, axis=1)
            dn = pltpu.roll(xc, ROPE_DIM // 2, axis=1)
            rc = xc * cos + up * sina + dn * sinb
            if c < ATTN_WIDTH // LANES:
                q_ref[rs, c * LANES:(c + 1) * LANES] = (rc * (HEAD_DIM ** -0.5)).astype(BF16)
            else:
                k0, k1 = _dup_heads(rc)
                k_ref[rs, :LANES] = k0.astype(BF16)
                k_ref[rs, LANES:] = k1.astype(BF16)

        v0, v1 = _dup_heads(z[:, QK_WIDTH:QK_WIDTH + KV_WIDTH])
        v_ref[rs, :LANES] = v0.astype(BF16)
        v_ref[rs, LANES:] = v1.astype(BF16)
        su = z[:, QK_WIDTH + KV_WIDTH:QK_WIDTH + KV_WIDTH + SGU_WIDTH]
        sv = z[:, QK_WIDTH + KV_WIDTH + SGU_WIDTH:]
        u_ref[rs, :] = jax.nn.gelu(su).astype(BF16)
        gv = jax.nn.gelu(sv)
        mu = jnp.mean(gv, axis=-1, keepdims=True)
        gc = gv - mu
        ln = gc * lax.rsqrt(jnp.mean(gc * gc, axis=-1, keepdims=True) + EPS) * lng_ref[...] + lnb_ref[...]
        vn_ref[rs, :] = ln.astype(BF16)


def _rope_tables(seq):
    half = ROPE_DIM // 2
    inv_freq = ROPE_THETA ** (-(jnp.arange(half, dtype=F32) * 2.0) / ROPE_DIM)
    ang = jnp.arange(seq).astype(F32)[:, None] * inv_freq[None, :]
    cos, sin = jnp.cos(ang), jnp.sin(ang)
    j = jnp.arange(LANES) % HEAD_DIM
    f = j % half
    cos_t = jnp.where(j[None, :] < ROPE_DIM, cos[:, f], 1.0)
    sina_t = jnp.where(j[None, :] < half, -sin[:, f], 0.0)
    sinb_t = jnp.where((j[None, :] >= half) & (j[None, :] < ROPE_DIM), sin[:, f], 0.0)
    return cos_t.astype(F32), sina_t.astype(F32), sinb_t.astype(F32)


def _in_proj(x2d, seq, g1, w_in, qkg, tables, seg, lng, lnb):
    t = x2d.shape[0]
    rows = IN_PROJ_ROWS
    n_seq = seq // rows
    const = lambda i: (0, 0)
    tab = pl.BlockSpec((rows, LANES), lambda i: (i % n_seq, 0))
    return pl.pallas_call(
        _in_proj_kernel,
        grid=(t // rows,),
        in_specs=[
            pl.BlockSpec((rows, D_MODEL), lambda i: (i, 0)),
            pl.BlockSpec((1, D_MODEL), const),
            pl.BlockSpec((D_MODEL, IN_PROJ_WIDTH), const),
            pl.BlockSpec((1, QK_WIDTH), const),
            tab, tab, tab,
            pl.BlockSpec((QK_WIDTH, QK_WIDTH), const),
            pl.BlockSpec((1, SGU_WIDTH), const),
            pl.BlockSpec((1, SGU_WIDTH), const),
        ],
        out_specs=[
            pl.BlockSpec((rows, ATTN_WIDTH), lambda i: (i, 0)),
            pl.BlockSpec((rows, KV_DUP_WIDTH), lambda i: (i, 0)),
            pl.BlockSpec((rows, KV_DUP_WIDTH), lambda i: (i, 0)),
            pl.BlockSpec((rows, SGU_WIDTH), lambda i: (i, 0)),
            pl.BlockSpec((rows, SGU_WIDTH), lambda i: (i, 0)),
        ],
        out_shape=[
            jax.ShapeDtypeStruct((t, ATTN_WIDTH), BF16),
            jax.ShapeDtypeStruct((t, KV_DUP_WIDTH), BF16),
            jax.ShapeDtypeStruct((t, KV_DUP_WIDTH), BF16),
            jax.ShapeDtypeStruct((t, SGU_WIDTH), BF16),
            jax.ShapeDtypeStruct((t, SGU_WIDTH), BF16),
        ],
        compiler_params=_cparams("parallel"),
        name="in_proj",
    )(x2d, g1, w_in, qkg, *tables, seg, lng, lnb)


def _mixer_kernel(sink_ref, q_ref, kp_ref, kc_ref, kn_ref, vp_ref, vc_ref, vx_ref, u_ref, g_ref, x_ref,
                  ws_ref, bs_ref, ag_ref, sg_ref, wo_ref, o_ref, mix_ref):
    i = pl.program_id(1)
    n_i = pl.num_programs(1)
    n_sub = MIXER_ROWS // BLOCK
    kwin = jnp.concatenate([kp_ref[...], kc_ref[...], kn_ref[...]], axis=0)
    vwin = jnp.concatenate([vp_ref[...], vc_ref[...], vx_ref[...]], axis=0)

    srows = Q_PER_KV * BLOCK
    r = lax.broadcasted_iota(jnp.int32, (srows, 3 * BLOCK), 0) & (BLOCK - 1)
    c = lax.broadcasted_iota(jnp.int32, (srows, 3 * BLOCK), 1)
    band = (c >= r) & (c <= r + 2 * BLOCK)
    hrow = lax.broadcasted_iota(jnp.int32, (srows, 1), 0) // BLOCK
    low = lax.broadcasted_iota(jnp.int32, (BLOCK, LANES), 1) < HEAD_DIM
    keep = (low.astype(BF16), (~low).astype(BF16))
    ones = jnp.ones((3 * BLOCK, LANES), BF16)

    for j in range(n_sub):
        valid = band
        if j == 0:
            valid = valid & ((c >= BLOCK) | (i > 0))
        if j == n_sub - 1:
            valid = valid & ((c < 2 * BLOCK) | (i < n_i - 1))
        kj = kwin[j * BLOCK:(j + 3) * BLOCK, :]
        vj = vwin[j * BLOCK:(j + 3) * BLOCK, :]
        a_tiles = []
        for hk in range(N_KV_HEADS):
            qs = jnp.concatenate(
                [q_ref[j * BLOCK:(j + 1) * BLOCK, (h // 2) * LANES:(h // 2 + 1) * LANES] * keep[h % 2]
                 for h in range(hk * Q_PER_KV, (hk + 1) * Q_PER_KV)], axis=0)
            kh = kj[:, hk * LANES:(hk + 1) * LANES]
            vh = vj[:, hk * LANES:(hk + 1) * LANES]
            s = lax.dot_general(qs, kh, (((1,), (1,)), ((), ())), preferred_element_type=F32)
            s = jnp.where(valid, s, -jnp.inf)
            sink = jnp.zeros((srows, 1), F32)
            for g in range(Q_PER_KV):
                sink = jnp.where(hrow == g, sink_ref[hk * Q_PER_KV + g], sink)
            m = jnp.maximum(jnp.max(s, axis=-1, keepdims=True), sink)
            p = jnp.exp(s - m).astype(BF16)
            ov = jnp.dot(p, jnp.concatenate([vh, ones], axis=-1), preferred_element_type=F32)
            o = ov[:, :LANES] / (ov[:, LANES:] + jnp.exp(sink - m))
            for g in range(0, Q_PER_KV, 2):
                a_tiles.append(jnp.where(low, o[g * BLOCK:(g + 1) * BLOCK, :], o[(g + 1) * BLOCK:(g + 2) * BLOCK, :]))
        a = jnp.concatenate(a_tiles, axis=-1)
        a = a * lax.rsqrt(jnp.mean(a * a, axis=-1, keepdims=True) + EPS) * ag_ref[...]

        mixed_tiles = []
        for t in range(SGU_WIDTH // LANES):
            vt = g_ref[j * BLOCK:(j + 1) * BLOCK, t * LANES:(t + 1) * LANES]
            mixed_tiles.append(jnp.where(low, jnp.dot(ws_ref[2 * t], vt, preferred_element_type=F32),
                                         jnp.dot(ws_ref[2 * t + 1], vt, preferred_element_type=F32)))
        mixed = jnp.concatenate(mixed_tiles, axis=-1) + bs_ref[...]
        gated = u_ref[j * BLOCK:(j + 1) * BLOCK, :].astype(F32) * mixed
        gated = gated * lax.rsqrt(jnp.mean(gated * gated, axis=-1, keepdims=True) + EPS) * sg_ref[...]
        mix_ref[j * BLOCK:(j + 1) * BLOCK, :] = jnp.concatenate([a, gated], axis=-1).astype(BF16)

    o_ref[...] = x_ref[...] + jnp.dot(mix_ref[...], wo_ref[...], preferred_element_type=F32)


def _mixer(batch, seq, sink, q, k, v, u, vn, x2d, ws, bs, ag, sg, wo):
    rows = MIXER_ROWS
    n_i = seq // rows
    sub = rows // BLOCK
    n_blk = batch * seq // BLOCK
    const2 = lambda b, i, s: (0, 0)
    cur = lambda b, i, s: (b * n_i + i, 0)
    prv = lambda b, i, s: (jnp.maximum((b * n_i + i) * sub - 1, 0), 0)
    nxt = lambda b, i, s: (jnp.minimum((b * n_i + i + 1) * sub, n_blk - 1), 0)
    grid_spec = pltpu.PrefetchScalarGridSpec(
        num_scalar_prefetch=1,
        grid=(batch, n_i),
        in_specs=[
            pl.BlockSpec((rows, ATTN_WIDTH), cur),
            pl.BlockSpec((BLOCK, KV_DUP_WIDTH), prv),
            pl.BlockSpec((rows, KV_DUP_WIDTH), cur),
            pl.BlockSpec((BLOCK, KV_DUP_WIDTH), nxt),
            pl.BlockSpec((BLOCK, KV_DUP_WIDTH), prv),
            pl.BlockSpec((rows, KV_DUP_WIDTH), cur),
            pl.BlockSpec((BLOCK, KV_DUP_WIDTH), nxt),
            pl.BlockSpec((rows, SGU_WIDTH), cur),
            pl.BlockSpec((rows, SGU_WIDTH), cur),
            pl.BlockSpec((rows, D_MODEL), cur),
            pl.BlockSpec((N_SGU_GROUPS, BLOCK, BLOCK), lambda b, i, s: (0, 0, 0)),
            pl.BlockSpec((BLOCK, SGU_WIDTH), const2),
            pl.BlockSpec((1, ATTN_WIDTH), const2),
            pl.BlockSpec((1, SGU_WIDTH), const2),
            pl.BlockSpec((D_MODEL, D_MODEL), const2),
        ],
        out_specs=pl.BlockSpec((rows, D_MODEL), cur),
        scratch_shapes=[pltpu.VMEM((rows, D_MODEL), BF16)],
    )
    return pl.pallas_call(
        _mixer_kernel,
        grid_spec=grid_spec,
        out_shape=jax.ShapeDtypeStruct((batch * seq, D_MODEL), F32),
        compiler_params=_cparams("parallel", "parallel"),
        name="mixer",
    )(sink, q, k, k, k, v, v, v, u, vn, x2d, ws, bs, ag, sg, wo)


def _mix_half(x, p):
    batch, seq, _ = x.shape
    x2d = x.reshape(batch * seq, D_MODEL)
    q, k, v, u, vn = _in_proj(x2d, seq, p["g1"], p["w_in"], p["qkg"], _rope_tables(seq), p["seg"], p["lng"],
                              p["lnb"])
    return _mixer(batch, seq, p["sink"], q, k, v, u, vn, x2d, p["ws"], p["bs"], p["ag"], p["sg"], p["wo"])


def _prep_params(norm1_g, w_in, q_norm_g, k_norm_g, attn_sink, sgu_ln_g, sgu_ln_b, w_spatial, b_spatial,
                 attn_out_g, sgu_out_g, w_out):
    head = jnp.arange(QK_WIDTH) // HEAD_DIM
    return dict(
        g1=norm1_g.reshape(1, D_MODEL),
        w_in=w_in.astype(BF16),
        qkg=jnp.concatenate([jnp.tile(q_norm_g, N_Q_HEADS), jnp.tile(k_norm_g, N_KV_HEADS)]).reshape(1, QK_WIDTH),
        seg=(head[:, None] == head[None, :]).astype(BF16),
        lng=sgu_ln_g.reshape(1, SGU_WIDTH),
        lnb=sgu_ln_b.reshape(1, SGU_WIDTH),
        sink=attn_sink.astype(F32),
        ws=w_spatial.astype(BF16),
        bs=jnp.repeat(b_spatial.T, SGU_GROUP_DIM, axis=1),
        ag=attn_out_g.reshape(1, ATTN_WIDTH),
        sg=sgu_out_g.reshape(1, SGU_WIDTH),
        wo=w_out.astype(BF16),
    )


def _pack_bf16_pair(lo, hi):
    lo_b = lax.bitcast_convert_type(lo.astype(BF16).astype(F32), jnp.uint32) >> 16
    hi_b = lax.bitcast_convert_type(hi.astype(BF16).astype(F32), jnp.uint32) & jnp.uint32(0xFFFF0000)
    return hi_b | lo_b


def _unpack_bf16_pair(packed):
    lo = lax.bitcast_convert_type(packed << 16, F32).astype(BF16)
    hi = lax.bitcast_convert_type(packed & jnp.uint32(0xFFFF0000), F32).astype(BF16)
    return lo, hi


def _router_kernel(x_ref, g2_ref, wh_ref, wl_ref, br_ref, tri_ref, xn_ref, idx_ref, rank_ref, gate_ref, cnt_ref,
                   run_ref):
    @pl.when(pl.program_id(0) == 0)
    def _():
        run_ref[...] = jnp.zeros_like(run_ref)

    x = x_ref[...]
    xn = x * lax.rsqrt(jnp.mean(x * x, axis=-1, keepdims=True) + EPS) * g2_ref[...]
    xn_ref[...] = lax.bitcast_convert_type(_pack_bf16_pair(xn[:, :D_MODEL // 2], xn[:, D_MODEL // 2:]), jnp.int32)

    xh = xn.astype(BF16)
    xl = (xn - xh.astype(F32)).astype(BF16)
    nt = (((1,), (1,)), ((), ()))
    logits = (lax.dot_general(wh_ref[...], xh, nt, preferred_element_type=F32)
              + lax.dot_general(wh_ref[...], xl, nt, preferred_element_type=F32)
              + lax.dot_general(wl_ref[...], xh, nt, preferred_element_type=F32)) + br_ref[...]
    rows = logits.shape[1]
    erow = lax.broadcasted_iota(jnp.int32, (N_EXPERTS, rows), 0)
    work = logits
    vals, sels = [], []
    for k in range(TOP_K):
        m = jnp.max(work, axis=0, keepdims=True)
        ik = jnp.min(jnp.where(work == m, erow, N_EXPERTS), axis=0, keepdims=True)
        sel = erow == ik
        idx_ref[k:k + 1, :] = ik
        vals.append(m)
        sels.append(sel)
        work = jnp.where(sel, -jnp.inf, work)

    exps = [jnp.exp(v - vals[0]) for v in vals]
    den = exps[0] + exps[1] + exps[2] + exps[3]
    gate_ref[...] = jnp.zeros_like(gate_ref)
    for k in range(TOP_K):
        gate_ref[k:k + 1, :] = exps[k] / den

    onehot = jnp.zeros((N_EXPERTS, rows), F32)
    for sel in sels:
        onehot = onehot + sel.astype(F32)
    before = jnp.dot(onehot.astype(BF16), tri_ref[...], preferred_element_type=F32) + run_ref[:, :1]
    for k in range(TOP_K):
        rank_ref[k:k + 1, :] = jnp.sum(jnp.where(sels[k], before, 0.0), axis=0, keepdims=True).astype(jnp.int32)
    run_ref[...] = run_ref[...] + jnp.sum(onehot, axis=1, keepdims=True)
    cnt_ref[...] = run_ref[...]


def _router(x2d, g2, wr_hi, wr_lo, br, tri):
    t = x2d.shape[0]
    rows = ROUTER_ROWS
    const = lambda i: (0, 0)
    return pl.pallas_call(
        _router_kernel,
        grid=(t // rows,),
        in_specs=[
            pl.BlockSpec((rows, D_MODEL), lambda i: (i, 0)),
            pl.BlockSpec((1, D_MODEL), const),
            pl.BlockSpec((N_EXPERTS, D_MODEL), const),
            pl.BlockSpec((N_EXPERTS, D_MODEL), const),
            pl.BlockSpec((N_EXPERTS, 1), const),
            pl.BlockSpec((rows, rows), const),
        ],
        out_specs=[
            pl.BlockSpec((rows, D_MODEL // 2), lambda i: (i, 0)),
            pl.BlockSpec((TOP_K, rows), lambda i: (0, i)),
            pl.BlockSpec((TOP_K, rows), lambda i: (0, i)),
            pl.BlockSpec((2 * TOP_K, rows), lambda i: (0, i)),
            pl.BlockSpec((N_EXPERTS, LANES), const),
        ],
        out_shape=[
            jax.ShapeDtypeStruct((t, D_MODEL // 2), jnp.int32),
            jax.ShapeDtypeStruct((TOP_K, t), jnp.int32),
            jax.ShapeDtypeStruct((TOP_K, t), jnp.int32),
            jax.ShapeDtypeStruct((2 * TOP_K, t), F32),
            jax.ShapeDtypeStruct((N_EXPERTS, LANES), F32),
        ],
        scratch_shapes=[pltpu.VMEM((N_EXPERTS, LANES), F32)],
        compiler_params=_cparams("arbitrary"),
        name="router",
    )(x2d, g2, wr_hi, wr_lo, br, tri)


def _dest_kernel(pstart_ref, idx_ref, rank_ref, dest_ref):
    idx = idx_ref[...]
    dest = rank_ref[...]
    for e in range(N_EXPERTS):
        dest = dest + jnp.where(idx == e, pstart_ref[e], 0)
    dest_ref[...] = dest


def _dest(pstart, idx, rank):
    t = idx.shape[1]
    rows = min(DEST_ROWS, t)
    blk =pl.BlockSpec((TOP_K, rows), lambda i, s: (0, i))
    grid_spec = pltpu.PrefetchScalarGridSpec(num_scalar_prefetch=1, grid=(t // rows,), in_specs=[blk, blk],
                                             out_specs=blk)
    return pl.pallas_call(
        _dest_kernel,
        grid_spec=grid_spec,
        out_shape=jax.ShapeDtypeStruct((TOP_K, t), jnp.int32),
        compiler_params=_cparams("parallel"),
        name="dest",
    )(pstart, idx, rank)


SC_CORES = 2
SC_SUBCORES = 16
SC_WORKERS = SC_CORES * SC_SUBCORES
SC_WINDOW = 128


def _sc_mesh():
    return plsc.VectorSubcoreMesh(core_axis_name="c", subcore_axis_name="s", num_cores=SC_CORES,
                                  num_subcores=SC_SUBCORES)


def _sc_worker():
    return lax.axis_index("s") * SC_CORES + lax.axis_index("c")


def _sc_scatter(rows, idx, cap):
    t, width = rows.shape
    n_idx = idx.shape[0]
    per_worker = t // SC_WORKERS
    assert per_worker * SC_WORKERS == t and per_worker % SC_WINDOW == 0
    idx_flat = idx.reshape(n_idx * t)

    @functools.partial(
        pl.kernel,
        mesh=_sc_mesh(),
        out_type=jax.ShapeDtypeStruct((cap, width), rows.dtype),
        scratch_types=[
            pltpu.VMEM((SC_WINDOW,), jnp.int32),
            pltpu.VMEM((SC_WINDOW, width), rows.dtype),
            pltpu.SemaphoreType.DMA,
        ],
        name="sc_scatter",
    )
    def scatter(rows_hbm, idx_hbm, out_hbm, idx_v, rows_v, sem):
        base = _sc_worker() * per_worker

        @pl.loop(0, per_worker // SC_WINDOW)
        def _(step):
            off = pl.multiple_of(base + step * SC_WINDOW, SC_WINDOW)
            pltpu.sync_copy(rows_hbm.at[pl.ds(off, SC_WINDOW)], rows_v)
            for k in range(n_idx):
                pltpu.sync_copy(idx_hbm.at[pl.ds(pl.multiple_of(k * t + off, SC_WINDOW), SC_WINDOW)], idx_v)
                pltpu.async_copy(rows_v, out_hbm.at[idx_v], sem).wait()

    return scatter(rows, idx_flat)


def _experts_kernel(blk_e_ref, blk_src_ref, blk_valid_ref, x_ref, w1_ref, b1_ref, w2_ref, b2_ref, o_ref):
    del blk_e_ref, blk_src_ref
    valid = blk_valid_ref[pl.program_id(0)]
    half = D_MODEL // 2

    def run(rows):
        row = lax.broadcasted_iota(jnp.int32, (rows, half), 0)
        x = jnp.where(row < valid, x_ref[:rows, :], 0)
        lo, hi = _unpack_bf16_pair(lax.bitcast_convert_type(x, jnp.uint32))
        h = (jnp.dot(lo, w1_ref[:half, :].astype(BF16), preferred_element_type=F32)
             + jnp.dot(hi, w1_ref[half:, :].astype(BF16), preferred_element_type=F32) + b1_ref[...])
        gate = jnp.minimum(h[:, :D_FF], SWIGLU_LIMIT)
        up = jnp.clip(h[:, D_FF:], -SWIGLU_LIMIT, SWIGLU_LIMIT)
        act = (up + 1.0) * (gate * jax.nn.sigmoid(gate * SWIGLU_ALPHA))
        o = jnp.dot(act.astype(BF16), w2_ref[...].astype(BF16), preferred_element_type=F32) + b2_ref[...]
        o_ref[:rows, :] = lax.bitcast_convert_type(_pack_bf16_pair(o[:, :half], o[:, half:]), jnp.int32)
        if rows < EXPERT_ROWS:
            o_ref[rows:, :] = jnp.zeros((EXPERT_ROWS - rows, half), jnp.int32)

    lower = 0
    for rows in EXPERT_ROW_STEPS:
        pl.when((valid > lower) & (valid <= rows))(functools.partial(run, rows))
        lower = rows

    @pl.when(valid == 0)
    def _():
        o_ref[...] = jnp.zeros_like(o_ref)


def _experts(blk_e, blk_src, blk_valid, xs, w1, b1, w2, b2):
    cap = xs.shape[0]
    rows = EXPERT_ROWS
    grid_spec = pltpu.PrefetchScalarGridSpec(
        num_scalar_prefetch=3,
        grid=(cap // rows,),
        in_specs=[
            pl.BlockSpec((rows, D_MODEL // 2), lambda b, be, bs, bv: (bs[b], 0)),
            pl.BlockSpec((None, D_MODEL, 2 * D_FF), lambda b, be, bs, bv: (be[b], 0, 0)),
            pl.BlockSpec((None, 1, 2 * D_FF), lambda b, be, bs, bv: (be[b], 0, 0)),
            pl.BlockSpec((None, D_FF, D_MODEL), lambda b, be, bs, bv: (be[b], 0, 0)),
            pl.BlockSpec((None, 1, D_MODEL), lambda b, be, bs, bv: (be[b], 0, 0)),
        ],
        out_specs=pl.BlockSpec((rows, D_MODEL // 2), lambda b, be, bs, bv: (b, 0)),
    )
    return pl.pallas_call(
        _experts_kernel,
        grid_spec=grid_spec,
        out_shape=jax.ShapeDtypeStruct((cap, D_MODEL // 2), jnp.int32),
        compiler_params=_cparams("arbitrary"),
        name="experts",
    )(blk_e, blk_src, blk_valid, xs, w1, b1, w2, b2)


def _sc_gather(table, idx):
    n = idx.shape[0]
    width = table.shape[1]
    per_worker = n // SC_WORKERS
    assert per_worker * SC_WORKERS == n and per_worker % SC_WINDOW == 0

    @functools.partial(
        pl.kernel,
        mesh=_sc_mesh(),
        out_type=jax.ShapeDtypeStruct((n, width), table.dtype),
        scratch_types=[
            pltpu.VMEM((SC_WINDOW,), jnp.int32),
            pltpu.VMEM((SC_WINDOW, width), table.dtype),
            pltpu.SemaphoreType.DMA,
        ],
        name="sc_gather",
    )
    def gather(table_hbm, idx_hbm, out_hbm, idx_v, rows_v, sem):
        base = _sc_worker() * per_worker

        @pl.loop(0, per_worker // SC_WINDOW)
        def _(step):
            off = pl.multiple_of(base + step * SC_WINDOW, SC_WINDOW)
            pltpu.sync_copy(idx_hbm.at[pl.ds(off, SC_WINDOW)], idx_v)
            pltpu.async_copy(table_hbm.at[idx_v], rows_v, sem).wait()
            pltpu.sync_copy(rows_v, out_hbm.at[pl.ds(off, SC_WINDOW)])

    return gather(table, idx)


def _combine_kernel(gate_ref, x_ref, rows_ref, y_ref):
    gate_t = gate_ref[...].T
    half = D_MODEL // 2
    lo_sum = x_ref[:, :half]
    hi_sum = x_ref[:, half:]
    for k in range(TOP_K):
        packed = lax.bitcast_convert_type(rows_ref[k], jnp.uint32)
        g = gate_t[:, k:k + 1]
        lo_sum = lo_sum + g * lax.bitcast_convert_type(packed << 16, F32)
        hi_sum = hi_sum + g * lax.bitcast_convert_type(packed & jnp.uint32(0xFFFF0000), F32)
    y_ref[:, :half] = lo_sum
    y_ref[:, half:] = hi_sum


def _combine(gate, x2d, rows4):
    t = x2d.shape[0]
    rows = COMBINE_ROWS
    return pl.pallas_call(
        _combine_kernel,
        grid=(t // rows,),
        in_specs=[
            pl.BlockSpec((2 * TOP_K, rows), lambda i: (0, i)),
            pl.BlockSpec((rows, D_MODEL), lambda i: (i, 0)),
            pl.BlockSpec((TOP_K, rows, D_MODEL // 2), lambda i: (0, i, 0)),
        ],
        out_specs=pl.BlockSpec((rows, D_MODEL), lambda i: (i, 0)),
        out_shape=jax.ShapeDtypeStruct((t, D_MODEL), F32),
        compiler_params=_cparams("parallel"),
        name="combine",
    )(gate, x2d, rows4)


def _moe_half(x2d, m):
    t = x2d.shape[0]
    rows = EXPERT_ROWS
    cap = t * TOP_K + N_EXPERTS * rows
    n_blk = cap // rows
    xn, idx, rank, gate, cnt = _router(x2d, m["g2"], m["wr_hi"], m["wr_lo"], m["br"], m["tri"])

    counts = cnt[:, 0].astype(jnp.int32)
    padded = (counts + rows - 1) // rows * rows
    pends = jnp.cumsum(padded)
    pstart = pends - padded
    n_used = pends[-1:] // rows
    blk_src = jnp.minimum(jnp.arange(n_blk, dtype=jnp.int32), n_used - 1)
    blk_e = jnp.minimum(jnp.sum(pends[None, :] <= (blk_src * rows)[:, None], axis=1), N_EXPERTS - 1).astype(jnp.int32)
    blk_valid = jnp.clip(pstart[blk_e] + counts[blk_e] - blk_src * rows, 0, rows)
    blk_valid = jnp.where(jnp.arange(n_blk) < n_used, blk_valid, 0).astype(jnp.int32)

    dest = _dest(pstart.astype(jnp.int32), idx, rank)
    xs = _sc_scatter(xn, dest, cap)
    out_sorted = _experts(blk_e, blk_src, blk_valid, xs, m["w1"], m["b1"], m["w2"], m["b2"])
    rows4 = _sc_gather(out_sorted, dest.reshape(TOP_K * t))
    return _combine(gate, x2d, rows4.reshape(TOP_K, t, D_MODEL // 2))


def _prep_moe(norm2_g, w_router, b_router, w_moe_in, b_moe_in, w_moe_out, b_moe_out):
    r = jnp.arange(ROUTER_ROWS)
    wr_hi = w_router.T.astype(BF16)
    return dict(
        g2=norm2_g.reshape(1, D_MODEL),
        wr_hi=wr_hi,
        wr_lo=(w_router.T - wr_hi.astype(F32)).astype(BF16),
        br=b_router.reshape(N_EXPERTS, 1),
        tri=(r[:, None] < r[None, :]).astype(BF16),
        w1=w_moe_in,
        b1=b_moe_in.reshape(N_EXPERTS, 1, 2 * D_FF),
        w2=w_moe_out,
        b2=b_moe_out.reshape(N_EXPERTS, 1, D_MODEL),
    )


def kernel(x_prompt, x_sample, norm1_g, w_in, q_norm_g, k_norm_g, attn_sink, sgu_ln_g, sgu_ln_b, w_spatial,
           b_spatial, attn_out_g, sgu_out_g, w_out, norm2_g, w_router, b_router, w_moe_in, b_moe_in, w_moe_out,
           b_moe_out):
    p = _prep_params(norm1_g[0], w_in[0], q_norm_g[0], k_norm_g[0], attn_sink[0], sgu_ln_g[0], sgu_ln_b[0],
                     w_spatial[0], b_spatial[0], attn_out_g[0], sgu_out_g[0], w_out[0])
    m = _prep_moe(norm2_g[0], w_router[0], b_router[0], w_moe_in[0], b_moe_in[0], w_moe_out[0], b_moe_out[0])
    outs = []
    for x in (x_prompt, x_sample):
        x2 = _mix_half(x, p)
        outs.append(_moe_half(x2, m).reshape(x.shape))
    return tuple(outs)
```

```python
import functools

import jax
import jax.numpy as jnp
from jax import lax
from jax.experimental import pallas as pl
from jax.experimental.pallas import tpu as pltpu
from jax.experimental.pallas import tpu_sc as plsc

D_MODEL = 1024
HEAD_DIM = 64
N_Q_HEADS = 8
N_KV_HEADS = 2
Q_PER_KV = N_Q_HEADS // N_KV_HEADS
ATTN_WIDTH = N_Q_HEADS * HEAD_DIM
KV_WIDTH = N_KV_HEADS * HEAD_DIM
QK_WIDTH = ATTN_WIDTH + KV_WIDTH
KV_DUP_WIDTH = 2 * KV_WIDTH
N_SGU_GROUPS = 8
SGU_GROUP_DIM = 64
SGU_WIDTH = N_SGU_GROUPS * SGU_GROUP_DIM
IN_PROJ_WIDTH = ATTN_WIDTH + 2 * KV_WIDTH + 2 * SGU_WIDTH
BLOCK = 128
ROPE_THETA = 500000.0
ROPE_DIM = HEAD_DIM // 4
N_EXPERTS = 32
TOP_K = 4
D_FF = D_MODEL
SWIGLU_LIMIT = 7.0
SWIGLU_ALPHA = 1.702
EPS = 1e-6

LANES = 128
IN_PROJ_ROWS = 1024
IN_PROJ_CHUNK = 256
MIXER_ROWS = 1024
ROUTER_ROWS = 1024
DEST_ROWS = 2048
COMBINE_ROWS = 1024
EXPERT_ROWS = 1024
EXPERT_ROW_STEPS = (256, 512, 1024)
VMEM_LIMIT_BYTES = 56 * 1024 * 1024

F32 = jnp.float32
BF16 = jnp.bfloat16


def _cparams(*semantics):
    return pltpu.CompilerParams(dimension_semantics=semantics, vmem_limit_bytes=VMEM_LIMIT_BYTES)


def _dup_heads(tile):
    low = lax.broadcasted_iota(jnp.int32, tile.shape, 1) < HEAD_DIM
    swapped = pltpu.roll(tile, HEAD_DIM, axis=1)
    return jnp.where(low, tile, swapped), jnp.where(low, swapped, tile)


def _in_proj_kernel(x_ref, g1_ref, w_ref, qkg_ref, cos_ref, sina_ref, sinb_ref, seg_ref, lng_ref, lnb_ref,
                    q_ref, k_ref, v_ref, u_ref, vn_ref):
    for r0 in range(0, IN_PROJ_ROWS, IN_PROJ_CHUNK):
        rs = slice(r0, r0 + IN_PROJ_CHUNK)
        x = x_ref[rs, :]
        h = x * lax.rsqrt(jnp.mean(x * x, axis=-1, keepdims=True) + EPS) * g1_ref[...]
        z = jnp.dot(h.astype(BF16), w_ref[...], preferred_element_type=F32)

        qk = z[:, :QK_WIDTH]
        ss = jnp.dot((qk * qk).astype(BF16), seg_ref[...], preferred_element_type=F32)
        qkn = qk * lax.rsqrt(ss * (1.0 / HEAD_DIM) + EPS) * qkg_ref[...]
        cos, sina, sinb = cos_ref[rs, :], sina_ref[rs, :], sinb_ref[rs, :]
        for c in range(QK_WIDTH // LANES):
            xc = qkn[:, c * LANES:(c + 1) * LANES]
            up = pltpu.roll(xc, LANES - ROPE_DIM // 2, axis=1)
            dn = pltpu.roll(xc, ROPE_DIM // 2, axis=1)
            rc = xc * cos + up * sina + dn * sinb
            if c < ATTN_WIDTH // LANES:
                q_ref[rs, c * LANES:(c + 1) * LANES] = (rc * (HEAD_DIM ** -0.5)).astype(BF16)
            else:
                k0, k1 = _dup_heads(rc)
                k_ref[rs, :LANES] = k0.astype(BF16)
                k_ref[rs, LANES:] = k1.astype(BF16)

        v0, v1 = _dup_heads(z[:, QK_WIDTH:QK_WIDTH + KV_WIDTH])
        v_ref[rs, :LANES] = v0.astype(BF16)
        v_ref[rs, LANES:] = v1.astype(BF16)
        su = z[:, QK_WIDTH + KV_WIDTH:QK_WIDTH + KV_WIDTH + SGU_WIDTH]
        sv = z[:, QK_WIDTH + KV_WIDTH + SGU_WIDTH:]
        u_ref[rs, :] = jax.nn.gelu(su).astype(BF16)
        gv = jax.nn.gelu(sv)
        mu = jnp.mean(gv, axis=-1, keepdims=True)
        gc = gv - mu
        ln = gc * lax.rsqrt(jnp.mean(gc * gc, axis=-1, keepdims=True) + EPS) * lng_ref[...] + lnb_ref[...]
        vn_ref[rs, :] = ln.astype(BF16)


def _rope_tables(seq):
    half = ROPE_DIM // 2
    inv_freq = ROPE_THETA ** (-(jnp.arange(half, dtype=F32) * 2.0) / ROPE_DIM)
    ang = jnp.arange(seq).astype(F32)[:, None] * inv_freq[None, :]
    cos, sin = jnp.cos(ang), jnp.sin(ang)
    j = jnp.arange(LANES) % HEAD_DIM
    f = j % half
    cos_t = jnp.where(j[None, :] < ROPE_DIM, cos[:, f], 1.0)
    sina_t = jnp.where(j[None, :] < half, -sin[:, f], 0.0)
    sinb_t = jnp.where((j[None, :] >= half) & (j[None, :] < ROPE_DIM), sin[:, f], 0.0)
    return cos_t.astype(F32), sina_t.astype(F32), sinb_t.astype(F32)


def _in_proj(x2d, seq, g1, w_in, qkg, tables, seg, lng, lnb):
    t = x2d.shape[0]
    rows = IN_PROJ_ROWS
    n_seq = seq // rows
    const = lambda i: (0, 0)
    tab = pl.BlockSpec((rows, LANES), lambda i: (i % n_seq, 0))
    return pl.pallas_call(
        _in_proj_kernel,
        grid=(t // rows,),
        in_specs=[
            pl.BlockSpec((rows, D_MODEL), lambda i: (i, 0)),
            pl.BlockSpec((1, D_MODEL), const),
            pl.BlockSpec((D_MODEL, IN_PROJ_WIDTH), const),
            pl.BlockSpec((1, QK_WIDTH), const),
            tab, tab, tab,
            pl.BlockSpec((QK_WIDTH, QK_WIDTH), const),
            pl.BlockSpec((1, SGU_WIDTH), const),
            pl.BlockSpec((1, SGU_WIDTH), const),
        ],
        out_specs=[
            pl.BlockSpec((rows, ATTN_WIDTH), lambda i: (i, 0)),
            pl.BlockSpec((rows, KV_DUP_WIDTH), lambda i: (i, 0)),
            pl.BlockSpec((rows, KV_DUP_WIDTH), lambda i: (i, 0)),
            pl.BlockSpec((rows, SGU_WIDTH), lambda i: (i, 0)),
            pl.BlockSpec((rows, SGU_WIDTH), lambda i: (i, 0)),
        ],
        out_shape=[
            jax.ShapeDtypeStruct((t, ATTN_WIDTH), BF16),
            jax.ShapeDtypeStruct((t, KV_DUP_WIDTH), BF16),
            jax.ShapeDtypeStruct((t, KV_DUP_WIDTH), BF16),
            jax.ShapeDtypeStruct((t, SGU_WIDTH), BF16),
            jax.ShapeDtypeStruct((t, SGU_WIDTH), BF16),
        ],
        compiler_params=_cparams("parallel"),
        name="in_proj",
    )(x2d, g1, w_in, qkg, *tables, seg, lng, lnb)


def _mixer_kernel(sink_ref, q_ref, kp_ref, kc_ref, kn_ref, vp_ref, vc_ref, vx_ref, u_ref, g_ref, x_ref,
                  ws_ref, bs_ref, ag_ref, sg_ref, wo_ref, o_ref, mix_ref):
    i = pl.program_id(1)
    n_i = pl.num_programs(1)
    n_sub = MIXER_ROWS // BLOCK
    kwin = jnp.concatenate([kp_ref[...], kc_ref[...], kn_ref[...]], axis=0)
    vwin = jnp.concatenate([vp_ref[...], vc_ref[...], vx_ref[...]], axis=0)

    srows = Q_PER_KV * BLOCK
    r = lax.broadcasted_iota(jnp.int32, (srows, 3 * BLOCK), 0) & (BLOCK - 1)
    c = lax.broadcasted_iota(jnp.int32, (srows, 3 * BLOCK), 1)
    band = (c >= r) & (c <= r + 2 * BLOCK)
    hrow = lax.broadcasted_iota(jnp.int32, (srows, 1), 0) // BLOCK
    low = lax.broadcasted_iota(jnp.int32, (BLOCK, LANES), 1) < HEAD_DIM
    keep = (low.astype(BF16), (~low).astype(BF16))
    ones = jnp.ones((3 * BLOCK, LANES), BF16)

    for j in range(n_sub):
        valid = band
        if j == 0:
            valid = valid & ((c >= BLOCK) | (i > 0))
        if j == n_sub - 1:
            valid = valid & ((c < 2 * BLOCK) | (i < n_i - 1))
        kj = kwin[j * BLOCK:(j + 3) * BLOCK, :]
        vj = vwin[j * BLOCK:(j + 3) * BLOCK, :]
        a_tiles = []
        for hk in range(N_KV_HEADS):
            qs = jnp.concatenate(
                [q_ref[j * BLOCK:(j + 1) * BLOCK, (h // 2) * LANES:(h // 2 + 1) * LANES] * keep[h % 2]
                 for h in range(hk * Q_PER_KV, (hk + 1) * Q_PER_KV)], axis=0)
            kh = kj[:, hk * LANES:(hk + 1) * LANES]
            vh = vj[:, hk * LANES:(hk + 1) * LANES]
            s = lax.dot_general(qs, kh, (((1,), (1,)), ((), ())), preferred_element_type=F32)
            s = jnp.where(valid, s, -jnp.inf)
            sink = jnp.zeros((srows, 1), F32)
            for g in range(Q_PER_KV):
                sink = jnp.where(hrow == g, sink_ref[hk * Q_PER_KV + g], sink)
            m = jnp.maximum(jnp.max(s, axis=-1, keepdims=True), sink)
            p = jnp.exp(s - m).astype(BF16)
            ov = jnp.dot(p, jnp.concatenate([vh, ones], axis=-1), preferred_element_type=F32)
            o = ov[:, :LANES] / (ov[:, LANES:] + jnp.exp(sink - m))
            for g in range(0, Q_PER_KV, 2):
                a_tiles.append(jnp.where(low, o[g * BLOCK:(g + 1) * BLOCK, :], o[(g + 1) * BLOCK:(g + 2) * BLOCK, :]))
        a = jnp.concatenate(a_tiles, axis=-1)
        a = a * lax.rsqrt(jnp.mean(a * a, axis=-1, keepdims=True) + EPS) * ag_ref[...]

        mixed_tiles = []
        for t in range(SGU_WIDTH // LANES):
            vt = g_ref[j * BLOCK:(j + 1) * BLOCK, t * LANES:(t + 1) * LANES]
            mixed_tiles.append(jnp.where(low, jnp.dot(ws_ref[2 * t], vt, preferred_element_type=F32),
                                         jnp.dot(ws_ref[2 * t + 1], vt, preferred_element_type=F32)))
        mixed = jnp.concatenate(mixed_tiles, axis=-1) + bs_ref[...]
        gated = u_ref[j * BLOCK:(j + 1) * BLOCK, :].astype(F32) * mixed
        gated = gated * lax.rsqrt(jnp.mean(gated * gated, axis=-1, keepdims=True) + EPS) * sg_ref[...]
        mix_ref[j * BLOCK:(j + 1) * BLOCK, :] = jnp.concatenate([a, gated], axis=-1).astype(BF16)

    o_ref[...] = x_ref[...] + jnp.dot(mix_ref[...], wo_ref[...], preferred_element_type=F32)


def _mixer(batch, seq, sink, q, k, v, u, vn, x2d, ws, bs, ag, sg, wo):
    rows = MIXER_ROWS
    n_i = seq // rows
    sub = rows // BLOCK
    n_blk = batch * seq // BLOCK
    const2 = lambda b, i, s: (0, 0)
    cur = lambda b, i, s: (b * n_i + i, 0)
    prv = lambda b, i, s: (jnp.maximum((b * n_i + i) * sub - 1, 0), 0)
    nxt = lambda b, i, s: (jnp.minimum((b * n_i + i + 1) * sub, n_blk - 1), 0)
    grid_spec = pltpu.PrefetchScalarGridSpec(
        num_scalar_prefetch=1,
        grid=(batch, n_i),
        in_specs=[
            pl.BlockSpec((rows, ATTN_WIDTH), cur),
            pl.BlockSpec((BLOCK, KV_DUP_WIDTH), prv),
            pl.BlockSpec((rows, KV_DUP_WIDTH), cur),
            pl.BlockSpec((BLOCK, KV_DUP_WIDTH), nxt),
            pl.BlockSpec((BLOCK, KV_DUP_WIDTH), prv),
            pl.BlockSpec((rows, KV_DUP_WIDTH), cur),
            pl.BlockSpec((BLOCK, KV_DUP_WIDTH), nxt),
            pl.BlockSpec((rows, SGU_WIDTH), cur),
            pl.BlockSpec((rows, SGU_WIDTH), cur),
            pl.BlockSpec((rows, D_MODEL), cur),
            pl.BlockSpec((N_SGU_GROUPS, BLOCK, BLOCK), lambda b, i, s: (0, 0, 0)),
            pl.BlockSpec((BLOCK, SGU_WIDTH), const2),
            pl.BlockSpec((1, ATTN_WIDTH), const2),
            pl.BlockSpec((1, SGU_WIDTH), const2),
            pl.BlockSpec((D_MODEL, D_MODEL), const2),
        ],
        out_specs=pl.BlockSpec((rows, D_MODEL), cur),
        scratch_shapes=[pltpu.VMEM((rows, D_MODEL), BF16)],
    )
    return pl.pallas_call(
        _mixer_kernel,
        grid_spec=grid_spec,
        out_shape=jax.ShapeDtypeStruct((batch * seq, D_MODEL), F32),
        compiler_params=_cparams("parallel", "parallel"),
        name="mixer",
    )(sink, q, k, k, k, v, v, v, u, vn, x2d, ws, bs, ag, sg, wo)


def _mix_half(x, p):
    batch, seq, _ = x.shape
    x2d = x.reshape(batch * seq, D_MODEL)
    q, k, v, u, vn = _in_proj(x2d, seq, p["g1"], p["w_in"], p["qkg"], _rope_tables(seq), p["seg"], p["lng"],
                              p["lnb"])
    return _mixer(batch, seq, p["sink"], q, k, v, u, vn, x2d, p["ws"], p["bs"], p["ag"], p["sg"], p["wo"])


def _prep_params(norm1_g, w_in, q_norm_g, k_norm_g, attn_sink, sgu_ln_g, sgu_ln_b, w_spatial, b_spatial,
                 attn_out_g, sgu_out_g, w_out):
    head = jnp.arange(QK_WIDTH) // HEAD_DIM
    return dict(
        g1=norm1_g.reshape(1, D_MODEL),
        w_in=w_in.astype(BF16),
        qkg=jnp.concatenate([jnp.tile(q_norm_g, N_Q_HEADS), jnp.tile(k_norm_g, N_KV_HEADS)]).reshape(1, QK_WIDTH),
        seg=(head[:, None] == head[None, :]).astype(BF16),
        lng=sgu_ln_g.reshape(1, SGU_WIDTH),
        lnb=sgu_ln_b.reshape(1, SGU_WIDTH),
        sink=attn_sink.astype(F32),
        ws=w_spatial.astype(BF16),
        bs=jnp.repeat(b_spatial.T, SGU_GROUP_DIM, axis=1),
        ag=attn_out_g.reshape(1, ATTN_WIDTH),
        sg=sgu_out_g.reshape(1, SGU_WIDTH),
        wo=w_out.astype(BF16),
    )


def _pack_bf16_pair(lo, hi):
    lo_b = lax.bitcast_convert_type(lo.astype(BF16).astype(F32), jnp.uint32) >> 16
    hi_b = lax.bitcast_convert_type(hi.astype(BF16).astype(F32), jnp.uint32) & jnp.uint32(0xFFFF0000)
    return hi_b | lo_b


def _unpack_bf16_pair(packed):
    lo = lax.bitcast_convert_type(packed << 16, F32).astype(BF16)
    hi = lax.bitcast_convert_type(packed & jnp.uint32(0xFFFF0000), F32).astype(BF16)
    return lo, hi


def _router_kernel(x_ref, g2_ref, wh_ref, wl_ref, br_ref, tri_ref, xn_ref, idx_ref, rank_ref, gate_ref, cnt_ref,
                   run_ref):
    @pl.when(pl.program_id(0) == 0)
    def _():
        run_ref[...] = jnp.zeros_like(run_ref)

    x = x_ref[...]
    xn = x * lax.rsqrt(jnp.mean(x * x, axis=-1, keepdims=True) + EPS) * g2_ref[...]
    xn_ref[...] = lax.bitcast_convert_type(_pack_bf16_pair(xn[:, :D_MODEL // 2], xn[:, D_MODEL // 2:]), jnp.int32)

    xh = xn.astype(BF16)
    xl = (xn - xh.astype(F32)).astype(BF16)
    nt = (((1,), (1,)), ((), ()))
    logits = (lax.dot_general(wh_ref[...], xh, nt, preferred_element_type=F32)
              + lax.dot_general(wh_ref[...], xl, nt, preferred_element_type=F32)
              + lax.dot_general(wl_ref[...], xh, nt, preferred_element_type=F32)) + br_ref[...]
    rows = logits.shape[1]
    erow = lax.broadcasted_iota(jnp.int32, (N_EXPERTS, rows), 0)
    work = logits
    vals, sels = [], []
    for k in range(TOP_K):
        m = jnp.max(work, axis=0, keepdims=True)
        ik = jnp.min(jnp.where(work == m, erow, N_EXPERTS), axis=0, keepdims=True)
        sel = erow == ik
        idx_ref[k:k + 1, :] = ik
        vals.append(m)
        sels.append(sel)
        work = jnp.where(sel, -jnp.inf, work)

    exps = [jnp.exp(v - vals[0]) for v in vals]
    den = exps[0] + exps[1] + exps[2] + exps[3]
    gate_ref[...] = jnp.zeros_like(gate_ref)
    for k in range(TOP_K):
        gate_ref[k:k + 1, :] = exps[k] / den

    onehot = jnp.zeros((N_EXPERTS, rows), F32)
    for sel in sels:
        onehot = onehot + sel.astype(F32)
    before = jnp.dot(onehot.astype(BF16), tri_ref[...], preferred_element_type=F32) + run_ref[:, :1]
    for k in range(TOP_K):
        rank_ref[k:k + 1, :] = jnp.sum(jnp.where(sels[k], before, 0.0), axis=0, keepdims=True).astype(jnp.int32)
    run_ref[...] = run_ref[...] + jnp.sum(onehot, axis=1, keepdims=True)
    cnt_ref[...] = run_ref[...]


def _router(x2d, g2, wr_hi, wr_lo, br, tri):
    t = x2d.shape[0]
    rows = ROUTER_ROWS
    const = lambda i: (0, 0)
    return pl.pallas_call(
        _router_kernel,
        grid=(t // rows,),
        in_specs=[
            pl.BlockSpec((rows, D_MODEL), lambda i: (i, 0)),
            pl.BlockSpec((1, D_MODEL), const),
            pl.BlockSpec((N_EXPERTS, D_MODEL), const),
            pl.BlockSpec((N_EXPERTS, D_MODEL), const),
            pl.BlockSpec((N_EXPERTS, 1), const),
            pl.BlockSpec((rows, rows), const),
        ],
        out_specs=[
            pl.BlockSpec((rows, D_MODEL // 2), lambda i: (i, 0)),
            pl.BlockSpec((TOP_K, rows), lambda i: (0, i)),
            pl.BlockSpec((TOP_K, rows), lambda i: (0, i)),
            pl.BlockSpec((2 * TOP_K, rows), lambda i: (0, i)),
            pl.BlockSpec((N_EXPERTS, LANES), const),
        ],
        out_shape=[
            jax.ShapeDtypeStruct((t, D_MODEL // 2), jnp.int32),
            jax.ShapeDtypeStruct((TOP_K, t), jnp.int32),
            jax.ShapeDtypeStruct((TOP_K, t), jnp.int32),
            jax.ShapeDtypeStruct((2 * TOP_K, t), F32),
            jax.ShapeDtypeStruct((N_EXPERTS, LANES), F32),
        ],
        scratch_shapes=[pltpu.VMEM((N_EXPERTS, LANES), F32)],
        compiler_params=_cparams("arbitrary"),
        name="router",
    )(x2d, g2, wr_hi, wr_lo, br, tri)


def _dest_kernel(pstart_ref, idx_ref, rank_ref, dest_ref):
    idx = idx_ref[...]
    dest = rank_ref[...]
    for e in range(N_EXPERTS):
        dest = dest + jnp.where(idx == e, pstart_ref[e], 0)
    dest_ref[...] = dest


def _dest(pstart, idx, rank):
    t = idx.shape[1]
    rows = min(DEST_ROWS, t)
    blk =pl.BlockSpec((TOP_K, rows), lambda i, s: (0, i))
    grid_spec = pltpu.PrefetchScalarGridSpec(num_scalar_prefetch=1, grid=(t // rows,), in_specs=[blk, blk],
                                             out_specs=blk)
    return pl.pallas_call(
        _dest_kernel,
        grid_spec=grid_spec,
        out_shape=jax.ShapeDtypeStruct((TOP_K, t), jnp.int32),
        compiler_params=_cparams("parallel"),
        name="dest",
    )(pstart, idx, rank)


SC_CORES = 2
SC_SUBCORES = 16
SC_WORKERS = SC_CORES * SC_SUBCORES
SC_WINDOW = 128


def _sc_mesh():
    return plsc.VectorSubcoreMesh(core_axis_name="c", subcore_axis_name="s", num_cores=SC_CORES,
                                  num_subcores=SC_SUBCORES)


def _sc_worker():
    return lax.axis_index("s") * SC_CORES + lax.axis_index("c")


def _sc_scatter(rows, idx, cap):
    t, width = rows.shape
    n_idx = idx.shape[0]
    per_worker = t // SC_WORKERS
    assert per_worker * SC_WORKERS == t and per_worker % SC_WINDOW == 0
    idx_flat = idx.reshape(n_idx * t)

    @functools.partial(
        pl.kernel,
        mesh=_sc_mesh(),
        out_type=jax.ShapeDtypeStruct((cap, width), rows.dtype),
        scratch_types=[
            pltpu.VMEM((SC_WINDOW,), jnp.int32),
            pltpu.VMEM((SC_WINDOW, width), rows.dtype),
            pltpu.SemaphoreType.DMA,
        ],
        name="sc_scatter",
    )
    def scatter(rows_hbm, idx_hbm, out_hbm, idx_v, rows_v, sem):
        base = _sc_worker() * per_worker

        @pl.loop(0, per_worker // SC_WINDOW)
        def _(step):
            off = pl.multiple_of(base + step * SC_WINDOW, SC_WINDOW)
            pltpu.sync_copy(rows_hbm.at[pl.ds(off, SC_WINDOW)], rows_v)
            for k in range(n_idx):
                pltpu.sync_copy(idx_hbm.at[pl.ds(pl.multiple_of(k * t + off, SC_WINDOW), SC_WINDOW)], idx_v)
                pltpu.async_copy(rows_v, out_hbm.at[idx_v], sem).wait()

    return scatter(rows, idx_flat)


def _experts_kernel(blk_e_ref, blk_src_ref, blk_valid_ref, blk_first_ref, blk_next_ref, blk_slot_ref,
                    x_ref, w1_hbm, b1_ref, w2_hbm, b2_ref, o_ref, w1_buf, w2_buf, sems):
    del blk_src_ref
    b = pl.program_id(0)
    valid = blk_valid_ref[b]
    slot = blk_slot_ref[b]
    half = D_MODEL // 2

    def weight_copies(expert, s):
        return (pltpu.make_async_copy(w1_hbm.at[expert], w1_buf.at[s], sems.at[0, s]),
                pltpu.make_async_copy(w2_hbm.at[expert], w2_buf.at[s], sems.at[1, s]))

    @pl.when(b == 0)
    def _():
        for copy in weight_copies(blk_e_ref[0], slot):
            copy.start()

    @pl.when(blk_first_ref[b] == 1)
    def _():
        for copy in weight_copies(blk_e_ref[b], slot):
            copy.wait()

        @pl.when(blk_next_ref[b] >= 0)
        def _():
            for copy in weight_copies(blk_next_ref[b], 1 - slot):
                copy.start()

    w1_ref = w1_buf.at[slot]
    w2_ref = w2_buf.at[slot]

    def run(rows):
        row = lax.broadcasted_iota(jnp.int32, (rows, half), 0)
        x = jnp.where(row < valid, x_ref[:rows, :], 0)
        lo, hi = _unpack_bf16_pair(lax.bitcast_convert_type(x, jnp.uint32))
        h = (jnp.dot(lo, w1_ref[:half, :].astype(BF16), preferred_element_type=F32)
             + jnp.dot(hi, w1_ref[half:, :].astype(BF16), preferred_element_type=F32) + b1_ref[...])
        gate = jnp.minimum(h[:, :D_FF], SWIGLU_LIMIT)
        up = jnp.clip(h[:, D_FF:], -SWIGLU_LIMIT, SWIGLU_LIMIT)
        act = (up + 1.0) * (gate * jax.nn.sigmoid(gate * SWIGLU_ALPHA))
        o = jnp.dot(act.astype(BF16), w2_ref[...].astype(BF16), preferred_element_type=F32) + b2_ref[...]
        o_ref[:rows, :] = lax.bitcast_convert_type(_pack_bf16_pair(o[:, :half], o[:, half:]), jnp.int32)
        if rows < EXPERT_ROWS:
            o_ref[rows:, :] = jnp.zeros((EXPERT_ROWS - rows, half), jnp.int32)

    lower = 0
    for rows in EXPERT_ROW_STEPS:
        pl.when((valid > lower) & (valid <= rows))(functools.partial(run, rows))
        lower = rows

    @pl.when(valid == 0)
    def _():
        o_ref[...] = jnp.zeros_like(o_ref)


def _experts(blk_e, blk_src, blk_valid, blk_first, blk_next, blk_slot, xs, w1, b1, w2, b2):
    cap = xs.shape[0]
    rows = EXPERT_ROWS
    grid_spec = pltpu.PrefetchScalarGridSpec(
        num_scalar_prefetch=6,
        grid=(cap // rows,),
        in_specs=[
            pl.BlockSpec((rows, D_MODEL // 2), lambda b, be, bs, *_: (bs[b], 0)),
            pl.BlockSpec(memory_space=pl.ANY),
            pl.BlockSpec((None, 1, 2 * D_FF), lambda b, be, *_: (be[b], 0, 0)),
            pl.BlockSpec(memory_space=pl.ANY),
            pl.BlockSpec((None, 1, D_MODEL), lambda b, be, *_: (be[b], 0, 0)),
        ],
        out_specs=pl.BlockSpec((rows, D_MODEL // 2), lambda b, *_: (b, 0)),
        scratch_shapes=[
            pltpu.VMEM((2, D_MODEL, 2 * D_FF), F32),
            pltpu.VMEM((2, D_FF, D_MODEL), F32),
            pltpu.SemaphoreType.DMA((2, 2)),
        ],
    )
    return pl.pallas_call(
        _experts_kernel,
        grid_spec=grid_spec,
        out_shape=jax.ShapeDtypeStruct((cap, D_MODEL // 2), jnp.int32),
        compiler_params=_cparams("arbitrary"),
        name="experts",
    )(blk_e, blk_src, blk_valid, blk_first, blk_next, blk_slot, xs, w1, b1, w2, b2)


def _sc_gather(table, idx):
    n = idx.shape[0]
    width = table.shape[1]
    per_worker = n // SC_WORKERS
    assert per_worker * SC_WORKERS == n and per_worker % SC_WINDOW == 0

    @functools.partial(
        pl.kernel,
        mesh=_sc_mesh(),
        out_type=jax.ShapeDtypeStruct((n, width), table.dtype),
        scratch_types=[
            pltpu.VMEM((SC_WINDOW,), jnp.int32),
            pltpu.VMEM((SC_WINDOW, width), table.dtype),
            pltpu.SemaphoreType.DMA,
        ],
        name="sc_gather",
    )
    def gather(table_hbm, idx_hbm, out_hbm, idx_v, rows_v, sem):
        base = _sc_worker() * per_worker

        @pl.loop(0, per_worker // SC_WINDOW)
        def _(step):
            off = pl.multiple_of(base + step * SC_WINDOW, SC_WINDOW)
            pltpu.sync_copy(idx_hbm.at[pl.ds(off, SC_WINDOW)], idx_v)
            pltpu.async_copy(table_hbm.at[idx_v], rows_v, sem).wait()
            pltpu.sync_copy(rows_v, out_hbm.at[pl.ds(off, SC_WINDOW)])

    return gather(table, idx)


def _combine_kernel(gate_ref, x_ref, rows_ref, y_ref):
    gate_t = gate_ref[...].T
    half = D_MODEL // 2
    lo_sum = x_ref[:, :half]
    hi_sum = x_ref[:, half:]
    for k in range(TOP_K):
        packed = lax.bitcast_convert_type(rows_ref[k], jnp.uint32)
        g = gate_t[:, k:k + 1]
        lo_sum = lo_sum + g * lax.bitcast_convert_type(packed << 16, F32)
        hi_sum = hi_sum + g * lax.bitcast_convert_type(packed & jnp.uint32(0xFFFF0000), F32)
    y_ref[:, :half] = lo_sum
    y_ref[:, half:] = hi_sum


def _combine(gate, x2d, rows4):
    t = x2d.shape[0]
    rows = COMBINE_ROWS
    return pl.pallas_call(
        _combine_kernel,
        grid=(t // rows,),
        in_specs=[
            pl.BlockSpec((2 * TOP_K, rows), lambda i: (0, i)),
            pl.BlockSpec((rows, D_MODEL), lambda i: (i, 0)),
            pl.BlockSpec((TOP_K, rows, D_MODEL // 2), lambda i: (0, i, 0)),
        ],
        out_specs=pl.BlockSpec((rows, D_MODEL), lambda i: (i, 0)),
        out_shape=jax.ShapeDtypeStruct((t, D_MODEL), F32),
        compiler_params=_cparams("parallel"),
        name="combine",
    )(gate, x2d, rows4)


def _moe_half(x2d, m):
    t = x2d.shape[0]
    rows = EXPERT_ROWS
    cap = t * TOP_K + N_EXPERTS * rows
    n_blk = cap // rows
    xn, idx, rank, gate, cnt = _router(x2d, m["g2"], m["wr_hi"], m["wr_lo"], m["br"], m["tri"])

    counts = cnt[:, 0].astype(jnp.int32)
    padded = (counts + rows - 1) // rows * rows
    pends = jnp.cumsum(padded)
    pstart = pends - padded
    n_used = pends[-1:] // rows
    blk_src = jnp.minimum(jnp.arange(n_blk, dtype=jnp.int32), n_used - 1)
    starts = (blk_src * rows)[:, None]
    owner = (pstart[None, :] <= starts) & (starts < pends[None, :])
    blk_e = jnp.sum(jnp.where(owner, jnp.arange(N_EXPERTS)[None, :], 0), axis=1).astype(jnp.int32)
    filled_to = jnp.sum(jnp.where(owner, (pstart + counts)[None, :], 0), axis=1)
    blk_valid = jnp.clip(filled_to - blk_src * rows, 0, rows)
    in_use = jnp.arange(n_blk) < n_used
    blk_valid = jnp.where(in_use, blk_valid, 0).astype(jnp.int32)
    expert = jnp.arange(N_EXPERTS)
    has_rows = counts > 0
    later = jnp.where(has_rows[None, :] & (expert[None, :] > expert[:, None]), expert[None, :], N_EXPERTS)
    next_expert = jnp.min(later, axis=1)
    next_expert = jnp.where(next_expert == N_EXPERTS, -1, next_expert)
    expert_slot = (jnp.cumsum(has_rows) - 1) % 2
    blk_first = (in_use & (jnp.sum(jnp.where(owner, pstart[None, :], 0), axis=1) == blk_src * rows)).astype(jnp.int32)
    blk_next = jnp.sum(jnp.where(owner, next_expert[None, :], 0), axis=1).astype(jnp.int32)
    blk_slot = jnp.sum(jnp.where(owner, expert_slot[None, :], 0), axis=1).astype(jnp.int32)

    dest = _dest(pstart.astype(jnp.int32), idx, rank)
    xs = _sc_scatter(xn, dest, cap)
    out_sorted = _experts(blk_e, blk_src, blk_valid, blk_first, blk_next, blk_slot, xs, m["w1"], m["b1"], m["w2"],
                          m["b2"])
    rows4 = _sc_gather(out_sorted, dest.reshape(TOP_K * t))
    return _combine(gate, x2d, rows4.reshape(TOP_K, t, D_MODEL // 2))


def _prep_moe(norm2_g, w_router, b_router, w_moe_in, b_moe_in, w_moe_out, b_moe_out):
    r = jnp.arange(ROUTER_ROWS)
    wr_hi = w_router.T.astype(BF16)
    return dict(
        g2=norm2_g.reshape(1, D_MODEL),
        wr_hi=wr_hi,
        wr_lo=(w_router.T - wr_hi.astype(F32)).astype(BF16),
        br=b_router.reshape(N_EXPERTS, 1),
        tri=(r[:, None] < r[None, :]).astype(BF16),
        w1=w_moe_in,
        b1=b_moe_in.reshape(N_EXPERTS, 1, 2 * D_FF),
        w2=w_moe_out,
        b2=b_moe_out.reshape(N_EXPERTS, 1, D_MODEL),
    )


def kernel(x_prompt, x_sample, norm1_g, w_in, q_norm_g, k_norm_g, attn_sink, sgu_ln_g, sgu_ln_b, w_spatial,
           b_spatial, attn_out_g, sgu_out_g, w_out, norm2_g, w_router, b_router, w_moe_in, b_moe_in, w_moe_out,
           b_moe_out):
    p = _prep_params(norm1_g[0], w_in[0], q_norm_g[0], k_norm_g[0], attn_sink[0], sgu_ln_g[0], sgu_ln_b[0],
                     w_spatial[0], b_spatial[0], attn_out_g[0], sgu_out_g[0], w_out[0])
    m = _prep_moe(norm2_g[0], w_router[0], b_router[0], w_moe_in[0], b_moe_in[0], w_moe_out[0], b_moe_out[0])
    outs = []
    for x in (x_prompt, x_sample):
        x2 = _mix_half(x, p)
        outs.append(_moe_half(x2, m).reshape(x.shape))
    return tuple(outs)
```

```python
import functools

import jax
import jax.numpy as jnp
from jax import lax
from jax.experimental import pallas as pl
from jax.experimental.pallas import tpu as pltpu
from jax.experimental.pallas import tpu_sc as plsc

D_MODEL = 1024
HEAD_DIM = 64
N_Q_HEADS = 8
N_KV_HEADS = 2
Q_PER_KV = N_Q_HEADS // N_KV_HEADS
ATTN_WIDTH = N_Q_HEADS * HEAD_DIM
KV_WIDTH = N_KV_HEADS * HEAD_DIM
QK_WIDTH = ATTN_WIDTH + KV_WIDTH
KV_DUP_WIDTH = 2 * KV_WIDTH
N_SGU_GROUPS = 8
SGU_GROUP_DIM = 64
SGU_WIDTH = N_SGU_GROUPS * SGU_GROUP_DIM
IN_PROJ_WIDTH = ATTN_WIDTH + 2 * KV_WIDTH + 2 * SGU_WIDTH
BLOCK = 128
ROPE_THETA = 500000.0
ROPE_DIM = HEAD_DIM // 4
N_EXPERTS = 32
TOP_K = 4
D_FF = D_MODEL
SWIGLU_LIMIT = 7.0
SWIGLU_ALPHA = 1.702
EPS = 1e-6

LANES = 128
IN_PROJ_ROWS = 2048
IN_PROJ_CHUNK = 256
MIXER_ROWS = 1024
ROUTER_ROWS = 1024
DEST_ROWS = 2048
COMBINE_ROWS = 1024
EXPERT_ROWS = 1024
EXPERT_ROW_STEPS = (256, 512, 1024)
VMEM_LIMIT_BYTES = 56 * 1024 * 1024

F32 = jnp.float32
BF16 = jnp.bfloat16


def _cparams(*semantics):
    return pltpu.CompilerParams(dimension_semantics=semantics, vmem_limit_bytes=VMEM_LIMIT_BYTES)


def _dup_heads(tile):
    low = lax.broadcasted_iota(jnp.int32, tile.shape, 1) < HEAD_DIM
    swapped = pltpu.roll(tile, HEAD_DIM, axis=1)
    return jnp.where(low, tile, swapped), jnp.where(low, swapped, tile)


def _in_proj_kernel(x_ref, g1_ref, w_ref, qkg_ref, cos_ref, sina_ref, sinb_ref, seg_ref, lng_ref, lnb_ref,
                    q_ref, k_ref, v_ref, u_ref, vn_ref):
    for r0 in range(0, IN_PROJ_ROWS, IN_PROJ_CHUNK):
        rs = slice(r0, r0 + IN_PROJ_CHUNK)
        x = x_ref[rs, :]
        h = x * lax.rsqrt(jnp.mean(x * x, axis=-1, keepdims=True) + EPS) * g1_ref[...]
        z = jnp.dot(h.astype(BF16), w_ref[...], preferred_element_type=F32)

        qk = z[:, :QK_WIDTH]
        ss = jnp.dot((qk * qk).astype(BF16), seg_ref[...], preferred_element_type=F32)
        qkn = qk * lax.rsqrt(ss * (1.0 / HEAD_DIM) + EPS) * qkg_ref[...]
        cos, sina, sinb = cos_ref[rs, :], sina_ref[rs, :], sinb_ref[rs, :]
        for c in range(QK_WIDTH // LANES):
            xc = qkn[:, c * LANES:(c + 1) * LANES]
            up = pltpu.roll(xc, LANES - ROPE_DIM // 2, axis=1)
            dn = pltpu.roll(xc, ROPE_DIM // 2, axis=1)
            rc = xc * cos + up * sina + dn * sinb
            if c < ATTN_WIDTH // LANES:
                q_ref[rs, c * LANES:(c + 1) * LANES] = (rc * (HEAD_DIM ** -0.5)).astype(BF16)
            else:
                k0, k1 = _dup_heads(rc)
                k_ref[rs, :LANES] = k0.astype(BF16)
                k_ref[rs, LANES:] = k1.astype(BF16)

        v0, v1 = _dup_heads(z[:, QK_WIDTH:QK_WIDTH + KV_WIDTH])
        v_ref[rs, :LANES] = v0.astype(BF16)
        v_ref[rs, LANES:] = v1.astype(BF16)
        su = z[:, QK_WIDTH + KV_WIDTH:QK_WIDTH + KV_WIDTH + SGU_WIDTH]
        sv = z[:, QK_WIDTH + KV_WIDTH + SGU_WIDTH:]
        u_ref[rs, :] = jax.nn.gelu(su).astype(BF16)
        gv = jax.nn.gelu(sv)
        mu = jnp.mean(gv, axis=-1, keepdims=True)
        gc = gv - mu
        ln = gc * lax.rsqrt(jnp.mean(gc * gc, axis=-1, keepdims=True) + EPS) * lng_ref[...] + lnb_ref[...]
        vn_ref[rs, :] = ln.astype(BF16)


def _rope_tables(seq):
    half = ROPE_DIM // 2
    inv_freq = ROPE_THETA ** (-(jnp.arange(half, dtype=F32) * 2.0) / ROPE_DIM)
    ang = jnp.arange(seq).astype(F32)[:, None] * inv_freq[None, :]
    cos, sin = jnp.cos(ang), jnp.sin(ang)
    j = jnp.arange(LANES) % HEAD_DIM
    f = j % half
    cos_t = jnp.where(j[None, :] < ROPE_DIM, cos[:, f], 1.0)
    sina_t = jnp.where(j[None, :] < half, -sin[:, f], 0.0)
    sinb_t = jnp.where((j[None, :] >= half) & (j[None, :] < ROPE_DIM), sin[:, f], 0.0)
    return cos_t.astype(F32), sina_t.astype(F32), sinb_t.astype(F32)


def _in_proj(x2d, seq, g1, w_in, qkg, tables, seg, lng, lnb):
    t = x2d.shape[0]
    rows = IN_PROJ_ROWS
    n_seq = seq // rows
    const = lambda i: (0, 0)
    tab = pl.BlockSpec((rows, LANES), lambda i: (i % n_seq, 0))
    return pl.pallas_call(
        _in_proj_kernel,
        grid=(t // rows,),
        in_specs=[
            pl.BlockSpec((rows, D_MODEL), lambda i: (i, 0)),
            pl.BlockSpec((1, D_MODEL), const),
            pl.BlockSpec((D_MODEL, IN_PROJ_WIDTH), const),
            pl.BlockSpec((1, QK_WIDTH), const),
            tab, tab, tab,
            pl.BlockSpec((QK_WIDTH, QK_WIDTH), const),
            pl.BlockSpec((1, SGU_WIDTH), const),
            pl.BlockSpec((1, SGU_WIDTH), const),
        ],
        out_specs=[
            pl.BlockSpec((rows, ATTN_WIDTH), lambda i: (i, 0)),
            pl.BlockSpec((rows, KV_DUP_WIDTH), lambda i: (i, 0)),
            pl.BlockSpec((rows, KV_DUP_WIDTH), lambda i: (i, 0)),
            pl.BlockSpec((rows, SGU_WIDTH), lambda i: (i, 0)),
            pl.BlockSpec((rows, SGU_WIDTH), lambda i: (i, 0)),
        ],
        out_shape=[
            jax.ShapeDtypeStruct((t, ATTN_WIDTH), BF16),
            jax.ShapeDtypeStruct((t, KV_DUP_WIDTH), BF16),
            jax.ShapeDtypeStruct((t, KV_DUP_WIDTH), BF16),
            jax.ShapeDtypeStruct((t, SGU_WIDTH), BF16),
            jax.ShapeDtypeStruct((t, SGU_WIDTH), BF16),
        ],
        compiler_params=_cparams("parallel"),
        name="in_proj",
    )(x2d, g1, w_in, qkg, *tables, seg, lng, lnb)


def _mixer_kernel(sink_ref, q_ref, kp_ref, kc_ref, kn_ref, vp_ref, vc_ref, vx_ref, u_ref, g_ref, x_ref,
                  ws_ref, bs_ref, ag_ref, sg_ref, wo_ref, g2_ref, wh_ref, wl_ref, br_ref,
                  o_ref, xn_ref, logit_ref, mix_ref):
    i = pl.program_id(1)
    n_i = pl.num_programs(1)
    n_sub = MIXER_ROWS // BLOCK
    kwin = jnp.concatenate([kp_ref[...], kc_ref[...], kn_ref[...]], axis=0)
    vwin = jnp.concatenate([vp_ref[...], vc_ref[...], vx_ref[...]], axis=0)

    srows = Q_PER_KV * BLOCK
    r = lax.broadcasted_iota(jnp.int32, (srows, 3 * BLOCK), 0) & (BLOCK - 1)
    c = lax.broadcasted_iota(jnp.int32, (srows, 3 * BLOCK), 1)
    band = (c >= r) & (c <= r + 2 * BLOCK)
    hrow = lax.broadcasted_iota(jnp.int32, (srows, 1), 0) // BLOCK
    low = lax.broadcasted_iota(jnp.int32, (BLOCK, LANES), 1) < HEAD_DIM
    keep = (low.astype(BF16), (~low).astype(BF16))
    ones = jnp.ones((3 * BLOCK, LANES), BF16)

    for j in range(n_sub):
        valid = band
        if j == 0:
            valid = valid & ((c >= BLOCK) | (i > 0))
        if j == n_sub - 1:
            valid = valid & ((c < 2 * BLOCK) | (i < n_i - 1))
        kj = kwin[j * BLOCK:(j + 3) * BLOCK, :]
        vj = vwin[j * BLOCK:(j + 3) * BLOCK, :]
        a_tiles = []
        for hk in range(N_KV_HEADS):
            qs = jnp.concatenate(
                [q_ref[j * BLOCK:(j + 1) * BLOCK, (h // 2) * LANES:(h // 2 + 1) * LANES] * keep[h % 2]
                 for h in range(hk * Q_PER_KV, (hk + 1) * Q_PER_KV)], axis=0)
            kh = kj[:, hk * LANES:(hk + 1) * LANES]
            vh = vj[:, hk * LANES:(hk + 1) * LANES]
            s = lax.dot_general(qs, kh, (((1,), (1,)), ((), ())), preferred_element_type=F32)
            s = jnp.where(valid, s, -jnp.inf)
            sink = jnp.zeros((srows, 1), F32)
            for g in range(Q_PER_KV):
                sink = jnp.where(hrow == g, sink_ref[hk * Q_PER_KV + g], sink)
            m = jnp.maximum(jnp.max(s, axis=-1, keepdims=True), sink)
            p = jnp.exp(s - m).astype(BF16)
            ov = jnp.dot(p, jnp.concatenate([vh, ones], axis=-1), preferred_element_type=F32)
            o = ov[:, :LANES] / (ov[:, LANES:] + jnp.exp(sink - m))
            for g in range(0, Q_PER_KV, 2):
                a_tiles.append(jnp.where(low, o[g * BLOCK:(g + 1) * BLOCK, :], o[(g + 1) * BLOCK:(g + 2) * BLOCK, :]))
        a = jnp.concatenate(a_tiles, axis=-1)
        a = a * lax.rsqrt(jnp.mean(a * a, axis=-1, keepdims=True) + EPS) * ag_ref[...]

        mixed_tiles = []
        for t in range(SGU_WIDTH // LANES):
            vt = g_ref[j * BLOCK:(j + 1) * BLOCK, t * LANES:(t + 1) * LANES]
            mixed_tiles.append(jnp.where(low, jnp.dot(ws_ref[2 * t], vt, preferred_element_type=F32),
                                         jnp.dot(ws_ref[2 * t + 1], vt, preferred_element_type=F32)))
        mixed = jnp.concatenate(mixed_tiles, axis=-1) + bs_ref[...]
        gated = u_ref[j * BLOCK:(j + 1) * BLOCK, :].astype(F32) * mixed
        gated = gated * lax.rsqrt(jnp.mean(gated * gated, axis=-1, keepdims=True) + EPS) * sg_ref[...]
        mix_ref[j * BLOCK:(j + 1) * BLOCK, :] = jnp.concatenate([a, gated], axis=-1).astype(BF16)

    x2 = x_ref[...] + jnp.dot(mix_ref[...], wo_ref[...], preferred_element_type=F32)
    o_ref[...] = x2
    _moe_input(x2, g2_ref, wh_ref, wl_ref, br_ref, xn_ref, logit_ref)


def _mixer(batch, seq, sink, q, k, v, u, vn, x2d, ws, bs, ag, sg, wo, g2, wr_hi, wr_lo, br):
    rows = MIXER_ROWS
    n_i = seq // rows
    sub = rows // BLOCK
    n_blk = batch * seq // BLOCK
    const2 = lambda b, i, s: (0, 0)
    cur = lambda b, i, s: (b * n_i + i, 0)
    prv = lambda b, i, s: (jnp.maximum((b * n_i + i) * sub - 1, 0), 0)
    nxt = lambda b, i, s: (jnp.minimum((b * n_i + i + 1) * sub, n_blk - 1), 0)
    grid_spec = pltpu.PrefetchScalarGridSpec(
        num_scalar_prefetch=1,
        grid=(batch, n_i),
        in_specs=[
            pl.BlockSpec((rows, ATTN_WIDTH), cur),
            pl.BlockSpec((BLOCK, KV_DUP_WIDTH), prv),
            pl.BlockSpec((rows, KV_DUP_WIDTH), cur),
            pl.BlockSpec((BLOCK, KV_DUP_WIDTH), nxt),
            pl.BlockSpec((BLOCK, KV_DUP_WIDTH), prv),
            pl.BlockSpec((rows, KV_DUP_WIDTH), cur),
            pl.BlockSpec((BLOCK, KV_DUP_WIDTH), nxt),
            pl.BlockSpec((rows, SGU_WIDTH), cur),
            pl.BlockSpec((rows, SGU_WIDTH), cur),
            pl.BlockSpec((rows, D_MODEL), cur),
            pl.BlockSpec((N_SGU_GROUPS, BLOCK, BLOCK), lambda b, i, s: (0, 0, 0)),
            pl.BlockSpec((BLOCK, SGU_WIDTH), const2),
            pl.BlockSpec((1, ATTN_WIDTH), const2),
            pl.BlockSpec((1, SGU_WIDTH), const2),
            pl.BlockSpec((D_MODEL, D_MODEL), const2),
            pl.BlockSpec((1, D_MODEL), const2),
            pl.BlockSpec((N_EXPERTS, D_MODEL), const2),
            pl.BlockSpec((N_EXPERTS, D_MODEL), const2),
            pl.BlockSpec((N_EXPERTS, 1), const2),
        ],
        out_specs=[
            pl.BlockSpec((rows, D_MODEL), cur),
            pl.BlockSpec((rows, D_MODEL // 2), cur),
            pl.BlockSpec((N_EXPERTS, rows), lambda b, i, s: (0, b * n_i + i)),
        ],
        scratch_shapes=[pltpu.VMEM((rows, D_MODEL), BF16)],
    )
    t = batch * seq
    return pl.pallas_call(
        _mixer_kernel,
        grid_spec=grid_spec,
        out_shape=[
            jax.ShapeDtypeStruct((t, D_MODEL), F32),
            jax.ShapeDtypeStruct((t, D_MODEL // 2), jnp.int32),
            jax.ShapeDtypeStruct((N_EXPERTS, t), F32),
        ],
        compiler_params=_cparams("parallel", "parallel"),
        name="mixer",
    )(sink, q, k, k, k, v, v, v, u, vn, x2d, ws, bs, ag, sg, wo, g2, wr_hi, wr_lo, br)


def _mix_half(x, p, m):
    batch, seq, _ = x.shape
    x2d = x.reshape(batch * seq, D_MODEL)
    q, k, v, u, vn = _in_proj(x2d, seq, p["g1"], p["w_in"], p["qkg"], _rope_tables(seq), p["seg"], p["lng"],
                              p["lnb"])
    return _mixer(batch, seq, p["sink"], q, k, v, u, vn, x2d, p["ws"], p["bs"], p["ag"], p["sg"], p["wo"],
                  m["g2"], m["wr_hi"], m["wr_lo"], m["br"])


def _prep_params(norm1_g, w_in, q_norm_g, k_norm_g, attn_sink, sgu_ln_g, sgu_ln_b, w_spatial, b_spatial,
                 attn_out_g, sgu_out_g, w_out):
    head = jnp.arange(QK_WIDTH) // HEAD_DIM
    return dict(
        g1=norm1_g.reshape(1, D_MODEL),
        w_in=w_in.astype(BF16),
        qkg=jnp.concatenate([jnp.tile(q_norm_g, N_Q_HEADS), jnp.tile(k_norm_g, N_KV_HEADS)]).reshape(1, QK_WIDTH),
        seg=(head[:, None] == head[None, :]).astype(BF16),
        lng=sgu_ln_g.reshape(1, SGU_WIDTH),
        lnb=sgu_ln_b.reshape(1, SGU_WIDTH),
        sink=attn_sink.astype(F32),
        ws=w_spatial.astype(BF16),
        bs=jnp.repeat(b_spatial.T, SGU_GROUP_DIM, axis=1),
        ag=attn_out_g.reshape(1, ATTN_WIDTH),
        sg=sgu_out_g.reshape(1, SGU_WIDTH),
        wo=w_out.astype(BF16),
    )


def _pack_bf16_pair(lo, hi):
    lo_b = lax.bitcast_convert_type(lo.astype(BF16).astype(F32), jnp.uint32) >> 16
    hi_b = lax.bitcast_convert_type(hi.astype(BF16).astype(F32), jnp.uint32) & jnp.uint32(0xFFFF0000)
    return hi_b | lo_b


def _unpack_bf16_pair(packed):
    lo = lax.bitcast_convert_type(packed << 16, F32).astype(BF16)
    hi = lax.bitcast_convert_type(packed & jnp.uint32(0xFFFF0000), F32).astype(BF16)
    return lo, hi


def _moe_input(x, g2_ref, wh_ref, wl_ref, br_ref, xn_ref, logit_ref):
    xn = x * lax.rsqrt(jnp.mean(x * x, axis=-1, keepdims=True) + EPS) * g2_ref[...]
    xn_ref[...] = lax.bitcast_convert_type(_pack_bf16_pair(xn[:, :D_MODEL // 2], xn[:, D_MODEL // 2:]), jnp.int32)
    xh = xn.astype(BF16)
    xl = (xn - xh.astype(F32)).astype(BF16)
    nt = (((1,), (1,)), ((), ()))
    logit_ref[...] = (lax.dot_general(wh_ref[...], xh, nt, preferred_element_type=F32)
                      + lax.dot_general(wh_ref[...], xl, nt, preferred_element_type=F32)
                      + lax.dot_general(wl_ref[...], xh, nt, preferred_element_type=F32)) + br_ref[...]


def _router_kernel(logit_ref, tri_ref, idx_ref, rank_ref, gate_ref, cnt_ref, run_ref):
    @pl.when(pl.program_id(0) == 0)
    def _():
        run_ref[...] = jnp.zeros_like(run_ref)

    logits = logit_ref[...]
    rows = logits.shape[1]
    erow = lax.broadcasted_iota(jnp.int32, (N_EXPERTS, rows), 0)
    work = logits
    vals, sels = [], []
    for k in range(TOP_K):
        m = jnp.max(work, axis=0, keepdims=True)
        ik = jnp.min(jnp.where(work == m, erow, N_EXPERTS), axis=0, keepdims=True)
        sel = erow == ik
        idx_ref[k:k + 1, :] = ik
        vals.append(m)
        sels.append(sel)
        work = jnp.where(sel, -jnp.inf, work)

    exps = [jnp.exp(v - vals[0]) for v in vals]
    den = exps[0] + exps[1] + exps[2] + exps[3]
    gate_ref[...] = jnp.zeros_like(gate_ref)
    for k in range(TOP_K):
        gate_ref[k:k + 1, :] = exps[k] / den

    onehot = jnp.zeros((N_EXPERTS, rows), F32)
    for sel in sels:
        onehot = onehot + sel.astype(F32)
    before = jnp.dot(onehot.astype(BF16), tri_ref[...], preferred_element_type=F32) + run_ref[:, :1]
    for k in range(TOP_K):
        rank_ref[k:k + 1, :] = jnp.sum(jnp.where(sels[k], before, 0.0), axis=0, keepdims=True).astype(jnp.int32)
    run_ref[...] = run_ref[...] + jnp.sum(onehot, axis=1, keepdims=True)
    cnt_ref[...] = run_ref[...]


def _router(logits, tri):
    t = logits.shape[1]
    rows = ROUTER_ROWS
    const = lambda i: (0, 0)
    return pl.pallas_call(
        _router_kernel,
        grid=(t // rows,),
        in_specs=[
            pl.BlockSpec((N_EXPERTS, rows), lambda i: (0, i)),
            pl.BlockSpec((rows, rows), const),
        ],
        out_specs=[
            pl.BlockSpec((TOP_K, rows), lambda i: (0, i)),
            pl.BlockSpec((TOP_K, rows), lambda i: (0, i)),
            pl.BlockSpec((2 * TOP_K, rows), lambda i: (0, i)),
            pl.BlockSpec((N_EXPERTS, LANES), const),
        ],
        out_shape=[
            jax.ShapeDtypeStruct((TOP_K, t), jnp.int32),
            jax.ShapeDtypeStruct((TOP_K, t), jnp.int32),
            jax.ShapeDtypeStruct((2 * TOP_K, t), F32),
            jax.ShapeDtypeStruct((N_EXPERTS, LANES), F32),
        ],
        scratch_shapes=[pltpu.VMEM((N_EXPERTS, LANES), F32)],
        compiler_params=_cparams("arbitrary"),
        name="router",
    )(logits, tri)


def _dest_kernel(pstart_ref, idx_ref, rank_ref, dest_ref):
    idx = idx_ref[...]
    dest = rank_ref[...]
    for e in range(N_EXPERTS):
        dest = dest + jnp.where(idx == e, pstart_ref[e], 0)
    dest_ref[...] = dest


def _dest(pstart, idx, rank):
    t = idx.shape[1]
    rows = min(DEST_ROWS, t)
    blk =pl.BlockSpec((TOP_K, rows), lambda i, s: (0, i))
    grid_spec = pltpu.PrefetchScalarGridSpec(num_scalar_prefetch=1, grid=(t // rows,), in_specs=[blk, blk],
                                             out_specs=blk)
    return pl.pallas_call(
        _dest_kernel,
        grid_spec=grid_spec,
        out_shape=jax.ShapeDtypeStruct((TOP_K, t), jnp.int32),
        compiler_params=_cparams("parallel"),
        name="dest",
    )(pstart, idx, rank)


SC_CORES = 2
SC_SUBCORES = 16
SC_WORKERS = SC_CORES * SC_SUBCORES
SC_WINDOW = 128


def _sc_mesh():
    return plsc.VectorSubcoreMesh(core_axis_name="c", subcore_axis_name="s", num_cores=SC_CORES,
                                  num_subcores=SC_SUBCORES)


def _sc_worker():
    return lax.axis_index("s") * SC_CORES + lax.axis_index("c")


def _sc_scatter(rows, idx, cap):
    t, width = rows.shape
    n_idx = idx.shape[0]
    per_worker = t // SC_WORKERS
    assert per_worker * SC_WORKERS == t and per_worker % SC_WINDOW == 0
    idx_flat = idx.reshape(n_idx * t)

    @functools.partial(
        pl.kernel,
        mesh=_sc_mesh(),
        out_type=jax.ShapeDtypeStruct((cap, width), rows.dtype),
        scratch_types=[
            pltpu.VMEM((SC_WINDOW,), jnp.int32),
            pltpu.VMEM((SC_WINDOW, width), rows.dtype),
            pltpu.SemaphoreType.DMA,
        ],
        name="sc_scatter",
    )
    def scatter(rows_hbm, idx_hbm, out_hbm, idx_v, rows_v, sem):
        base = _sc_worker() * per_worker

        @pl.loop(0, per_worker // SC_WINDOW)
        def _(step):
            off = pl.multiple_of(base + step * SC_WINDOW, SC_WINDOW)
            pltpu.sync_copy(rows_hbm.at[pl.ds(off, SC_WINDOW)], rows_v)
            for k in range(n_idx):
                pltpu.sync_copy(idx_hbm.at[pl.ds(pl.multiple_of(k * t + off, SC_WINDOW), SC_WINDOW)], idx_v)
                pltpu.async_copy(rows_v, out_hbm.at[idx_v], sem).wait()

    return scatter(rows, idx_flat)


def _experts_kernel(blk_e_ref, blk_src_ref, blk_valid_ref, blk_first_ref, blk_next_ref, blk_slot_ref,
                    x_ref, w1_hbm, b1_ref, w2_hbm, b2_ref, o_ref, w1_buf, w2_buf, sems):
    del blk_src_ref
    b = pl.program_id(0)
    valid = blk_valid_ref[b]
    slot = blk_slot_ref[b]
    half = D_MODEL // 2

    def weight_copies(expert, s):
        return (pltpu.make_async_copy(w1_hbm.at[expert], w1_buf.at[s], sems.at[0, s]),
                pltpu.make_async_copy(w2_hbm.at[expert], w2_buf.at[s], sems.at[1, s]))

    @pl.when(b == 0)
    def _():
        for copy in weight_copies(blk_e_ref[0], slot):
            copy.start()

    @pl.when(blk_first_ref[b] == 1)
    def _():
        for copy in weight_copies(blk_e_ref[b], slot):
            copy.wait()

        @pl.when(blk_next_ref[b] >= 0)
        def _():
            for copy in weight_copies(blk_next_ref[b], 1 - slot):
                copy.start()

    w1_ref = w1_buf.at[slot]
    w2_ref = w2_buf.at[slot]

    def run(rows):
        row = lax.broadcasted_iota(jnp.int32, (rows, half), 0)
        x = jnp.where(row < valid, x_ref[:rows, :], 0)
        lo, hi = _unpack_bf16_pair(lax.bitcast_convert_type(x, jnp.uint32))
        h = (jnp.dot(lo, w1_ref[:half, :].astype(BF16), preferred_element_type=F32)
             + jnp.dot(hi, w1_ref[half:, :].astype(BF16), preferred_element_type=F32) + b1_ref[...])
        gate = jnp.minimum(h[:, :D_FF], SWIGLU_LIMIT)
        up = jnp.clip(h[:, D_FF:], -SWIGLU_LIMIT, SWIGLU_LIMIT)
        act = (up + 1.0) * (gate * jax.nn.sigmoid(gate * SWIGLU_ALPHA))
        o = jnp.dot(act.astype(BF16), w2_ref[...].astype(BF16), preferred_element_type=F32) + b2_ref[...]
        o_ref[:rows, :] = lax.bitcast_convert_type(_pack_bf16_pair(o[:, :half], o[:, half:]), jnp.int32)
        if rows < EXPERT_ROWS:
            o_ref[rows:, :] = jnp.zeros((EXPERT_ROWS - rows, half), jnp.int32)

    lower = 0
    for rows in EXPERT_ROW_STEPS:
        pl.when((valid > lower) & (valid <= rows))(functools.partial(run, rows))
        lower = rows

    @pl.when(valid == 0)
    def _():
        o_ref[...] = jnp.zeros_like(o_ref)


def _experts(blk_e, blk_src, blk_valid, blk_first, blk_next, blk_slot, xs, w1, b1, w2, b2):
    cap = xs.shape[0]
    rows = EXPERT_ROWS
    grid_spec = pltpu.PrefetchScalarGridSpec(
        num_scalar_prefetch=6,
        grid=(cap // rows,),
        in_specs=[
            pl.BlockSpec((rows, D_MODEL // 2), lambda b, be, bs, *_: (bs[b], 0)),
            pl.BlockSpec(memory_space=pl.ANY),
            pl.BlockSpec((None, 1, 2 * D_FF), lambda b, be, *_: (be[b], 0, 0)),
            pl.BlockSpec(memory_space=pl.ANY),
            pl.BlockSpec((None, 1, D_MODEL), lambda b, be, *_: (be[b], 0, 0)),
        ],
        out_specs=pl.BlockSpec((rows, D_MODEL // 2), lambda b, *_: (b, 0)),
        scratch_shapes=[
            pltpu.VMEM((2, D_MODEL, 2 * D_FF), F32),
            pltpu.VMEM((2, D_FF, D_MODEL), F32),
            pltpu.SemaphoreType.DMA((2, 2)),
        ],
    )
    return pl.pallas_call(
        _experts_kernel,
        grid_spec=grid_spec,
        out_shape=jax.ShapeDtypeStruct((cap, D_MODEL // 2), jnp.int32),
        compiler_params=_cparams("arbitrary"),
        name="experts",
    )(blk_e, blk_src, blk_valid, blk_first, blk_next, blk_slot, xs, w1, b1, w2, b2)


def _sc_gather(table, idx):
    n = idx.shape[0]
    width = table.shape[1]
    per_worker = n // SC_WORKERS
    assert per_worker * SC_WORKERS == n and per_worker % SC_WINDOW == 0

    @functools.partial(
        pl.kernel,
        mesh=_sc_mesh(),
        out_type=jax.ShapeDtypeStruct((n, width), table.dtype),
        scratch_types=[
            pltpu.VMEM((SC_WINDOW,), jnp.int32),
            pltpu.VMEM((SC_WINDOW, width), table.dtype),
            pltpu.SemaphoreType.DMA,
        ],
        name="sc_gather",
    )
    def gather(table_hbm, idx_hbm, out_hbm, idx_v, rows_v, sem):
        base = _sc_worker() * per_worker

        @pl.loop(0, per_worker // SC_WINDOW)
        def _(step):
            off = pl.multiple_of(base + step * SC_WINDOW, SC_WINDOW)
            pltpu.sync_copy(idx_hbm.at[pl.ds(off, SC_WINDOW)], idx_v)
            pltpu.async_copy(table_hbm.at[idx_v], rows_v, sem).wait()
            pltpu.sync_copy(rows_v, out_hbm.at[pl.ds(off, SC_WINDOW)])

    return gather(table, idx)


def _combine_kernel(gate_ref, x_ref, rows_ref, y_ref):
    gate_t = gate_ref[...].T
    half = D_MODEL // 2
    lo_sum = x_ref[:, :half]
    hi_sum = x_ref[:, half:]
    for k in range(TOP_K):
        packed = lax.bitcast_convert_type(rows_ref[k], jnp.uint32)
        g = gate_t[:, k:k + 1]
        lo_sum = lo_sum + g * lax.bitcast_convert_type(packed << 16, F32)
        hi_sum = hi_sum + g * lax.bitcast_convert_type(packed & jnp.uint32(0xFFFF0000), F32)
    y_ref[:, :half] = lo_sum
    y_ref[:, half:] = hi_sum


def _combine(gate, x2d, rows4):
    t = x2d.shape[0]
    rows = COMBINE_ROWS
    return pl.pallas_call(
        _combine_kernel,
        grid=(t // rows,),
        in_specs=[
            pl.BlockSpec((2 * TOP_K, rows), lambda i: (0, i)),
            pl.BlockSpec((rows, D_MODEL), lambda i: (i, 0)),
            pl.BlockSpec((TOP_K, rows, D_MODEL // 2), lambda i: (0, i, 0)),
        ],
        out_specs=pl.BlockSpec((rows, D_MODEL), lambda i: (i, 0)),
        out_shape=jax.ShapeDtypeStruct((t, D_MODEL), F32),
        compiler_params=_cparams("parallel"),
        name="combine",
    )(gate, x2d, rows4)


def _moe_half(x2d, xn, logits, m):
    t = x2d.shape[0]
    rows = EXPERT_ROWS
    cap = t * TOP_K + N_EXPERTS * rows
    n_blk = cap // rows
    idx, rank, gate, cnt = _router(logits, m["tri"])

    counts = cnt[:, 0].astype(jnp.int32)
    padded = (counts + rows - 1) // rows * rows
    pends = jnp.cumsum(padded)
    pstart = pends - padded
    n_used = pends[-1:] // rows
    blk_src = jnp.minimum(jnp.arange(n_blk, dtype=jnp.int32), n_used - 1)
    starts = (blk_src * rows)[:, None]
    owner = (pstart[None, :] <= starts) & (starts < pends[None, :])
    blk_e = jnp.sum(jnp.where(owner, jnp.arange(N_EXPERTS)[None, :], 0), axis=1).astype(jnp.int32)
    filled_to = jnp.sum(jnp.where(owner, (pstart + counts)[None, :], 0), axis=1)
    blk_valid = jnp.clip(filled_to - blk_src * rows, 0, rows)
    in_use = jnp.arange(n_blk) < n_used
    blk_valid = jnp.where(in_use, blk_valid, 0).astype(jnp.int32)
    expert = jnp.arange(N_EXPERTS)
    has_rows = counts > 0
    later = jnp.where(has_rows[None, :] & (expert[None, :] > expert[:, None]), expert[None, :], N_EXPERTS)
    next_expert = jnp.min(later, axis=1)
    next_expert = jnp.where(next_expert == N_EXPERTS, -1, next_expert)
    expert_slot = (jnp.cumsum(has_rows) - 1) % 2
    blk_first = (in_use & (jnp.sum(jnp.where(owner, pstart[None, :], 0), axis=1) == blk_src * rows)).astype(jnp.int32)
    blk_next = jnp.sum(jnp.where(owner, next_expert[None, :], 0), axis=1).astype(jnp.int32)
    blk_slot = jnp.sum(jnp.where(owner, expert_slot[None, :], 0), axis=1).astype(jnp.int32)

    dest = _dest(pstart.astype(jnp.int32), idx, rank)
    xs = _sc_scatter(xn, dest, cap)
    out_sorted = _experts(blk_e, blk_src, blk_valid, blk_first, blk_next, blk_slot, xs, m["w1"], m["b1"], m["w2"],
                          m["b2"])
    rows4 = _sc_gather(out_sorted, dest.reshape(TOP_K * t))
    return _combine(gate, x2d, rows4.reshape(TOP_K, t, D_MODEL // 2))


def _prep_moe(norm2_g, w_router, b_router, w_moe_in, b_moe_in, w_moe_out, b_moe_out):
    r = jnp.arange(ROUTER_ROWS)
    wr_hi = w_router.T.astype(BF16)
    return dict(
        g2=norm2_g.reshape(1, D_MODEL),
        wr_hi=wr_hi,
        wr_lo=(w_router.T - wr_hi.astype(F32)).astype(BF16),
        br=b_router.reshape(N_EXPERTS, 1),
        tri=(r[:, None] < r[None, :]).astype(BF16),
        w1=w_moe_in,
        b1=b_moe_in.reshape(N_EXPERTS, 1, 2 * D_FF),
        w2=w_moe_out,
        b2=b_moe_out.reshape(N_EXPERTS, 1, D_MODEL),
    )


def kernel(x_prompt, x_sample, norm1_g, w_in, q_norm_g, k_norm_g, attn_sink, sgu_ln_g, sgu_ln_b, w_spatial,
           b_spatial, attn_out_g, sgu_out_g, w_out, norm2_g, w_router, b_router, w_moe_in, b_moe_in, w_moe_out,
           b_moe_out):
    p = _prep_params(norm1_g[0], w_in[0], q_norm_g[0], k_norm_g[0], attn_sink[0], sgu_ln_g[0], sgu_ln_b[0],
                     w_spatial[0], b_spatial[0], attn_out_g[0], sgu_out_g[0], w_out[0])
    m = _prep_moe(norm2_g[0], w_router[0], b_router[0], w_moe_in[0], b_moe_in[0], w_moe_out[0], b_moe_out[0])
    outs = []
    for x in (x_prompt, x_sample):
        x2, xn, logits = _mix_half(x, p, m)
        outs.append(_moe_half(x2, xn, logits, m).reshape(x.shape))
    return tuple(outs)
```

```python
import functools
import math

import jax
import jax.numpy as jnp
from jax import lax
from jax.experimental import pallas as pl
from jax.experimental.pallas import tpu as pltpu
from jax.experimental.pallas import tpu_sc as plsc

D_MODEL = 1024
HEAD_DIM = 64
N_Q_HEADS = 8
N_KV_HEADS = 2
Q_PER_KV = N_Q_HEADS // N_KV_HEADS
ATTN_WIDTH = N_Q_HEADS * HEAD_DIM
KV_WIDTH = N_KV_HEADS * HEAD_DIM
QK_WIDTH = ATTN_WIDTH + KV_WIDTH
KV_DUP_WIDTH = 2 * KV_WIDTH
N_SGU_GROUPS = 8
SGU_GROUP_DIM = 64
SGU_WIDTH = N_SGU_GROUPS * SGU_GROUP_DIM
IN_PROJ_WIDTH = ATTN_WIDTH + 2 * KV_WIDTH + 2 * SGU_WIDTH
BLOCK = 128
ROPE_THETA = 500000.0
ROPE_DIM = HEAD_DIM // 4
N_EXPERTS = 32
TOP_K = 4
D_FF = D_MODEL
SWIGLU_LIMIT = 7.0
SWIGLU_ALPHA = 1.702
EPS = 1e-6
LOG2_E = 1.4426950408889634

LANES = 128
IN_PROJ_ROWS = 1024
IN_PROJ_CHUNK = 256
MIXER_ROWS = 1024
ROUTER_ROWS = 1024
DEST_ROWS = 2048
COMBINE_ROWS = 1024
EXPERT_ROWS = 1024
EXPERT_ROW_STEPS = (256, 512, 1024)
VMEM_LIMIT_BYTES = 56 * 1024 * 1024

F32 = jnp.float32
BF16 = jnp.bfloat16


def _cparams(*semantics):
    return pltpu.CompilerParams(dimension_semantics=semantics, vmem_limit_bytes=VMEM_LIMIT_BYTES)


def _dup_heads(tile):
    low = lax.broadcasted_iota(jnp.int32, tile.shape, 1) < HEAD_DIM
    swapped = pltpu.roll(tile, HEAD_DIM, axis=1)
    return jnp.where(low, tile, swapped), jnp.where(low, swapped, tile)


def _gelu_tanh(x):
    k = 2.0 * math.sqrt(2.0 / math.pi) * LOG2_E
    return x / (1.0 + jnp.exp2(x * (-k - (k * 0.044715) * (x * x))))


def _in_proj_kernel(x_ref, g1_ref, w_ref, qkg_ref, cos_ref, sina_ref, sinb_ref, seg_ref, lng_ref, lnb_ref,
                    q_ref, k_ref, v_ref, u_ref, vn_ref):
    for r0 in range(0, IN_PROJ_ROWS, IN_PROJ_CHUNK):
        rs = slice(r0, r0 + IN_PROJ_CHUNK)
        x = x_ref[rs, :]
        h = x * lax.rsqrt(jnp.mean(x * x, axis=-1, keepdims=True) + EPS) * g1_ref[...]
        z = jnp.dot(h.astype(BF16), w_ref[...], preferred_element_type=F32)

        qk = z[:, :QK_WIDTH]
        ss = jnp.dot((qk * qk).astype(BF16), seg_ref[...], preferred_element_type=F32)
        qkn = qk * lax.rsqrt(ss * (1.0 / HEAD_DIM) + EPS) * qkg_ref[...]
        cos, sina, sinb = cos_ref[rs, :], sina_ref[rs, :], sinb_ref[rs, :]
        for c in range(QK_WIDTH // LANES):
            xc = qkn[:, c * LANES:(c + 1) * LANES]
            up = pltpu.roll(xc, LANES - ROPE_DIM // 2, axis=1)
            dn = pltpu.roll(xc, ROPE_DIM // 2, axis=1)
            rc = xc * cos + up * sina + dn * sinb
            if c < ATTN_WIDTH // LANES:
                q_ref[rs, c * LANES:(c + 1) * LANES] = (rc * (HEAD_DIM ** -0.5 * LOG2_E)).astype(BF16)
            else:
                k0, k1 = _dup_heads(rc)
                k_ref[rs, :LANES] = k0.astype(BF16)
                k_ref[rs, LANES:] = k1.astype(BF16)

        v0, v1 = _dup_heads(z[:, QK_WIDTH:QK_WIDTH + KV_WIDTH])
        v_ref[rs, :LANES] = v0.astype(BF16)
        v_ref[rs, LANES:] = v1.astype(BF16)
        su = z[:, QK_WIDTH + KV_WIDTH:QK_WIDTH + KV_WIDTH + SGU_WIDTH]
        sv = z[:, QK_WIDTH + KV_WIDTH + SGU_WIDTH:]
        u_ref[rs, :] = _gelu_tanh(su).astype(BF16)
        gv = _gelu_tanh(sv)
        mu = jnp.mean(gv, axis=-1, keepdims=True)
        gc = gv - mu
        ln = gc * lax.rsqrt(jnp.mean(gc * gc, axis=-1, keepdims=True) + EPS) * lng_ref[...] + lnb_ref[...]
        vn_ref[rs, :] = ln.astype(BF16)


def _rope_tables(seq):
    half = ROPE_DIM // 2
    inv_freq = ROPE_THETA ** (-(jnp.arange(half, dtype=F32) * 2.0) / ROPE_DIM)
    ang = jnp.arange(seq).astype(F32)[:, None] * inv_freq[None, :]
    cos, sin = jnp.cos(ang), jnp.sin(ang)
    j = jnp.arange(LANES) % HEAD_DIM
    f = j % half
    cos_t = jnp.where(j[None, :] < ROPE_DIM, cos[:, f], 1.0)
    sina_t = jnp.where(j[None, :] < half, -sin[:, f], 0.0)
    sinb_t = jnp.where((j[None, :] >= half) & (j[None, :] < ROPE_DIM), sin[:, f], 0.0)
    return cos_t.astype(F32), sina_t.astype(F32), sinb_t.astype(F32)


def _in_proj(x2d, seq, g1, w_in, qkg, tables, seg, lng, lnb):
    t = x2d.shape[0]
    rows = IN_PROJ_ROWS
    n_seq = seq // rows
    const = lambda i: (0, 0)
    tab = pl.BlockSpec((rows, LANES), lambda i: (i % n_seq, 0))
    return pl.pallas_call(
        _in_proj_kernel,
        grid=(t // rows,),
        in_specs=[
            pl.BlockSpec((rows, D_MODEL), lambda i: (i, 0)),
            pl.BlockSpec((1, D_MODEL), const),
            pl.BlockSpec((D_MODEL, IN_PROJ_WIDTH), const),
            pl.BlockSpec((1, QK_WIDTH), const),
            tab, tab, tab,
            pl.BlockSpec((QK_WIDTH, QK_WIDTH), const),
            pl.BlockSpec((1, SGU_WIDTH), const),
            pl.BlockSpec((1, SGU_WIDTH), const),
        ],
        out_specs=[
            pl.BlockSpec((rows, ATTN_WIDTH), lambda i: (i, 0)),
            pl.BlockSpec((rows, KV_DUP_WIDTH), lambda i: (i, 0)),
            pl.BlockSpec((rows, KV_DUP_WIDTH), lambda i: (i, 0)),
            pl.BlockSpec((rows, SGU_WIDTH), lambda i: (i, 0)),
            pl.BlockSpec((rows, SGU_WIDTH), lambda i: (i, 0)),
        ],
        out_shape=[
            jax.ShapeDtypeStruct((t, ATTN_WIDTH), BF16),
            jax.ShapeDtypeStruct((t, KV_DUP_WIDTH), BF16),
            jax.ShapeDtypeStruct((t, KV_DUP_WIDTH), BF16),
            jax.ShapeDtypeStruct((t, SGU_WIDTH), BF16),
            jax.ShapeDtypeStruct((t, SGU_WIDTH), BF16),
        ],
        compiler_params=_cparams("parallel"),
        name="in_proj",
    )(x2d, g1, w_in, qkg, *tables, seg, lng, lnb)


def _mixer_kernel(sink_ref, q_ref, kp_ref, kc_ref, kn_ref, vp_ref, vc_ref, vx_ref, u_ref, g_ref, x_ref,
                  ws_ref, bs_ref, ag_ref, sg_ref, wo_ref, o_ref, mix_ref):
    i = pl.program_id(1)
    n_i = pl.num_programs(1)
    n_sub = MIXER_ROWS // BLOCK
    kwin = jnp.concatenate([kp_ref[...], kc_ref[...], kn_ref[...]], axis=0)
    vwin = jnp.concatenate([vp_ref[...], vc_ref[...], vx_ref[...]], axis=0)

    srows = Q_PER_KV * BLOCK
    r = lax.broadcasted_iota(jnp.int32, (srows, 3 * BLOCK), 0) & (BLOCK - 1)
    c = lax.broadcasted_iota(jnp.int32, (srows, 3 * BLOCK), 1)
    band = (c >= r) & (c <= r + 2 * BLOCK)
    hrow = lax.broadcasted_iota(jnp.int32, (srows, 1), 0) // BLOCK
    low = lax.broadcasted_iota(jnp.int32, (BLOCK, LANES), 1) < HEAD_DIM
    keep = (low.astype(BF16), (~low).astype(BF16))
    ones = jnp.ones((3 * BLOCK, LANES), BF16)

    for j in range(n_sub):
        valid = band
        if j == 0:
            valid = valid & ((c >= BLOCK) | (i > 0))
        if j == n_sub - 1:
            valid = valid & ((c < 2 * BLOCK) | (i < n_i - 1))
        kj = kwin[j * BLOCK:(j + 3) * BLOCK, :]
        vj = vwin[j * BLOCK:(j + 3) * BLOCK, :]
        a_tiles = []
        for hk in range(N_KV_HEADS):
            qs = jnp.concatenate(
                [q_ref[j * BLOCK:(j + 1) * BLOCK, (h // 2) * LANES:(h // 2 + 1) * LANES] * keep[h % 2]
                 for h in range(hk * Q_PER_KV, (hk + 1) * Q_PER_KV)], axis=0)
            kh = kj[:, hk * LANES:(hk + 1) * LANES]
            vh = vj[:, hk * LANES:(hk + 1) * LANES]
            s = lax.dot_general(qs, kh, (((1,), (1,)), ((), ())), preferred_element_type=F32)
            s = jnp.where(valid, s, -jnp.inf)
            sink = jnp.zeros((srows, 1), F32)
            for g in range(Q_PER_KV):
                sink = jnp.where(hrow == g, sink_ref[hk * Q_PER_KV + g] * LOG2_E, sink)
            m = jnp.maximum(jnp.max(s, axis=-1, keepdims=True), sink)
            p = jnp.exp2(s - m).astype(BF16)
            ov = jnp.dot(p, jnp.concatenate([vh, ones], axis=-1), preferred_element_type=F32)
            o = ov[:, :LANES] / (ov[:, LANES:] + jnp.exp2(sink - m))
            for g in range(0, Q_PER_KV, 2):
                a_tiles.append(jnp.where(low, o[g * BLOCK:(g + 1) * BLOCK, :], o[(g + 1) * BLOCK:(g + 2) * BLOCK, :]))
        a = jnp.concatenate(a_tiles, axis=-1)
        a = a * lax.rsqrt(jnp.mean(a * a, axis=-1, keepdims=True) + EPS) * ag_ref[...]

        mixed_tiles = []
        for t in range(SGU_WIDTH // LANES):
            vt = g_ref[j * BLOCK:(j + 1) * BLOCK, t * LANES:(t + 1) * LANES]
            mixed_tiles.append(jnp.where(low, jnp.dot(ws_ref[2 * t], vt, preferred_element_type=F32),
                                         jnp.dot(ws_ref[2 * t + 1], vt, preferred_element_type=F32)))
        mixed = jnp.concatenate(mixed_tiles, axis=-1) + bs_ref[...]
        gated = u_ref[j * BLOCK:(j + 1) * BLOCK, :].astype(F32) * mixed
        gated = gated * lax.rsqrt(jnp.mean(gated * gated, axis=-1, keepdims=True) + EPS) * sg_ref[...]
        mix_ref[j * BLOCK:(j + 1) * BLOCK, :] = jnp.concatenate([a, gated], axis=-1).astype(BF16)

    o_ref[...] = x_ref[...] + jnp.dot(mix_ref[...], wo_ref[...], preferred_element_type=F32)


def _mixer(batch, seq, sink, q, k, v, u, vn, x2d, ws, bs, ag, sg, wo):
    rows = MIXER_ROWS
    n_i = seq // rows
    sub = rows // BLOCK
    n_blk = batch * seq // BLOCK
    const2 = lambda b, i, s: (0, 0)
    cur = lambda b, i, s: (b * n_i + i, 0)
    prv = lambda b, i, s: (jnp.maximum((b * n_i + i) * sub - 1, 0), 0)
    nxt = lambda b, i, s: (jnp.minimum((b * n_i + i + 1) * sub, n_blk - 1), 0)
    grid_spec = pltpu.PrefetchScalarGridSpec(
        num_scalar_prefetch=1,
        grid=(batch, n_i),
        in_specs=[
            pl.BlockSpec((rows, ATTN_WIDTH), cur),
            pl.BlockSpec((BLOCK, KV_DUP_WIDTH), prv),
            pl.BlockSpec((rows, KV_DUP_WIDTH), cur),
            pl.BlockSpec((BLOCK, KV_DUP_WIDTH), nxt),
            pl.BlockSpec((BLOCK, KV_DUP_WIDTH), prv),
            pl.BlockSpec((rows, KV_DUP_WIDTH), cur),
            pl.BlockSpec((BLOCK, KV_DUP_WIDTH), nxt),
            pl.BlockSpec((rows, SGU_WIDTH), cur),
            pl.BlockSpec((rows, SGU_WIDTH), cur),
            pl.BlockSpec((rows, D_MODEL), cur),
            pl.BlockSpec((N_SGU_GROUPS, BLOCK, BLOCK), lambda b, i, s: (0, 0, 0)),
            pl.BlockSpec((BLOCK, SGU_WIDTH), const2),
            pl.BlockSpec((1, ATTN_WIDTH), const2),
            pl.BlockSpec((1, SGU_WIDTH), const2),
            pl.BlockSpec((D_MODEL, D_MODEL), const2),
        ],
        out_specs=pl.BlockSpec((rows, D_MODEL), cur),
        scratch_shapes=[pltpu.VMEM((rows, D_MODEL), BF16)],
    )
    return pl.pallas_call(
        _mixer_kernel,
        grid_spec=grid_spec,
        out_shape=jax.ShapeDtypeStruct((batch * seq, D_MODEL), F32),
        compiler_params=_cparams("parallel", "parallel"),
        name="mixer",
    )(sink, q, k, k, k, v, v, v, u, vn, x2d, ws, bs, ag, sg, wo)


def _mix_half(x, p):
    batch, seq, _ = x.shape
    x2d = x.reshape(batch * seq, D_MODEL)
    q, k, v, u, vn = _in_proj(x2d, seq, p["g1"], p["w_in"], p["qkg"], _rope_tables(seq), p["seg"], p["lng"],
                              p["lnb"])
    return _mixer(batch, seq, p["sink"], q, k, v, u, vn, x2d, p["ws"], p["bs"], p["ag"], p["sg"], p["wo"])


def _prep_params(norm1_g, w_in, q_norm_g, k_norm_g, attn_sink, sgu_ln_g, sgu_ln_b, w_spatial, b_spatial,
                 attn_out_g, sgu_out_g, w_out):
    head = jnp.arange(QK_WIDTH) // HEAD_DIM
    return dict(
        g1=norm1_g.reshape(1, D_MODEL),
        w_in=w_in.astype(BF16),
        qkg=jnp.concatenate([jnp.tile(q_norm_g, N_Q_HEADS), jnp.tile(k_norm_g, N_KV_HEADS)]).reshape(1, QK_WIDTH),
        seg=(head[:, None] == head[None, :]).astype(BF16),
        lng=sgu_ln_g.reshape(1, SGU_WIDTH),
        lnb=sgu_ln_b.reshape(1, SGU_WIDTH),
        sink=attn_sink.astype(F32),
        ws=w_spatial.astype(BF16),
        bs=jnp.repeat(b_spatial.T, SGU_GROUP_DIM, axis=1),
        ag=attn_out_g.reshape(1, ATTN_WIDTH),
        sg=sgu_out_g.reshape(1, SGU_WIDTH),
        wo=w_out.astype(BF16),
    )


def _pack_bf16_pair(lo, hi):
    lo_b = lax.bitcast_convert_type(lo.astype(BF16).astype(F32), jnp.uint32) >> 16
    hi_b = lax.bitcast_convert_type(hi.astype(BF16).astype(F32), jnp.uint32) & jnp.uint32(0xFFFF0000)
    return hi_b | lo_b


def _unpack_bf16_pair(packed):
    lo = lax.bitcast_convert_type(packed << 16, F32).astype(BF16)
    hi = lax.bitcast_convert_type(packed & jnp.uint32(0xFFFF0000), F32).astype(BF16)
    return lo, hi


def _router_kernel(x_ref, g2_ref, wh_ref, wl_ref, br_ref, tri_ref, xn_ref, idx_ref, rank_ref, gate_ref, cnt_ref,
                   run_ref):
    @pl.when(pl.program_id(0) == 0)
    def _():
        run_ref[...] = jnp.zeros_like(run_ref)

    x = x_ref[...]
    xn = x * lax.rsqrt(jnp.mean(x * x, axis=-1, keepdims=True) + EPS) * g2_ref[...]
    xn_ref[...] = lax.bitcast_convert_type(_pack_bf16_pair(xn[:, :D_MODEL // 2], xn[:, D_MODEL // 2:]), jnp.int32)

    xh = xn.astype(BF16)
    xl = (xn - xh.astype(F32)).astype(BF16)
    nt = (((1,), (1,)), ((), ()))
    logits = (lax.dot_general(wh_ref[...], xh, nt, preferred_element_type=F32)
              + lax.dot_general(wh_ref[...], xl, nt, preferred_element_type=F32)
              + lax.dot_general(wl_ref[...], xh, nt, preferred_element_type=F32)) + br_ref[...]
    rows = logits.shape[1]
    erow = lax.broadcasted_iota(jnp.int32, (N_EXPERTS, rows), 0)
    work = logits
    vals, sels = [], []
    for k in range(TOP_K):
        m = jnp.max(work, axis=0, keepdims=True)
        ik = jnp.min(jnp.where(work == m, erow, N_EXPERTS), axis=0, keepdims=True)
        sel = erow == ik
        idx_ref[k:k + 1, :] = ik
        vals.append(m)
        sels.append(sel)
        work = jnp.where(sel, -jnp.inf, work)

    exps = [jnp.exp(v - vals[0]) for v in vals]
    den = exps[0] + exps[1] + exps[2] + exps[3]
    gate_ref[...] = jnp.zeros_like(gate_ref)
    for k in range(TOP_K):
        gate_ref[k:k + 1, :] = exps[k] / den

    onehot = jnp.zeros((N_EXPERTS, rows), F32)
    for sel in sels:
        onehot = onehot + sel.astype(F32)
    before = jnp.dot(onehot.astype(BF16), tri_ref[...], preferred_element_type=F32) + run_ref[:, :1]
    for k in range(TOP_K):
        rank_ref[k:k + 1, :] = jnp.sum(jnp.where(sels[k], before, 0.0), axis=0, keepdims=True).astype(jnp.int32)
    run_ref[...] = run_ref[...] + jnp.sum(onehot, axis=1, keepdims=True)
    cnt_ref[...] = run_ref[...]


def _router(x2d, g2, wr_hi, wr_lo, br, tri):
    t = x2d.shape[0]
    rows = ROUTER_ROWS
    const = lambda i: (0, 0)
    return pl.pallas_call(
        _router_kernel,
        grid=(t // rows,),
        in_specs=[
            pl.BlockSpec((rows, D_MODEL), lambda i: (i, 0)),
            pl.BlockSpec((1, D_MODEL), const),
            pl.BlockSpec((N_EXPERTS, D_MODEL), const),
            pl.BlockSpec((N_EXPERTS, D_MODEL), const),
            pl.BlockSpec((N_EXPERTS, 1), const),
            pl.BlockSpec((rows, rows), const),
        ],
        out_specs=[
            pl.BlockSpec((rows, D_MODEL // 2), lambda i: (i, 0)),
            pl.BlockSpec((TOP_K, rows), lambda i: (0, i)),
            pl.BlockSpec((TOP_K, rows), lambda i: (0, i)),
            pl.BlockSpec((2 * TOP_K, rows), lambda i: (0, i)),
            pl.BlockSpec((N_EXPERTS, LANES), const),
        ],
        out_shape=[
            jax.ShapeDtypeStruct((t, D_MODEL // 2), jnp.int32),
            jax.ShapeDtypeStruct((TOP_K, t), jnp.int32),
            jax.ShapeDtypeStruct((TOP_K, t), jnp.int32),
            jax.ShapeDtypeStruct((2 * TOP_K, t), F32),
            jax.ShapeDtypeStruct((N_EXPERTS, LANES), F32),
        ],
        scratch_shapes=[pltpu.VMEM((N_EXPERTS, LANES), F32)],
        compiler_params=_cparams("arbitrary"),
        name="router",
    )(x2d, g2, wr_hi, wr_lo, br, tri)


def _dest_kernel(pstart_ref, idx_ref, rank_ref, dest_ref):
    idx = idx_ref[...]
    dest = rank_ref[...]
    for e in range(N_EXPERTS):
        dest = dest + jnp.where(idx == e, pstart_ref[e], 0)
    dest_ref[...] = dest


def _dest(pstart, idx, rank):
    t = idx.shape[1]
    rows = min(DEST_ROWS, t)
    blk =pl.BlockSpec((TOP_K, rows), lambda i, s: (0, i))
    grid_spec = pltpu.PrefetchScalarGridSpec(num_scalar_prefetch=1, grid=(t // rows,), in_specs=[blk, blk],
                                             out_specs=blk)
    return pl.pallas_call(
        _dest_kernel,
        grid_spec=grid_spec,
        out_shape=jax.ShapeDtypeStruct((TOP_K, t), jnp.int32),
        compiler_params=_cparams("parallel"),
        name="dest",
    )(pstart, idx, rank)


SC_CORES = 2
SC_SUBCORES = 16
SC_WORKERS = SC_CORES * SC_SUBCORES
SC_WINDOW = 128


def _sc_mesh():
    return plsc.VectorSubcoreMesh(core_axis_name="c", subcore_axis_name="s", num_cores=SC_CORES,
                                  num_subcores=SC_SUBCORES)


def _sc_worker():
    return lax.axis_index("s") * SC_CORES + lax.axis_index("c")


def _sc_scatter(rows, idx, cap):
    t, width = rows.shape
    n_idx = idx.shape[0]
    per_worker = t // SC_WORKERS
    assert per_worker * SC_WORKERS == t and per_worker % SC_WINDOW == 0
    idx_flat = idx.reshape(n_idx * t)

    @functools.partial(
        pl.kernel,
        mesh=_sc_mesh(),
        out_type=jax.ShapeDtypeStruct((cap, width), rows.dtype),
        scratch_types=[
            pltpu.VMEM((SC_WINDOW,), jnp.int32),
            pltpu.VMEM((SC_WINDOW, width), rows.dtype),
            pltpu.SemaphoreType.DMA,
        ],
        name="sc_scatter",
    )
    def scatter(rows_hbm, idx_hbm, out_hbm, idx_v, rows_v, sem):
        base = _sc_worker() * per_worker

        @pl.loop(0, per_worker // SC_WINDOW)
        def _(step):
            off = pl.multiple_of(base + step * SC_WINDOW, SC_WINDOW)
            pltpu.sync_copy(rows_hbm.at[pl.ds(off, SC_WINDOW)], rows_v)
            for k in range(n_idx):
                pltpu.sync_copy(idx_hbm.at[pl.ds(pl.multiple_of(k * t + off, SC_WINDOW), SC_WINDOW)], idx_v)
                pltpu.async_copy(rows_v, out_hbm.at[idx_v], sem).wait()

    return scatter(rows, idx_flat)


def _experts_kernel(blk_e_ref, blk_src_ref, blk_valid_ref, blk_first_ref, blk_next_ref, blk_slot_ref,
                    x_ref, w1_hbm, b1_ref, w2_hbm, b2_ref, o_ref, w1_buf, w2_buf, sems):
    del blk_src_ref
    b = pl.program_id(0)
    valid = blk_valid_ref[b]
    slot = blk_slot_ref[b]
    half = D_MODEL // 2

    def weight_copies(expert, s):
        return (pltpu.make_async_copy(w1_hbm.at[expert], w1_buf.at[s], sems.at[0, s]),
                pltpu.make_async_copy(w2_hbm.at[expert], w2_buf.at[s], sems.at[1, s]))

    @pl.when(b == 0)
    def _():
        for copy in weight_copies(blk_e_ref[0], slot):
            copy.start()

    @pl.when(blk_first_ref[b] == 1)
    def _():
        for copy in weight_copies(blk_e_ref[b], slot):
            copy.wait()

        @pl.when(blk_next_ref[b] >= 0)
        def _():
            for copy in weight_copies(blk_next_ref[b], 1 - slot):
                copy.start()

    w1_ref = w1_buf.at[slot]
    w2_ref = w2_buf.at[slot]

    def run(rows):
        row = lax.broadcasted_iota(jnp.int32, (rows, half), 0)
        x = jnp.where(row < valid, x_ref[:rows, :], 0)
        lo, hi = _unpack_bf16_pair(lax.bitcast_convert_type(x, jnp.uint32))
        h = (jnp.dot(lo, w1_ref[:half, :].astype(BF16), preferred_element_type=F32)
             + jnp.dot(hi, w1_ref[half:, :].astype(BF16), preferred_element_type=F32) + b1_ref[...])
        gate = jnp.minimum(h[:, :D_FF], SWIGLU_LIMIT)
        up = jnp.clip(h[:, D_FF:], -SWIGLU_LIMIT, SWIGLU_LIMIT)
        act = (up + 1.0) * (gate / (1.0 + jnp.exp2(gate * (-SWIGLU_ALPHA * LOG2_E))))
        o = jnp.dot(act.astype(BF16), w2_ref[...].astype(BF16), preferred_element_type=F32) + b2_ref[...]
        o_ref[:rows, :] = lax.bitcast_convert_type(_pack_bf16_pair(o[:, :half], o[:, half:]), jnp.int32)
        if rows < EXPERT_ROWS:
            o_ref[rows:, :] = jnp.zeros((EXPERT_ROWS - rows, half), jnp.int32)

    lower = 0
    for rows in EXPERT_ROW_STEPS:
        pl.when((valid > lower) & (valid <= rows))(functools.partial(run, rows))
        lower = rows

    @pl.when(valid == 0)
    def _():
        o_ref[...] = jnp.zeros_like(o_ref)


def _experts(blk_e, blk_src, blk_valid, blk_first, blk_next, blk_slot, xs, w1, b1, w2, b2):
    cap = xs.shape[0]
    rows = EXPERT_ROWS
    grid_spec = pltpu.PrefetchScalarGridSpec(
        num_scalar_prefetch=6,
        grid=(cap // rows,),
        in_specs=[
            pl.BlockSpec((rows, D_MODEL // 2), lambda b, be, bs, *_: (bs[b], 0)),
            pl.BlockSpec(memory_space=pl.ANY),
            pl.BlockSpec((None, 1, 2 * D_FF), lambda b, be, *_: (be[b], 0, 0)),
            pl.BlockSpec(memory_space=pl.ANY),
            pl.BlockSpec((None, 1, D_MODEL), lambda b, be, *_: (be[b], 0, 0)),
        ],
        out_specs=pl.BlockSpec((rows, D_MODEL // 2), lambda b, *_: (b, 0)),
        scratch_shapes=[
            pltpu.VMEM((2, D_MODEL, 2 * D_FF), F32),
            pltpu.VMEM((2, D_FF, D_MODEL), F32),
            pltpu.SemaphoreType.DMA((2, 2)),
        ],
    )
    return pl.pallas_call(
        _experts_kernel,
        grid_spec=grid_spec,
        out_shape=jax.ShapeDtypeStruct((cap, D_MODEL // 2), jnp.int32),
        compiler_params=_cparams("arbitrary"),
        name="experts",
    )(blk_e, blk_src, blk_valid, blk_first, blk_next, blk_slot, xs, w1, b1, w2, b2)


def _sc_gather(table, idx):
    n = idx.shape[0]
    width = table.shape[1]
    per_worker = n // SC_WORKERS
    assert per_worker * SC_WORKERS == n and per_worker % SC_WINDOW == 0

    @functools.partial(
        pl.kernel,
        mesh=_sc_mesh(),
        out_type=jax.ShapeDtypeStruct((n, width), table.dtype),
        scratch_types=[
            pltpu.VMEM((SC_WINDOW,), jnp.int32),
            pltpu.VMEM((SC_WINDOW, width), table.dtype),
            pltpu.SemaphoreType.DMA,
        ],
        name="sc_gather",
    )
    def gather(table_hbm, idx_hbm, out_hbm, idx_v, rows_v, sem):
        base = _sc_worker() * per_worker

        @pl.loop(0, per_worker // SC_WINDOW)
        def _(step):
            off = pl.multiple_of(base + step * SC_WINDOW, SC_WINDOW)
            pltpu.sync_copy(idx_hbm.at[pl.ds(off, SC_WINDOW)], idx_v)
            pltpu.async_copy(table_hbm.at[idx_v], rows_v, sem).wait()
            pltpu.sync_copy(rows_v, out_hbm.at[pl.ds(off, SC_WINDOW)])

    return gather(table, idx)


def _combine_kernel(gate_ref, x_ref, rows_ref, y_ref):
    gate_t = gate_ref[...].T
    half = D_MODEL // 2
    lo_sum = x_ref[:, :half]
    hi_sum = x_ref[:, half:]
    for k in range(TOP_K):
        packed = lax.bitcast_convert_type(rows_ref[k], jnp.uint32)
        g = gate_t[:, k:k + 1]
        lo_sum = lo_sum + g * lax.bitcast_convert_type(packed << 16, F32)
        hi_sum = hi_sum + g * lax.bitcast_convert_type(packed & jnp.uint32(0xFFFF0000), F32)
    y_ref[:, :half] = lo_sum
    y_ref[:, half:] = hi_sum


def _combine(gate, x2d, rows4):
    t = x2d.shape[0]
    rows = COMBINE_ROWS
    return pl.pallas_call(
        _combine_kernel,
        grid=(t // rows,),
        in_specs=[
            pl.BlockSpec((2 * TOP_K, rows), lambda i: (0, i)),
            pl.BlockSpec((rows, D_MODEL), lambda i: (i, 0)),
            pl.BlockSpec((TOP_K, rows, D_MODEL // 2), lambda i: (0, i, 0)),
        ],
        out_specs=pl.BlockSpec((rows, D_MODEL), lambda i: (i, 0)),
        out_shape=jax.ShapeDtypeStruct((t, D_MODEL), F32),
        compiler_params=_cparams("parallel"),
        name="combine",
    )(gate, x2d, rows4)


def _moe_half(x2d, m):
    t = x2d.shape[0]
    rows = EXPERT_ROWS
    cap = t * TOP_K + N_EXPERTS * rows
    n_blk = cap // rows
    xn, idx, rank, gate, cnt = _router(x2d, m["g2"], m["wr_hi"], m["wr_lo"], m["br"], m["tri"])

    counts = cnt[:, 0].astype(jnp.int32)
    padded = (counts + rows - 1) // rows * rows
    pends = jnp.cumsum(padded)
    pstart = pends - padded
    n_used = pends[-1:] // rows
    blk_src = jnp.minimum(jnp.arange(n_blk, dtype=jnp.int32), n_used - 1)
    starts = (blk_src * rows)[:, None]
    owner = (pstart[None, :] <= starts) & (starts < pends[None, :])
    blk_e = jnp.sum(jnp.where(owner, jnp.arange(N_EXPERTS)[None, :], 0), axis=1).astype(jnp.int32)
    filled_to = jnp.sum(jnp.where(owner, (pstart + counts)[None, :], 0), axis=1)
    blk_valid = jnp.clip(filled_to - blk_src * rows, 0, rows)
    in_use = jnp.arange(n_blk) < n_used
    blk_valid = jnp.where(in_use, blk_valid, 0).astype(jnp.int32)
    expert = jnp.arange(N_EXPERTS)
    has_rows = counts > 0
    later = jnp.where(has_rows[None, :] & (expert[None, :] > expert[:, None]), expert[None, :], N_EXPERTS)
    next_expert = jnp.min(later, axis=1)
    next_expert = jnp.where(next_expert == N_EXPERTS, -1, next_expert)
    expert_slot = (jnp.cumsum(has_rows) - 1) % 2
    blk_first = (in_use & (jnp.sum(jnp.where(owner, pstart[None, :], 0), axis=1) == blk_src * rows)).astype(jnp.int32)
    blk_next = jnp.sum(jnp.where(owner, next_expert[None, :], 0), axis=1).astype(jnp.int32)
    blk_slot = jnp.sum(jnp.where(owner, expert_slot[None, :], 0), axis=1).astype(jnp.int32)

    dest = _dest(pstart.astype(jnp.int32), idx, rank)
    xs = _sc_scatter(xn, dest, cap)
    out_sorted = _experts(blk_e, blk_src, blk_valid, blk_first, blk_next, blk_slot, xs, m["w1"], m["b1"], m["w2"],
                          m["b2"])
    rows4 = _sc_gather(out_sorted, dest.reshape(TOP_K * t))
    return _combine(gate, x2d, rows4.reshape(TOP_K, t, D_MODEL // 2))


def _prep_moe(norm2_g, w_router, b_router, w_moe_in, b_moe_in, w_moe_out, b_moe_out):
    r = jnp.arange(ROUTER_ROWS)
    wr_hi = w_router.T.astype(BF16)
    return dict(
        g2=norm2_g.reshape(1, D_MODEL),
        wr_hi=wr_hi,
        wr_lo=(w_router.T - wr_hi.astype(F32)).astype(BF16),
        br=b_router.reshape(N_EXPERTS, 1),
        tri=(r[:, None] < r[None, :]).astype(BF16),
        w1=w_moe_in,
        b1=b_moe_in.reshape(N_EXPERTS, 1, 2 * D_FF),
        w2=w_moe_out,
        b2=b_moe_out.reshape(N_EXPERTS, 1, D_MODEL),
    )


def kernel(x_prompt, x_sample, norm1_g, w_in, q_norm_g, k_norm_g, attn_sink, sgu_ln_g, sgu_ln_b, w_spatial,
           b_spatial, attn_out_g, sgu_out_g, w_out, norm2_g, w_router, b_router, w_moe_in, b_moe_in, w_moe_out,
           b_moe_out):
    p = _prep_params(norm1_g[0], w_in[0], q_norm_g[0], k_norm_g[0], attn_sink[0], sgu_ln_g[0], sgu_ln_b[0],
                     w_spatial[0], b_spatial[0], attn_out_g[0], sgu_out_g[0], w_out[0])
    m = _prep_moe(norm2_g[0], w_router[0], b_router[0], w_moe_in[0], b_moe_in[0], w_moe_out[0], b_moe_out[0])
    outs = []
    for x in (x_prompt, x_sample):
        x2 = _mix_half(x, p)
        outs.append(_moe_half(x2, m).reshape(x.shape))
    return tuple(outs)
```

```python
import functools
import math

import jax
import jax.numpy as jnp
from jax import lax
from jax.experimental import pallas as pl
from jax.experimental.pallas import tpu as pltpu
from jax.experimental.pallas import tpu_sc as plsc

D_MODEL = 1024
HEAD_DIM = 64
N_Q_HEADS = 8
N_KV_HEADS = 2
Q_PER_KV = N_Q_HEADS // N_KV_HEADS
ATTN_WIDTH = N_Q_HEADS * HEAD_DIM
KV_WIDTH = N_KV_HEADS * HEAD_DIM
QK_WIDTH = ATTN_WIDTH + KV_WIDTH
KV_DUP_WIDTH = 2 * KV_WIDTH
N_SGU_GROUPS = 8
SGU_GROUP_DIM = 64
SGU_WIDTH = N_SGU_GROUPS * SGU_GROUP_DIM
IN_PROJ_WIDTH = ATTN_WIDTH + 2 * KV_WIDTH + 2 * SGU_WIDTH
BLOCK = 128
ROPE_THETA = 500000.0
ROPE_DIM = HEAD_DIM // 4
N_EXPERTS = 32
TOP_K = 4
D_FF = D_MODEL
SWIGLU_LIMIT = 7.0
SWIGLU_ALPHA = 1.702
EPS = 1e-6
LOG2_E = 1.4426950408889634

LANES = 128
IN_PROJ_ROWS = 1024
IN_PROJ_CHUNK = 256
MIXER_ROWS = 1024
ROUTER_ROWS = 1024
DEST_ROWS = 2048
COMBINE_ROWS = 1024
EXPERT_ROWS = 1024
EXPERT_ROW_STEPS = (256, 512, 1024)
VMEM_LIMIT_BYTES = 56 * 1024 * 1024

F32 = jnp.float32
BF16 = jnp.bfloat16


def _cparams(*semantics):
    return pltpu.CompilerParams(dimension_semantics=semantics, vmem_limit_bytes=VMEM_LIMIT_BYTES)


def _dup_heads(tile):
    low = lax.broadcasted_iota(jnp.int32, tile.shape, 1) < HEAD_DIM
    swapped = pltpu.roll(tile, HEAD_DIM, axis=1)
    return jnp.where(low, tile, swapped), jnp.where(low, swapped, tile)


def _gelu_tanh(x):
    k = 2.0 * math.sqrt(2.0 / math.pi) * LOG2_E
    return x / (1.0 + jnp.exp2(x * (-k - (k * 0.044715) * (x * x))))


def _in_proj_kernel(x_ref, g1_ref, w_ref, qkg_ref, cos_ref, sina_ref, sinb_ref, seg_ref, lng_ref, lnb_ref,
                    q_ref, k_ref, v_ref, u_ref, vn_ref):
    for r0 in range(0, IN_PROJ_ROWS, IN_PROJ_CHUNK):
        rs = slice(r0, r0 + IN_PROJ_CHUNK)
        x = x_ref[rs, :]
        h = x * lax.rsqrt(jnp.mean(x * x, axis=-1, keepdims=True) + EPS) * g1_ref[...]
        z = jnp.dot(h.astype(BF16), w_ref[...], preferred_element_type=F32)

        qk = z[:, :QK_WIDTH]
        ss = jnp.dot((qk * qk).astype(BF16), seg_ref[...], preferred_element_type=F32)
        qkn = qk * lax.rsqrt(ss * (1.0 / HEAD_DIM) + EPS) * qkg_ref[...]
        cos, sina, sinb = cos_ref[rs, :], sina_ref[rs, :], sinb_ref[rs, :]
        for c in range(QK_WIDTH // LANES):
            xc = qkn[:, c * LANES:(c + 1) * LANES]
            up = pltpu.roll(xc, LANES - ROPE_DIM // 2, axis=1)
            dn = pltpu.roll(xc, ROPE_DIM // 2, axis=1)
            rc = xc * cos + up * sina + dn * sinb
            if c < ATTN_WIDTH // LANES:
                q_ref[rs, c * LANES:(c + 1) * LANES] = (rc * (HEAD_DIM ** -0.5 * LOG2_E)).astype(BF16)
            else:
                k0, k1 = _dup_heads(rc)
                k_ref[rs, :LANES] = k0.astype(BF16)
                k_ref[rs, LANES:] = k1.astype(BF16)

        v0, v1 = _dup_heads(z[:, QK_WIDTH:QK_WIDTH + KV_WIDTH])
        v_ref[rs, :LANES] = v0.astype(BF16)
        v_ref[rs, LANES:] = v1.astype(BF16)
        su = z[:, QK_WIDTH + KV_WIDTH:QK_WIDTH + KV_WIDTH + SGU_WIDTH]
        sv = z[:, QK_WIDTH + KV_WIDTH + SGU_WIDTH:]
        u_ref[rs, :] = _gelu_tanh(su).astype(BF16)
        gv = _gelu_tanh(sv)
        mu = jnp.mean(gv, axis=-1, keepdims=True)
        gc = gv - mu
        ln = gc * lax.rsqrt(jnp.mean(gc * gc, axis=-1, keepdims=True) + EPS) * lng_ref[...] + lnb_ref[...]
        vn_ref[rs, :] = ln.astype(BF16)


def _rope_tables(seq):
    half = ROPE_DIM // 2
    inv_freq = ROPE_THETA ** (-(jnp.arange(half, dtype=F32) * 2.0) / ROPE_DIM)
    ang = jnp.arange(seq).astype(F32)[:, None] * inv_freq[None, :]
    cos, sin = jnp.cos(ang), jnp.sin(ang)
    j = jnp.arange(LANES) % HEAD_DIM
    f = j % half
    cos_t = jnp.where(j[None, :] < ROPE_DIM, cos[:, f], 1.0)
    sina_t = jnp.where(j[None, :] < half, -sin[:, f], 0.0)
    sinb_t = jnp.where((j[None, :] >= half) & (j[None, :] < ROPE_DIM), sin[:, f], 0.0)
    return cos_t.astype(F32), sina_t.astype(F32), sinb_t.astype(F32)


def _in_proj(x2d, seq, g1, w_in, qkg, tables, seg, lng, lnb):
    t = x2d.shape[0]
    rows = IN_PROJ_ROWS
    n_seq = seq // rows
    const = lambda i: (0, 0)
    tab = pl.BlockSpec((rows, LANES), lambda i: (i % n_seq, 0))
    return pl.pallas_call(
        _in_proj_kernel,
        grid=(t // rows,),
        in_specs=[
            pl.BlockSpec((rows, D_MODEL), lambda i: (i, 0)),
            pl.BlockSpec((1, D_MODEL), const),
            pl.BlockSpec((D_MODEL, IN_PROJ_WIDTH), const),
            pl.BlockSpec((1, QK_WIDTH), const),
            tab, tab, tab,
            pl.BlockSpec((QK_WIDTH, QK_WIDTH), const),
            pl.BlockSpec((1, SGU_WIDTH), const),
            pl.BlockSpec((1, SGU_WIDTH), const),
        ],
        out_specs=[
            pl.BlockSpec((rows, ATTN_WIDTH), lambda i: (i, 0)),
            pl.BlockSpec((rows, KV_DUP_WIDTH), lambda i: (i, 0)),
            pl.BlockSpec((rows, KV_DUP_WIDTH), lambda i: (i, 0)),
            pl.BlockSpec((rows, SGU_WIDTH), lambda i: (i, 0)),
            pl.BlockSpec((rows, SGU_WIDTH), lambda i: (i, 0)),
        ],
        out_shape=[
            jax.ShapeDtypeStruct((t, ATTN_WIDTH), BF16),
            jax.ShapeDtypeStruct((t, KV_DUP_WIDTH), BF16),
            jax.ShapeDtypeStruct((t, KV_DUP_WIDTH), BF16),
            jax.ShapeDtypeStruct((t, SGU_WIDTH), BF16),
            jax.ShapeDtypeStruct((t, SGU_WIDTH), BF16),
        ],
        compiler_params=_cparams("parallel"),
        name="in_proj",
    )(x2d, g1, w_in, qkg, *tables, seg, lng, lnb)


def _mixer_kernel(sink_ref, q_ref, kp_ref, kc_ref, kn_ref, vp_ref, vc_ref, vx_ref, u_ref, g_ref, x_ref,
                  ws_ref, bs_ref, ag_ref, sg_ref, wo_ref, o_ref, mix_ref):
    i = pl.program_id(1)
    n_i = pl.num_programs(1)
    n_sub = MIXER_ROWS // BLOCK
    kwin = jnp.concatenate([kp_ref[...], kc_ref[...], kn_ref[...]], axis=0)
    vwin = jnp.concatenate([vp_ref[...], vc_ref[...], vx_ref[...]], axis=0)

    srows = Q_PER_KV * BLOCK
    r = lax.broadcasted_iota(jnp.int32, (srows, 3 * BLOCK), 0) & (BLOCK - 1)
    c = lax.broadcasted_iota(jnp.int32, (srows, 3 * BLOCK), 1)
    band = (c >= r) & (c <= r + 2 * BLOCK)
    hrow = lax.broadcasted_iota(jnp.int32, (srows, 1), 0) // BLOCK
    low = lax.broadcasted_iota(jnp.int32, (BLOCK, LANES), 1) < HEAD_DIM
    keep = (low.astype(BF16), (~low).astype(BF16))
    ones = jnp.ones((3 * BLOCK, LANES), BF16)

    for j in range(n_sub):
        valid = band
        if j == 0:
            valid = valid & ((c >= BLOCK) | (i > 0))
        if j == n_sub - 1:
            valid = valid & ((c < 2 * BLOCK) | (i < n_i - 1))
        kj = kwin[j * BLOCK:(j + 3) * BLOCK, :]
        vj = vwin[j * BLOCK:(j + 3) * BLOCK, :]
        a_tiles = []
        for hk in range(N_KV_HEADS):
            qs = jnp.concatenate(
                [q_ref[j * BLOCK:(j + 1) * BLOCK, (h // 2) * LANES:(h // 2 + 1) * LANES] * keep[h % 2]
                 for h in range(hk * Q_PER_KV, (hk + 1) * Q_PER_KV)], axis=0)
            kh = kj[:, hk * LANES:(hk + 1) * LANES]
            vh = vj[:, hk * LANES:(hk + 1) * LANES]
            s = lax.dot_general(qs, kh, (((1,), (1,)), ((), ())), preferred_element_type=F32)
            s = jnp.where(valid, s, -jnp.inf)
            sink = jnp.zeros((srows, 1), F32)
            for g in range(Q_PER_KV):
                sink = jnp.where(hrow == g, sink_ref[hk * Q_PER_KV + g] * LOG2_E, sink)
            m = jnp.maximum(jnp.max(s, axis=-1, keepdims=True), sink)
            p = jnp.exp2(s - m).astype(BF16)
            ov = jnp.dot(p, jnp.concatenate([vh, ones], axis=-1), preferred_element_type=F32)
            o = ov[:, :LANES] / (ov[:, LANES:] + jnp.exp2(sink - m))
            for g in range(0, Q_PER_KV, 2):
                a_tiles.append(jnp.where(low, o[g * BLOCK:(g + 1) * BLOCK, :], o[(g + 1) * BLOCK:(g + 2) * BLOCK, :]))
        a = jnp.concatenate(a_tiles, axis=-1)
        a = a * lax.rsqrt(jnp.mean(a * a, axis=-1, keepdims=True) + EPS) * ag_ref[...]

        mixed_tiles = []
        for t in range(SGU_WIDTH // LANES):
            vt = g_ref[j * BLOCK:(j + 1) * BLOCK, t * LANES:(t + 1) * LANES]
            mixed_tiles.append(jnp.where(low, jnp.dot(ws_ref[2 * t], vt, preferred_element_type=F32),
                                         jnp.dot(ws_ref[2 * t + 1], vt, preferred_element_type=F32)))
        mixed = jnp.concatenate(mixed_tiles, axis=-1) + bs_ref[...]
        gated = u_ref[j * BLOCK:(j + 1) * BLOCK, :].astype(F32) * mixed
        gated = gated * lax.rsqrt(jnp.mean(gated * gated, axis=-1, keepdims=True) + EPS) * sg_ref[...]
        mix_ref[j * BLOCK:(j + 1) * BLOCK, :] = jnp.concatenate([a, gated], axis=-1).astype(BF16)

    o_ref[...] = x_ref[...] + jnp.dot(mix_ref[...], wo_ref[...], preferred_element_type=F32)


def _mixer(batch, seq, sink, q, k, v, u, vn, x2d, ws, bs, ag, sg, wo):
    rows = MIXER_ROWS
    n_i = seq // rows
    sub = rows // BLOCK
    n_blk = batch * seq // BLOCK
    const2 = lambda b, i, s: (0, 0)
    cur = lambda b, i, s: (b * n_i + i, 0)
    prv = lambda b, i, s: (jnp.maximum((b * n_i + i) * sub - 1, 0), 0)
    nxt = lambda b, i, s: (jnp.minimum((b * n_i + i + 1) * sub, n_blk - 1), 0)
    grid_spec = pltpu.PrefetchScalarGridSpec(
        num_scalar_prefetch=1,
        grid=(batch, n_i),
        in_specs=[
            pl.BlockSpec((rows, ATTN_WIDTH), cur),
            pl.BlockSpec((BLOCK, KV_DUP_WIDTH), prv),
            pl.BlockSpec((rows, KV_DUP_WIDTH), cur),
            pl.BlockSpec((BLOCK, KV_DUP_WIDTH), nxt),
            pl.BlockSpec((BLOCK, KV_DUP_WIDTH), prv),
            pl.BlockSpec((rows, KV_DUP_WIDTH), cur),
            pl.BlockSpec((BLOCK, KV_DUP_WIDTH), nxt),
            pl.BlockSpec((rows, SGU_WIDTH), cur),
            pl.BlockSpec((rows, SGU_WIDTH), cur),
            pl.BlockSpec((rows, D_MODEL), cur),
            pl.BlockSpec((N_SGU_GROUPS, BLOCK, BLOCK), lambda b, i, s: (0, 0, 0)),
            pl.BlockSpec((BLOCK, SGU_WIDTH), const2),
            pl.BlockSpec((1, ATTN_WIDTH), const2),
            pl.BlockSpec((1, SGU_WIDTH), const2),
            pl.BlockSpec((D_MODEL, D_MODEL), const2),
        ],
        out_specs=pl.BlockSpec((rows, D_MODEL), cur),
        scratch_shapes=[pltpu.VMEM((rows, D_MODEL), BF16)],
    )
    return pl.pallas_call(
        _mixer_kernel,
        grid_spec=grid_spec,
        out_shape=jax.ShapeDtypeStruct((batch * seq, D_MODEL), F32),
        compiler_params=_cparams("parallel", "parallel"),
        name="mixer",
    )(sink, q, k, k, k, v, v, v, u, vn, x2d, ws, bs, ag, sg, wo)


def _mix_half(x, p):
    batch, seq, _ = x.shape
    x2d = x.reshape(batch * seq, D_MODEL)
    q, k, v, u, vn = _in_proj(x2d, seq, p["g1"], p["w_in"], p["qkg"], _rope_tables(seq), p["seg"], p["lng"],
                              p["lnb"])
    return _mixer(batch, seq, p["sink"], q, k, v, u, vn, x2d, p["ws"], p["bs"], p["ag"], p["sg"], p["wo"])


def _prep_params(norm1_g, w_in, q_norm_g, k_norm_g, attn_sink, sgu_ln_g, sgu_ln_b, w_spatial, b_spatial,
                 attn_out_g, sgu_out_g, w_out):
    head = jnp.arange(QK_WIDTH) // HEAD_DIM
    return dict(
        g1=norm1_g.reshape(1, D_MODEL),
        w_in=w_in.astype(BF16),
        qkg=jnp.concatenate([jnp.tile(q_norm_g, N_Q_HEADS), jnp.tile(k_norm_g, N_KV_HEADS)]).reshape(1, QK_WIDTH),
        seg=(head[:, None] == head[None, :]).astype(BF16),
        lng=sgu_ln_g.reshape(1, SGU_WIDTH),
        lnb=sgu_ln_b.reshape(1, SGU_WIDTH),
        sink=attn_sink.astype(F32),
        ws=w_spatial.astype(BF16),
        bs=jnp.repeat(b_spatial.T, SGU_GROUP_DIM, axis=1),
        ag=attn_out_g.reshape(1, ATTN_WIDTH),
        sg=sgu_out_g.reshape(1, SGU_WIDTH),
        wo=w_out.astype(BF16),
    )


def _pack_bf16_pair(lo, hi):
    lo_b = lax.bitcast_convert_type(lo.astype(BF16).astype(F32), jnp.uint32) >> 16
    hi_b = lax.bitcast_convert_type(hi.astype(BF16).astype(F32), jnp.uint32) & jnp.uint32(0xFFFF0000)
    return hi_b | lo_b


def _unpack_bf16_pair(packed):
    lo = lax.bitcast_convert_type(packed << 16, F32).astype(BF16)
    hi = lax.bitcast_convert_type(packed & jnp.uint32(0xFFFF0000), F32).astype(BF16)
    return lo, hi


def _router_kernel(x_ref, g2_ref, wh_ref, wl_ref, br_ref, tri_ref, xn_ref, idx_ref, rank_ref, gate_ref, cnt_ref,
                   run_ref):
    @pl.when(pl.program_id(0) == 0)
    def _():
        run_ref[...] = jnp.zeros_like(run_ref)

    x = x_ref[...]
    xn = x * lax.rsqrt(jnp.mean(x * x, axis=-1, keepdims=True) + EPS) * g2_ref[...]
    xn_ref[...] = lax.bitcast_convert_type(_pack_bf16_pair(xn[:, :D_MODEL // 2], xn[:, D_MODEL // 2:]), jnp.int32)

    xh = xn.astype(BF16)
    xl = (xn - xh.astype(F32)).astype(BF16)
    nt = (((1,), (1,)), ((), ()))
    logits = (lax.dot_general(wh_ref[...], xh, nt, preferred_element_type=F32)
              + lax.dot_general(wh_ref[...], xl, nt, preferred_element_type=F32)
              + lax.dot_general(wl_ref[...], xh, nt, preferred_element_type=F32)) + br_ref[...]
    rows = logits.shape[1]
    erow = lax.broadcasted_iota(jnp.int32, (N_EXPERTS, rows), 0)
    work = logits
    vals, sels = [], []
    for k in range(TOP_K):
        m = jnp.max(work, axis=0, keepdims=True)
        ik = jnp.min(jnp.where(work == m, erow, N_EXPERTS), axis=0, keepdims=True)
        sel = erow == ik
        idx_ref[k:k + 1, :] = ik
        vals.append(m)
        sels.append(sel)
        work = jnp.where(sel, -jnp.inf, work)

    exps = [jnp.exp(v - vals[0]) for v in vals]
    den = exps[0] + exps[1] + exps[2] + exps[3]
    gate_ref[...] = jnp.zeros_like(gate_ref)
    for k in range(TOP_K):
        gate_ref[k:k + 1, :] = exps[k] / den

    onehot = jnp.zeros((N_EXPERTS, rows), F32)
    for sel in sels:
        onehot = onehot + sel.astype(F32)
    before = jnp.dot(onehot.astype(BF16), tri_ref[...], preferred_element_type=F32) + run_ref[:, :1]
    for k in range(TOP_K):
        rank_ref[k:k + 1, :] = jnp.sum(jnp.where(sels[k], before, 0.0), axis=0, keepdims=True).astype(jnp.int32)
    run_ref[...] = run_ref[...] + jnp.sum(onehot, axis=1, keepdims=True)
    cnt_ref[...] = run_ref[...]


def _router(x2d, g2, wr_hi, wr_lo, br, tri):
    t = x2d.shape[0]
    rows = ROUTER_ROWS
    const = lambda i: (0, 0)
    return pl.pallas_call(
        _router_kernel,
        grid=(t // rows,),
        in_specs=[
            pl.BlockSpec((rows, D_MODEL), lambda i: (i, 0)),
            pl.BlockSpec((1, D_MODEL), const),
            pl.BlockSpec((N_EXPERTS, D_MODEL), const),
            pl.BlockSpec((N_EXPERTS, D_MODEL), const),
            pl.BlockSpec((N_EXPERTS, 1), const),
            pl.BlockSpec((rows, rows), const),
        ],
        out_specs=[
            pl.BlockSpec((rows, D_MODEL // 2), lambda i: (i, 0)),
            pl.BlockSpec((TOP_K, rows), lambda i: (0, i)),
            pl.BlockSpec((TOP_K, rows), lambda i: (0, i)),
            pl.BlockSpec((2 * TOP_K, rows), lambda i: (0, i)),
            pl.BlockSpec((N_EXPERTS, LANES), const),
        ],
        out_shape=[
            jax.ShapeDtypeStruct((t, D_MODEL // 2), jnp.int32),
            jax.ShapeDtypeStruct((TOP_K, t), jnp.int32),
            jax.ShapeDtypeStruct((TOP_K, t), jnp.int32),
            jax.ShapeDtypeStruct((2 * TOP_K, t), F32),
            jax.ShapeDtypeStruct((N_EXPERTS, LANES), F32),
        ],
        scratch_shapes=[pltpu.VMEM((N_EXPERTS, LANES), F32)],
        compiler_params=_cparams("arbitrary"),
        name="router",
    )(x2d, g2, wr_hi, wr_lo, br, tri)


def _dest_kernel(pstart_ref, idx_ref, rank_ref, dest_ref):
    idx = idx_ref[...]
    dest = rank_ref[...]
    for e in range(N_EXPERTS):
        dest = dest + jnp.where(idx == e, pstart_ref[e], 0)
    dest_ref[...] = dest


def _dest(pstart, idx, rank):
    t = idx.shape[1]
    rows = min(DEST_ROWS, t)
    blk =pl.BlockSpec((TOP_K, rows), lambda i, s: (0, i))
    grid_spec = pltpu.PrefetchScalarGridSpec(num_scalar_prefetch=1, grid=(t // rows,), in_specs=[blk, blk],
                                             out_specs=blk)
    return pl.pallas_call(
        _dest_kernel,
        grid_spec=grid_spec,
        out_shape=jax.ShapeDtypeStruct((TOP_K, t), jnp.int32),
        compiler_params=_cparams("parallel"),
        name="dest",
    )(pstart, idx, rank)


SC_CORES = 2
SC_SUBCORES = 16
SC_WORKERS = SC_CORES * SC_SUBCORES
SC_WINDOW = 128


def _sc_mesh():
    return plsc.VectorSubcoreMesh(core_axis_name="c", subcore_axis_name="s", num_cores=SC_CORES,
                                  num_subcores=SC_SUBCORES)


def _sc_worker():
    return lax.axis_index("s") * SC_CORES + lax.axis_index("c")


def _sc_scatter(rows, idx, cap):
    t, width = rows.shape
    n_idx = idx.shape[0]
    per_worker = t // SC_WORKERS
    assert per_worker * SC_WORKERS == t and per_worker % SC_WINDOW == 0
    idx_flat = idx.reshape(n_idx * t)

    @functools.partial(
        pl.kernel,
        mesh=_sc_mesh(),
        out_type=jax.ShapeDtypeStruct((cap, width), rows.dtype),
        scratch_types=[
            pltpu.VMEM((SC_WINDOW,), jnp.int32),
            pltpu.VMEM((SC_WINDOW, width), rows.dtype),
            pltpu.SemaphoreType.DMA,
        ],
        name="sc_scatter",
    )
    def scatter(rows_hbm, idx_hbm, out_hbm, idx_v, rows_v, sem):
        base = _sc_worker() * per_worker

        @pl.loop(0, per_worker // SC_WINDOW)
        def _(step):
            off = pl.multiple_of(base + step * SC_WINDOW, SC_WINDOW)
            pltpu.sync_copy(rows_hbm.at[pl.ds(off, SC_WINDOW)], rows_v)
            for k in range(n_idx):
                pltpu.sync_copy(idx_hbm.at[pl.ds(pl.multiple_of(k * t + off, SC_WINDOW), SC_WINDOW)], idx_v)
                pltpu.async_copy(rows_v, out_hbm.at[idx_v], sem).wait()

    return scatter(rows, idx_flat)


def _experts_kernel(blk_e_ref, blk_src_ref, blk_valid_ref, blk_first_ref, blk_next_ref, blk_slot_ref,
                    x_ref, w1_hbm, b1_ref, w2_hbm, b2_ref, o_ref, w1_buf, w2_buf, w1_ref, w2_ref, sems):
    del blk_src_ref
    b = pl.program_id(0)
    valid = blk_valid_ref[b]
    slot = blk_slot_ref[b]
    half = D_MODEL // 2

    def weight_copies(expert, s):
        return (pltpu.make_async_copy(w1_hbm.at[expert], w1_buf.at[s], sems.at[0, s]),
                pltpu.make_async_copy(w2_hbm.at[expert], w2_buf.at[s], sems.at[1, s]))

    @pl.when(b == 0)
    def _():
        for copy in weight_copies(blk_e_ref[0], slot):
            copy.start()

    @pl.when(blk_first_ref[b] == 1)
    def _():
        for copy in weight_copies(blk_e_ref[b], slot):
            copy.wait()

        @pl.when(blk_next_ref[b] >= 0)
        def _():
            for copy in weight_copies(blk_next_ref[b], 1 - slot):
                copy.start()

        w1_ref[...] = w1_buf[slot].astype(BF16)
        w2_ref[...] = w2_buf[slot].astype(BF16)

    def run(rows):
        row = lax.broadcasted_iota(jnp.int32, (rows, half), 0)
        x = jnp.where(row < valid, x_ref[:rows, :], 0)
        lo, hi = _unpack_bf16_pair(lax.bitcast_convert_type(x, jnp.uint32))
        h = (jnp.dot(lo, w1_ref[:half, :], preferred_element_type=F32)
             + jnp.dot(hi, w1_ref[half:, :], preferred_element_type=F32) + b1_ref[...])
        gate = jnp.minimum(h[:, :D_FF], SWIGLU_LIMIT)
        up = jnp.clip(h[:, D_FF:], -SWIGLU_LIMIT, SWIGLU_LIMIT)
        act = (up + 1.0) * (gate / (1.0 + jnp.exp2(gate * (-SWIGLU_ALPHA * LOG2_E))))
        o = jnp.dot(act.astype(BF16), w2_ref[...], preferred_element_type=F32) + b2_ref[...]
        o_ref[:rows, :] = lax.bitcast_convert_type(_pack_bf16_pair(o[:, :half], o[:, half:]), jnp.int32)
        if rows < EXPERT_ROWS:
            o_ref[rows:, :] = jnp.zeros((EXPERT_ROWS - rows, half), jnp.int32)

    lower = 0
    for rows in EXPERT_ROW_STEPS:
        pl.when((valid > lower) & (valid <= rows))(functools.partial(run, rows))
        lower = rows

    @pl.when(valid == 0)
    def _():
        o_ref[...] = jnp.zeros_like(o_ref)


def _experts(blk_e, blk_src, blk_valid, blk_first, blk_next, blk_slot, xs, w1, b1, w2, b2):
    cap = xs.shape[0]
    rows = EXPERT_ROWS
    grid_spec = pltpu.PrefetchScalarGridSpec(
        num_scalar_prefetch=6,
        grid=(cap // rows,),
        in_specs=[
            pl.BlockSpec((rows, D_MODEL // 2), lambda b, be, bs, *_: (bs[b], 0)),
            pl.BlockSpec(memory_space=pl.ANY),
            pl.BlockSpec((None, 1, 2 * D_FF), lambda b, be, *_: (be[b], 0, 0)),
            pl.BlockSpec(memory_space=pl.ANY),
            pl.BlockSpec((None, 1, D_MODEL), lambda b, be, *_: (be[b], 0, 0)),
        ],
        out_specs=pl.BlockSpec((rows, D_MODEL // 2), lambda b, *_: (b, 0)),
        scratch_shapes=[
            pltpu.VMEM((2, D_MODEL, 2 * D_FF), F32),
            pltpu.VMEM((2, D_FF, D_MODEL), F32),
            pltpu.VMEM((D_MODEL, 2 * D_FF), BF16),
            pltpu.VMEM((D_FF, D_MODEL), BF16),
            pltpu.SemaphoreType.DMA((2, 2)),
        ],
    )
    return pl.pallas_call(
        _experts_kernel,
        grid_spec=grid_spec,
        out_shape=jax.ShapeDtypeStruct((cap, D_MODEL // 2), jnp.int32),
        compiler_params=_cparams("arbitrary"),
        name="experts",
    )(blk_e, blk_src, blk_valid, blk_first, blk_next, blk_slot, xs, w1, b1, w2, b2)


def _sc_gather(table, idx):
    n = idx.shape[0]
    width = table.shape[1]
    per_worker = n // SC_WORKERS
    assert per_worker * SC_WORKERS == n and per_worker % SC_WINDOW == 0

    @functools.partial(
        pl.kernel,
        mesh=_sc_mesh(),
        out_type=jax.ShapeDtypeStruct((n, width), table.dtype),
        scratch_types=[
            pltpu.VMEM((SC_WINDOW,), jnp.int32),
            pltpu.VMEM((SC_WINDOW, width), table.dtype),
            pltpu.SemaphoreType.DMA,
        ],
        name="sc_gather",
    )
    def gather(table_hbm, idx_hbm, out_hbm, idx_v, rows_v, sem):
        base = _sc_worker() * per_worker

        @pl.loop(0, per_worker // SC_WINDOW)
        def _(step):
            off = pl.multiple_of(base + step * SC_WINDOW, SC_WINDOW)
            pltpu.sync_copy(idx_hbm.at[pl.ds(off, SC_WINDOW)], idx_v)
            pltpu.async_copy(table_hbm.at[idx_v], rows_v, sem).wait()
            pltpu.sync_copy(rows_v, out_hbm.at[pl.ds(off, SC_WINDOW)])

    return gather(table, idx)


def _combine_kernel(gate_ref, x_ref, rows_ref, y_ref):
    gate_t = gate_ref[...].T
    half = D_MODEL // 2
    lo_sum = x_ref[:, :half]
    hi_sum = x_ref[:, half:]
    for k in range(TOP_K):
        packed = lax.bitcast_convert_type(rows_ref[k], jnp.uint32)
        g = gate_t[:, k:k + 1]
        lo_sum = lo_sum + g * lax.bitcast_convert_type(packed << 16, F32)
        hi_sum = hi_sum + g * lax.bitcast_convert_type(packed & jnp.uint32(0xFFFF0000), F32)
    y_ref[:, :half] = lo_sum
    y_ref[:, half:] = hi_sum


def _combine(gate, x2d, rows4):
    t = x2d.shape[0]
    rows = COMBINE_ROWS
    return pl.pallas_call(
        _combine_kernel,
        grid=(t // rows,),
        in_specs=[
            pl.BlockSpec((2 * TOP_K, rows), lambda i: (0, i)),
            pl.BlockSpec((rows, D_MODEL), lambda i: (i, 0)),
            pl.BlockSpec((TOP_K, rows, D_MODEL // 2), lambda i: (0, i, 0)),
        ],
        out_specs=pl.BlockSpec((rows, D_MODEL), lambda i: (i, 0)),
        out_shape=jax.ShapeDtypeStruct((t, D_MODEL), F32),
        compiler_params=_cparams("parallel"),
        name="combine",
    )(gate, x2d, rows4)


def _moe_half(x2d, m):
    t = x2d.shape[0]
    rows = EXPERT_ROWS
    cap = t * TOP_K + N_EXPERTS * rows
    n_blk = cap // rows
    xn, idx, rank, gate, cnt = _router(x2d, m["g2"], m["wr_hi"], m["wr_lo"], m["br"], m["tri"])

    counts = cnt[:, 0].astype(jnp.int32)
    padded = (counts + rows - 1) // rows * rows
    pends = jnp.cumsum(padded)
    pstart = pends - padded
    n_used = pends[-1:] // rows
    blk_src = jnp.minimum(jnp.arange(n_blk, dtype=jnp.int32), n_used - 1)
    starts = (blk_src * rows)[:, None]
    owner = (pstart[None, :] <= starts) & (starts < pends[None, :])
    blk_e = jnp.sum(jnp.where(owner, jnp.arange(N_EXPERTS)[None, :], 0), axis=1).astype(jnp.int32)
    filled_to = jnp.sum(jnp.where(owner, (pstart + counts)[None, :], 0), axis=1)
    blk_valid = jnp.clip(filled_to - blk_src * rows, 0, rows)
    in_use = jnp.arange(n_blk) < n_used
    blk_valid = jnp.where(in_use, blk_valid, 0).astype(jnp.int32)
    expert = jnp.arange(N_EXPERTS)
    has_rows = counts > 0
    later = jnp.where(has_rows[None, :] & (expert[None, :] > expert[:, None]), expert[None, :], N_EXPERTS)
    next_expert = jnp.min(later, axis=1)
    next_expert = jnp.where(next_expert == N_EXPERTS, -1, next_expert)
    expert_slot = (jnp.cumsum(has_rows) - 1) % 2
    blk_first = (in_use & (jnp.sum(jnp.where(owner, pstart[None, :], 0), axis=1) == blk_src * rows)).astype(jnp.int32)
    blk_next = jnp.sum(jnp.where(owner, next_expert[None, :], 0), axis=1).astype(jnp.int32)
    blk_slot = jnp.sum(jnp.where(owner, expert_slot[None, :], 0), axis=1).astype(jnp.int32)

    dest = _dest(pstart.astype(jnp.int32), idx, rank)
    xs = _sc_scatter(xn, dest, cap)
    out_sorted = _experts(blk_e, blk_src, blk_valid, blk_first, blk_next, blk_slot, xs, m["w1"], m["b1"], m["w2"],
                          m["b2"])
    rows4 = _sc_gather(out_sorted, dest.reshape(TOP_K * t))
    return _combine(gate, x2d, rows4.reshape(TOP_K, t, D_MODEL // 2))


def _prep_moe(norm2_g, w_router, b_router, w_moe_in, b_moe_in, w_moe_out, b_moe_out):
    r = jnp.arange(ROUTER_ROWS)
    wr_hi = w_router.T.astype(BF16)
    return dict(
        g2=norm2_g.reshape(1, D_MODEL),
        wr_hi=wr_hi,
        wr_lo=(w_router.T - wr_hi.astype(F32)).astype(BF16),
        br=b_router.reshape(N_EXPERTS, 1),
        tri=(r[:, None] < r[None, :]).astype(BF16),
        w1=w_moe_in,
        b1=b_moe_in.reshape(N_EXPERTS, 1, 2 * D_FF),
        w2=w_moe_out,
        b2=b_moe_out.reshape(N_EXPERTS, 1, D_MODEL),
    )


def kernel(x_prompt, x_sample, norm1_g, w_in, q_norm_g, k_norm_g, attn_sink, sgu_ln_g, sgu_ln_b, w_spatial,
           b_spatial, attn_out_g, sgu_out_g, w_out, norm2_g, w_router, b_router, w_moe_in, b_moe_in, w_moe_out,
           b_moe_out):
    p = _prep_params(norm1_g[0], w_in[0], q_norm_g[0], k_norm_g[0], attn_sink[0], sgu_ln_g[0], sgu_ln_b[0],
                     w_spatial[0], b_spatial[0], attn_out_g[0], sgu_out_g[0], w_out[0])
    m = _prep_moe(norm2_g[0], w_router[0], b_router[0], w_moe_in[0], b_moe_in[0], w_moe_out[0], b_moe_out[0])
    outs = []
    for x in (x_prompt, x_sample):
        x2 = _mix_half(x, p)
        outs.append(_moe_half(x2, m).reshape(x.shape))
    return tuple(outs)
```

```python
import functools
import math

import jax
import jax.numpy as jnp
from jax import lax
from jax.experimental import pallas as pl
from jax.experimental.pallas import tpu as pltpu
from jax.experimental.pallas import tpu_sc as plsc

D_MODEL = 1024
HEAD_DIM = 64
N_Q_HEADS = 8
N_KV_HEADS = 2
Q_PER_KV = N_Q_HEADS // N_KV_HEADS
ATTN_WIDTH = N_Q_HEADS * HEAD_DIM
KV_WIDTH = N_KV_HEADS * HEAD_DIM
QK_WIDTH = ATTN_WIDTH + KV_WIDTH
KV_DUP_WIDTH = 2 * KV_WIDTH
N_SGU_GROUPS = 8
SGU_GROUP_DIM = 64
SGU_WIDTH = N_SGU_GROUPS * SGU_GROUP_DIM
IN_PROJ_WIDTH = ATTN_WIDTH + 2 * KV_WIDTH + 2 * SGU_WIDTH
BLOCK = 128
ROPE_THETA = 500000.0
ROPE_DIM = HEAD_DIM // 4
N_EXPERTS = 32
TOP_K = 4
D_FF = D_MODEL
SWIGLU_LIMIT = 7.0
SWIGLU_ALPHA = 1.702
EPS = 1e-6
LOG2_E = 1.4426950408889634

LANES = 128
IN_PROJ_ROWS = 1024
IN_PROJ_CHUNK = 256
MIXER_ROWS = 1024
ROUTER_ROWS = 1024
DEST_ROWS = 2048
COMBINE_ROWS = 1024
EXPERT_ROWS = 2048
EXPERT_SUB = 1024
EXPERT_ROW_STEPS = (256, 512, 1024)
VMEM_LIMIT_BYTES = 56 * 1024 * 1024

F32 = jnp.float32
BF16 = jnp.bfloat16


def _cparams(*semantics):
    return pltpu.CompilerParams(dimension_semantics=semantics, vmem_limit_bytes=VMEM_LIMIT_BYTES)


def _dup_heads(tile):
    low = lax.broadcasted_iota(jnp.int32, tile.shape, 1) < HEAD_DIM
    swapped = pltpu.roll(tile, HEAD_DIM, axis=1)
    return jnp.where(low, tile, swapped), jnp.where(low, swapped, tile)


def _gelu_tanh(x):
    k = 2.0 * math.sqrt(2.0 / math.pi) * LOG2_E
    return x / (1.0 + jnp.exp2(x * (-k - (k * 0.044715) * (x * x))))


def _in_proj_kernel(x_ref, g1_ref, w_ref, qkg_ref, cos_ref, sina_ref, sinb_ref, seg_ref, lng_ref, lnb_ref,
                    q_ref, k_ref, v_ref, u_ref, vn_ref):
    for r0 in range(0, IN_PROJ_ROWS, IN_PROJ_CHUNK):
        rs = slice(r0, r0 + IN_PROJ_CHUNK)
        x = x_ref[rs, :]
        h = x * lax.rsqrt(jnp.mean(x * x, axis=-1, keepdims=True) + EPS) * g1_ref[...]
        z = jnp.dot(h.astype(BF16), w_ref[...], preferred_element_type=F32)

        qk = z[:, :QK_WIDTH]
        ss = jnp.dot((qk * qk).astype(BF16), seg_ref[...], preferred_element_type=F32)
        qkn = qk * lax.rsqrt(ss * (1.0 / HEAD_DIM) + EPS) * qkg_ref[...]
        cos, sina, sinb = cos_ref[rs, :], sina_ref[rs, :], sinb_ref[rs, :]
        for c in range(QK_WIDTH // LANES):
            xc = qkn[:, c * LANES:(c + 1) * LANES]
            up = pltpu.roll(xc, LANES - ROPE_DIM // 2, axis=1)
            dn = pltpu.roll(xc, ROPE_DIM // 2, axis=1)
            rc = xc * cos + up * sina + dn * sinb
            if c < ATTN_WIDTH // LANES:
                q_ref[rs, c * LANES:(c + 1) * LANES] = (rc * (HEAD_DIM ** -0.5 * LOG2_E)).astype(BF16)
            else:
                k0, k1 = _dup_heads(rc)
                k_ref[rs, :LANES] = k0.astype(BF16)
                k_ref[rs, LANES:] = k1.astype(BF16)

        v0, v1 = _dup_heads(z[:, QK_WIDTH:QK_WIDTH + KV_WIDTH])
        v_ref[rs, :LANES] = v0.astype(BF16)
        v_ref[rs, LANES:] = v1.astype(BF16)
        su = z[:, QK_WIDTH + KV_WIDTH:QK_WIDTH + KV_WIDTH + SGU_WIDTH]
        sv = z[:, QK_WIDTH + KV_WIDTH + SGU_WIDTH:]
        u_ref[rs, :] = _gelu_tanh(su).astype(BF16)
        gv = _gelu_tanh(sv)
        mu = jnp.mean(gv, axis=-1, keepdims=True)
        gc = gv - mu
        ln = gc * lax.rsqrt(jnp.mean(gc * gc, axis=-1, keepdims=True) + EPS) * lng_ref[...] + lnb_ref[...]
        vn_ref[rs, :] = ln.astype(BF16)


def _rope_tables(seq):
    half = ROPE_DIM // 2
    inv_freq = ROPE_THETA ** (-(jnp.arange(half, dtype=F32) * 2.0) / ROPE_DIM)
    ang = jnp.arange(seq).astype(F32)[:, None] * inv_freq[None, :]
    cos, sin = jnp.cos(ang), jnp.sin(ang)
    j = jnp.arange(LANES) % HEAD_DIM
    f = j % half
    cos_t = jnp.where(j[None, :] < ROPE_DIM, cos[:, f], 1.0)
    sina_t = jnp.where(j[None, :] < half, -sin[:, f], 0.0)
    sinb_t = jnp.where((j[None, :] >= half) & (j[None, :] < ROPE_DIM), sin[:, f], 0.0)
    return cos_t.astype(F32), sina_t.astype(F32), sinb_t.astype(F32)


def _in_proj(x2d, seq, g1, w_in, qkg, tables, seg, lng, lnb):
    t = x2d.shape[0]
    rows = IN_PROJ_ROWS
    n_seq = seq // rows
    const = lambda i: (0, 0)
    tab = pl.BlockSpec((rows, LANES), lambda i: (i % n_seq, 0))
    return pl.pallas_call(
        _in_proj_kernel,
        grid=(t // rows,),
        in_specs=[
            pl.BlockSpec((rows, D_MODEL), lambda i: (i, 0)),
            pl.BlockSpec((1, D_MODEL), const),
            pl.BlockSpec((D_MODEL, IN_PROJ_WIDTH), const),
            pl.BlockSpec((1, QK_WIDTH), const),
            tab, tab, tab,
            pl.BlockSpec((QK_WIDTH, QK_WIDTH), const),
            pl.BlockSpec((1, SGU_WIDTH), const),
            pl.BlockSpec((1, SGU_WIDTH), const),
        ],
        out_specs=[
            pl.BlockSpec((rows, ATTN_WIDTH), lambda i: (i, 0)),
            pl.BlockSpec((rows, KV_DUP_WIDTH), lambda i: (i, 0)),
            pl.BlockSpec((rows, KV_DUP_WIDTH), lambda i: (i, 0)),
            pl.BlockSpec((rows, SGU_WIDTH), lambda i: (i, 0)),
            pl.BlockSpec((rows, SGU_WIDTH), lambda i: (i, 0)),
        ],
        out_shape=[
            jax.ShapeDtypeStruct((t, ATTN_WIDTH), BF16),
            jax.ShapeDtypeStruct((t, KV_DUP_WIDTH), BF16),
            jax.ShapeDtypeStruct((t, KV_DUP_WIDTH), BF16),
            jax.ShapeDtypeStruct((t, SGU_WIDTH), BF16),
            jax.ShapeDtypeStruct((t, SGU_WIDTH), BF16),
        ],
        compiler_params=_cparams("parallel"),
        name="in_proj",
    )(x2d, g1, w_in, qkg, *tables, seg, lng, lnb)


def _mixer_kernel(sink_ref, q_ref, kp_ref, kc_ref, kn_ref, vp_ref, vc_ref, vx_ref, u_ref, g_ref, x_ref,
                  ws_ref, bs_ref, ag_ref, sg_ref, wo_ref, o_ref, mix_ref):
    i = pl.program_id(1)
    n_i = pl.num_programs(1)
    n_sub = MIXER_ROWS // BLOCK
    kwin = jnp.concatenate([kp_ref[...], kc_ref[...], kn_ref[...]], axis=0)
    vwin = jnp.concatenate([vp_ref[...], vc_ref[...], vx_ref[...]], axis=0)

    srows = Q_PER_KV * BLOCK
    r = lax.broadcasted_iota(jnp.int32, (srows, 3 * BLOCK), 0) & (BLOCK - 1)
    c = lax.broadcasted_iota(jnp.int32, (srows, 3 * BLOCK), 1)
    band = (c >= r) & (c <= r + 2 * BLOCK)
    hrow = lax.broadcasted_iota(jnp.int32, (srows, 1), 0) // BLOCK
    low = lax.broadcasted_iota(jnp.int32, (BLOCK, LANES), 1) < HEAD_DIM
    keep = (low.astype(BF16), (~low).astype(BF16))
    ones = jnp.ones((3 * BLOCK, LANES), BF16)

    for j in range(n_sub):
        valid = band
        if j == 0:
            valid = valid & ((c >= BLOCK) | (i > 0))
        if j == n_sub - 1:
            valid = valid & ((c < 2 * BLOCK) | (i < n_i - 1))
        kj = kwin[j * BLOCK:(j + 3) * BLOCK, :]
        vj = vwin[j * BLOCK:(j + 3) * BLOCK, :]
        a_tiles = []
        for hk in range(N_KV_HEADS):
            qs = jnp.concatenate(
                [q_ref[j * BLOCK:(j + 1) * BLOCK, (h // 2) * LANES:(h // 2 + 1) * LANES] * keep[h % 2]
                 for h in range(hk * Q_PER_KV, (hk + 1) * Q_PER_KV)], axis=0)
            kh = kj[:, hk * LANES:(hk + 1) * LANES]
            vh = vj[:, hk * LANES:(hk + 1) * LANES]
            s = lax.dot_general(qs, kh, (((1,), (1,)), ((), ())), preferred_element_type=F32)
            s = jnp.where(valid, s, -jnp.inf)
            sink = jnp.zeros((srows, 1), F32)
            for g in range(Q_PER_KV):
                sink = jnp.where(hrow == g, sink_ref[hk * Q_PER_KV + g] * LOG2_E, sink)
            m = jnp.maximum(jnp.max(s, axis=-1, keepdims=True), sink)
            p = jnp.exp2(s - m).astype(BF16)
            ov = jnp.dot(p, jnp.concatenate([vh, ones], axis=-1), preferred_element_type=F32)
            o = ov[:, :LANES] / (ov[:, LANES:] + jnp.exp2(sink - m))
            for g in range(0, Q_PER_KV, 2):
                a_tiles.append(jnp.where(low, o[g * BLOCK:(g + 1) * BLOCK, :], o[(g + 1) * BLOCK:(g + 2) * BLOCK, :]))
        a = jnp.concatenate(a_tiles, axis=-1)
        a = a * lax.rsqrt(jnp.mean(a * a, axis=-1, keepdims=True) + EPS) * ag_ref[...]

        mixed_tiles = []
        for t in range(SGU_WIDTH // LANES):
            vt = g_ref[j * BLOCK:(j + 1) * BLOCK, t * LANES:(t + 1) * LANES]
            mixed_tiles.append(jnp.where(low, jnp.dot(ws_ref[2 * t], vt, preferred_element_type=F32),
                                         jnp.dot(ws_ref[2 * t + 1], vt, preferred_element_type=F32)))
        mixed = jnp.concatenate(mixed_tiles, axis=-1) + bs_ref[...]
        gated = u_ref[j * BLOCK:(j + 1) * BLOCK, :].astype(F32) * mixed
        gated = gated * lax.rsqrt(jnp.mean(gated * gated, axis=-1, keepdims=True) + EPS) * sg_ref[...]
        mix_ref[j * BLOCK:(j + 1) * BLOCK, :] = jnp.concatenate([a, gated], axis=-1).astype(BF16)

    o_ref[...] = x_ref[...] + jnp.dot(mix_ref[...], wo_ref[...], preferred_element_type=F32)


def _mixer(batch, seq, sink, q, k, v, u, vn, x2d, ws, bs, ag, sg, wo):
    rows = MIXER_ROWS
    n_i = seq // rows
    sub = rows // BLOCK
    n_blk = batch * seq // BLOCK
    const2 = lambda b, i, s: (0, 0)
    cur = lambda b, i, s: (b * n_i + i, 0)
    prv = lambda b, i, s: (jnp.maximum((b * n_i + i) * sub - 1, 0), 0)
    nxt = lambda b, i, s: (jnp.minimum((b * n_i + i + 1) * sub, n_blk - 1), 0)
    grid_spec = pltpu.PrefetchScalarGridSpec(
        num_scalar_prefetch=1,
        grid=(batch, n_i),
        in_specs=[
            pl.BlockSpec((rows, ATTN_WIDTH), cur),
            pl.BlockSpec((BLOCK, KV_DUP_WIDTH), prv),
            pl.BlockSpec((rows, KV_DUP_WIDTH), cur),
            pl.BlockSpec((BLOCK, KV_DUP_WIDTH), nxt),
            pl.BlockSpec((BLOCK, KV_DUP_WIDTH), prv),
            pl.BlockSpec((rows, KV_DUP_WIDTH), cur),
            pl.BlockSpec((BLOCK, KV_DUP_WIDTH), nxt),
            pl.BlockSpec((rows, SGU_WIDTH), cur),
            pl.BlockSpec((rows, SGU_WIDTH), cur),
            pl.BlockSpec((rows, D_MODEL), cur),
            pl.BlockSpec((N_SGU_GROUPS, BLOCK, BLOCK), lambda b, i, s: (0, 0, 0)),
            pl.BlockSpec((BLOCK, SGU_WIDTH), const2),
            pl.BlockSpec((1, ATTN_WIDTH), const2),
            pl.BlockSpec((1, SGU_WIDTH), const2),
            pl.BlockSpec((D_MODEL, D_MODEL), const2),
        ],
        out_specs=pl.BlockSpec((rows, D_MODEL), cur),
        scratch_shapes=[pltpu.VMEM((rows, D_MODEL), BF16)],
    )
    return pl.pallas_call(
        _mixer_kernel,
        grid_spec=grid_spec,
        out_shape=jax.ShapeDtypeStruct((batch * seq, D_MODEL), F32),
        compiler_params=_cparams("parallel", "parallel"),
        name="mixer",
    )(sink, q, k, k, k, v, v, v, u, vn, x2d, ws, bs, ag, sg, wo)


def _mix_half(x, p):
    batch, seq, _ = x.shape
    x2d = x.reshape(batch * seq, D_MODEL)
    q, k, v, u, vn = _in_proj(x2d, seq, p["g1"], p["w_in"], p["qkg"], _rope_tables(seq), p["seg"], p["lng"],
                              p["lnb"])
    return _mixer(batch, seq, p["sink"], q, k, v, u, vn, x2d, p["ws"], p["bs"], p["ag"], p["sg"], p["wo"])


def _prep_params(norm1_g, w_in, q_norm_g, k_norm_g, attn_sink, sgu_ln_g, sgu_ln_b, w_spatial, b_spatial,
                 attn_out_g, sgu_out_g, w_out):
    head = jnp.arange(QK_WIDTH) // HEAD_DIM
    return dict(
        g1=norm1_g.reshape(1, D_MODEL),
        w_in=w_in.astype(BF16),
        qkg=jnp.concatenate([jnp.tile(q_norm_g, N_Q_HEADS), jnp.tile(k_norm_g, N_KV_HEADS)]).reshape(1, QK_WIDTH),
        seg=(head[:, None] == head[None, :]).astype(BF16),
        lng=sgu_ln_g.reshape(1, SGU_WIDTH),
        lnb=sgu_ln_b.reshape(1, SGU_WIDTH),
        sink=attn_sink.astype(F32),
        ws=w_spatial.astype(BF16),
        bs=jnp.repeat(b_spatial.T, SGU_GROUP_DIM, axis=1),
        ag=attn_out_g.reshape(1, ATTN_WIDTH),
        sg=sgu_out_g.reshape(1, SGU_WIDTH),
        wo=w_out.astype(BF16),
    )


def _pack_bf16_pair(lo, hi):
    lo_b = lax.bitcast_convert_type(lo.astype(BF16).astype(F32), jnp.uint32) >> 16
    hi_b = lax.bitcast_convert_type(hi.astype(BF16).astype(F32), jnp.uint32) & jnp.uint32(0xFFFF0000)
    return hi_b | lo_b


def _unpack_bf16_pair(packed):
    lo = lax.bitcast_convert_type(packed << 16, F32).astype(BF16)
    hi = lax.bitcast_convert_type(packed & jnp.uint32(0xFFFF0000), F32).astype(BF16)
    return lo, hi


def _router_kernel(x_ref, g2_ref, wh_ref, wl_ref, br_ref, tri_ref, xn_ref, idx_ref, rank_ref, gate_ref, cnt_ref,
                   run_ref):
    @pl.when(pl.program_id(0) == 0)
    def _():
        run_ref[...] = jnp.zeros_like(run_ref)

    x = x_ref[...]
    xn = x * lax.rsqrt(jnp.mean(x * x, axis=-1, keepdims=True) + EPS) * g2_ref[...]
    xn_ref[...] = lax.bitcast_convert_type(_pack_bf16_pair(xn[:, :D_MODEL // 2], xn[:, D_MODEL // 2:]), jnp.int32)

    xh = xn.astype(BF16)
    xl = (xn - xh.astype(F32)).astype(BF16)
    nt = (((1,), (1,)), ((), ()))
    logits = (lax.dot_general(wh_ref[...], xh, nt, preferred_element_type=F32)
              + lax.dot_general(wh_ref[...], xl, nt, preferred_element_type=F32)
              + lax.dot_general(wl_ref[...], xh, nt, preferred_element_type=F32)) + br_ref[...]
    rows = logits.shape[1]
    erow = lax.broadcasted_iota(jnp.int32, (N_EXPERTS, rows), 0)
    work = logits
    vals, sels = [], []
    for k in range(TOP_K):
        m = jnp.max(work, axis=0, keepdims=True)
        ik = jnp.min(jnp.where(work == m, erow, N_EXPERTS), axis=0, keepdims=True)
        sel = erow == ik
        idx_ref[k:k + 1, :] = ik
        vals.append(m)
        sels.append(sel)
        work = jnp.where(sel, -jnp.inf, work)

    exps = [jnp.exp(v - vals[0]) for v in vals]
    den = exps[0] + exps[1] + exps[2] + exps[3]
    gate_ref[...] = jnp.zeros_like(gate_ref)
    for k in range(TOP_K):
        gate_ref[k:k + 1, :] = exps[k] / den

    onehot = jnp.zeros((N_EXPERTS, rows), F32)
    for sel in sels:
        onehot = onehot + sel.astype(F32)
    before = jnp.dot(onehot.astype(BF16), tri_ref[...], preferred_element_type=F32) + run_ref[:, :1]
    for k in range(TOP_K):
        rank_ref[k:k + 1, :] = jnp.sum(jnp.where(sels[k], before, 0.0), axis=0, keepdims=True).astype(jnp.int32)
    run_ref[...] = run_ref[...] + jnp.sum(onehot, axis=1, keepdims=True)
    cnt_ref[...] = run_ref[...]


def _router(x2d, g2, wr_hi, wr_lo, br, tri):
    t = x2d.shape[0]
    rows = ROUTER_ROWS
    const = lambda i: (0, 0)
    return pl.pallas_call(
        _router_kernel,
        grid=(t // rows,),
        in_specs=[
            pl.BlockSpec((rows, D_MODEL), lambda i: (i, 0)),
            pl.BlockSpec((1, D_MODEL), const),
            pl.BlockSpec((N_EXPERTS, D_MODEL), const),
            pl.BlockSpec((N_EXPERTS, D_MODEL), const),
            pl.BlockSpec((N_EXPERTS, 1), const),
            pl.BlockSpec((rows, rows), const),
        ],
        out_specs=[
            pl.BlockSpec((rows, D_MODEL // 2), lambda i: (i, 0)),
            pl.BlockSpec((TOP_K, rows), lambda i: (0, i)),
            pl.BlockSpec((TOP_K, rows), lambda i: (0, i)),
            pl.BlockSpec((2 * TOP_K, rows), lambda i: (0, i)),
            pl.BlockSpec((N_EXPERTS, LANES), const),
        ],
        out_shape=[
            jax.ShapeDtypeStruct((t, D_MODEL // 2), jnp.int32),
            jax.ShapeDtypeStruct((TOP_K, t), jnp.int32),
            jax.ShapeDtypeStruct((TOP_K, t), jnp.int32),
            jax.ShapeDtypeStruct((2 * TOP_K, t), F32),
            jax.ShapeDtypeStruct((N_EXPERTS, LANES), F32),
        ],
        scratch_shapes=[pltpu.VMEM((N_EXPERTS, LANES), F32)],
        compiler_params=_cparams("arbitrary"),
        name="router",
    )(x2d, g2, wr_hi, wr_lo, br, tri)


def _dest_kernel(pstart_ref, idx_ref, rank_ref, dest_ref):
    idx = idx_ref[...]
    dest = rank_ref[...]
    for e in range(N_EXPERTS):
        dest = dest + jnp.where(idx == e, pstart_ref[e], 0)
    dest_ref[...] = dest


def _dest(pstart, idx, rank):
    t = idx.shape[1]
    rows = min(DEST_ROWS, t)
    blk =pl.BlockSpec((TOP_K, rows), lambda i, s: (0, i))
    grid_spec = pltpu.PrefetchScalarGridSpec(num_scalar_prefetch=1, grid=(t // rows,), in_specs=[blk, blk],
                                             out_specs=blk)
    return pl.pallas_call(
        _dest_kernel,
        grid_spec=grid_spec,
        out_shape=jax.ShapeDtypeStruct((TOP_K, t), jnp.int32),
        compiler_params=_cparams("parallel"),
        name="dest",
    )(pstart, idx, rank)


SC_CORES = 2
SC_SUBCORES = 16
SC_WORKERS = SC_CORES * SC_SUBCORES
SC_WINDOW = 128


def _sc_mesh():
    return plsc.VectorSubcoreMesh(core_axis_name="c", subcore_axis_name="s", num_cores=SC_CORES,
                                  num_subcores=SC_SUBCORES)


def _sc_worker():
    return lax.axis_index("s") * SC_CORES + lax.axis_index("c")


def _sc_scatter(rows, idx, cap):
    t, width = rows.shape
    n_idx = idx.shape[0]
    per_worker = t // SC_WORKERS
    assert per_worker * SC_WORKERS == t and per_worker % SC_WINDOW == 0
    idx_flat = idx.reshape(n_idx * t)

    @functools.partial(
        pl.kernel,
        mesh=_sc_mesh(),
        out_type=jax.ShapeDtypeStruct((cap, width), rows.dtype),
        scratch_types=[
            pltpu.VMEM((SC_WINDOW,), jnp.int32),
            pltpu.VMEM((SC_WINDOW, width), rows.dtype),
            pltpu.SemaphoreType.DMA,
        ],
        name="sc_scatter",
    )
    def scatter(rows_hbm, idx_hbm, out_hbm, idx_v, rows_v, sem):
        base = _sc_worker() * per_worker

        @pl.loop(0, per_worker // SC_WINDOW)
        def _(step):
            off = pl.multiple_of(base + step * SC_WINDOW, SC_WINDOW)
            pltpu.sync_copy(rows_hbm.at[pl.ds(off, SC_WINDOW)], rows_v)
            for k in range(n_idx):
                pltpu.sync_copy(idx_hbm.at[pl.ds(pl.multiple_of(k * t + off, SC_WINDOW), SC_WINDOW)], idx_v)
                pltpu.async_copy(rows_v, out_hbm.at[idx_v], sem).wait()

    return scatter(rows, idx_flat)


def _experts_kernel(blk_e_ref, blk_src_ref, blk_valid_ref, blk_first_ref, blk_next_ref, blk_slot_ref,
                    x_ref, w1_hbm, b1_ref, w2_hbm, b2_ref, o_ref, w1_buf, w2_buf, sems):
    del blk_src_ref
    b = pl.program_id(0)
    valid = blk_valid_ref[b]
    slot = blk_slot_ref[b]
    half = D_MODEL // 2

    def weight_copies(expert, s):
        return (pltpu.make_async_copy(w1_hbm.at[expert], w1_buf.at[s], sems.at[0, s]),
                pltpu.make_async_copy(w2_hbm.at[expert], w2_buf.at[s], sems.at[1, s]))

    @pl.when(b == 0)
    def _():
        for copy in weight_copies(blk_e_ref[0], slot):
            copy.start()

    @pl.when(blk_first_ref[b] == 1)
    def _():
        for copy in weight_copies(blk_e_ref[b], slot):
            copy.wait()

        @pl.when(blk_next_ref[b] >= 0)
        def _():
            for copy in weight_copies(blk_next_ref[b], 1 - slot):
                copy.start()

    w1_ref = w1_buf.at[slot]
    w2_ref = w2_buf.at[slot]

    def run(r0, rows, nvalid):
        row = lax.broadcasted_iota(jnp.int32, (rows, half), 0)
        x = jnp.where(row < nvalid, x_ref[r0:r0 + rows, :], 0)
        lo, hi = _unpack_bf16_pair(lax.bitcast_convert_type(x, jnp.uint32))
        h = (jnp.dot(lo, w1_ref[:half, :].astype(BF16), preferred_element_type=F32)
             + jnp.dot(hi, w1_ref[half:, :].astype(BF16), preferred_element_type=F32) + b1_ref[...])
        gate = jnp.minimum(h[:, :D_FF], SWIGLU_LIMIT)
        up = jnp.clip(h[:, D_FF:], -SWIGLU_LIMIT, SWIGLU_LIMIT)
        act = (up + 1.0) * (gate / (1.0 + jnp.exp2(gate * (-SWIGLU_ALPHA * LOG2_E))))
        o = jnp.dot(act.astype(BF16), w2_ref[...].astype(BF16), preferred_element_type=F32) + b2_ref[...]
        o_ref[r0:r0 + rows, :] = lax.bitcast_convert_type(_pack_bf16_pair(o[:, :half], o[:, half:]), jnp.int32)
        if rows < EXPERT_SUB:
            o_ref[r0 + rows:r0 + EXPERT_SUB, :] = jnp.zeros((EXPERT_SUB - rows, half), jnp.int32)

    for r0 in range(0, EXPERT_ROWS, EXPERT_SUB):
        nvalid = jnp.clip(valid - r0, 0, EXPERT_SUB)
        lower = 0
        for rows in EXPERT_ROW_STEPS:
            pl.when((nvalid > lower) & (nvalid <= rows))(functools.partial(run, r0, rows, nvalid))
            lower = rows

        @pl.when(nvalid == 0)
        def _(r0=r0):
            o_ref[r0:r0 + EXPERT_SUB, :] = jnp.zeros((EXPERT_SUB, half), jnp.int32)


def _experts(blk_e, blk_src, blk_valid, blk_first, blk_next, blk_slot, xs, w1, b1, w2, b2):
    cap = xs.shape[0]
    rows = EXPERT_ROWS
    grid_spec = pltpu.PrefetchScalarGridSpec(
        num_scalar_prefetch=6,
        grid=(cap // rows,),
        in_specs=[
            pl.BlockSpec((rows, D_MODEL // 2), lambda b, be, bs, *_: (bs[b], 0)),
            pl.BlockSpec(memory_space=pl.ANY),
            pl.BlockSpec((None, 1, 2 * D_FF), lambda b, be, *_: (be[b], 0, 0)),
            pl.BlockSpec(memory_space=pl.ANY),
            pl.BlockSpec((None, 1, D_MODEL), lambda b, be, *_: (be[b], 0, 0)),
        ],
        out_specs=pl.BlockSpec((rows, D_MODEL // 2), lambda b, *_: (b, 0)),
        scratch_shapes=[
            pltpu.VMEM((2, D_MODEL, 2 * D_FF), F32),
            pltpu.VMEM((2, D_FF, D_MODEL), F32),
            pltpu.SemaphoreType.DMA((2, 2)),
        ],
    )
    return pl.pallas_call(
        _experts_kernel,
        grid_spec=grid_spec,
        out_shape=jax.ShapeDtypeStruct((cap, D_MODEL // 2), jnp.int32),
        compiler_params=_cparams("arbitrary"),
        name="experts",
    )(blk_e, blk_src, blk_valid, blk_first, blk_next, blk_slot, xs, w1, b1, w2, b2)


def _sc_gather(table, idx):
    n = idx.shape[0]
    width = table.shape[1]
    per_worker = n // SC_WORKERS
    assert per_worker * SC_WORKERS == n and per_worker % SC_WINDOW == 0

    @functools.partial(
        pl.kernel,
        mesh=_sc_mesh(),
        out_type=jax.ShapeDtypeStruct((n, width), table.dtype),
        scratch_types=[
            pltpu.VMEM((SC_WINDOW,), jnp.int32),
            pltpu.VMEM((SC_WINDOW, width), table.dtype),
            pltpu.SemaphoreType.DMA,
        ],
        name="sc_gather",
    )
    def gather(table_hbm, idx_hbm, out_hbm, idx_v, rows_v, sem):
        base = _sc_worker() * per_worker

        @pl.loop(0, per_worker // SC_WINDOW)
        def _(step):
            off = pl.multiple_of(base + step * SC_WINDOW, SC_WINDOW)
            pltpu.sync_copy(idx_hbm.at[pl.ds(off, SC_WINDOW)], idx_v)
            pltpu.async_copy(table_hbm.at[idx_v], rows_v, sem).wait()
            pltpu.sync_copy(rows_v, out_hbm.at[pl.ds(off, SC_WINDOW)])

    return gather(table, idx)


def _combine_kernel(gate_ref, x_ref, rows_ref, y_ref):
    gate_t = gate_ref[...].T
    half = D_MODEL // 2
    lo_sum = x_ref[:, :half]
    hi_sum = x_ref[:, half:]
    for k in range(TOP_K):
        packed = lax.bitcast_convert_type(rows_ref[k], jnp.uint32)
        g = gate_t[:, k:k + 1]
        lo_sum = lo_sum + g * lax.bitcast_convert_type(packed << 16, F32)
        hi_sum = hi_sum + g * lax.bitcast_convert_type(packed & jnp.uint32(0xFFFF0000), F32)
    y_ref[:, :half] = lo_sum
    y_ref[:, half:] = hi_sum


def _combine(gate, x2d, rows4):
    t = x2d.shape[0]
    rows = COMBINE_ROWS
    return pl.pallas_call(
        _combine_kernel,
        grid=(t // rows,),
        in_specs=[
            pl.BlockSpec((2 * TOP_K, rows), lambda i: (0, i)),
            pl.BlockSpec((rows, D_MODEL), lambda i: (i, 0)),
            pl.BlockSpec((TOP_K, rows, D_MODEL // 2), lambda i: (0, i, 0)),
        ],
        out_specs=pl.BlockSpec((rows, D_MODEL), lambda i: (i, 0)),
        out_shape=jax.ShapeDtypeStruct((t, D_MODEL), F32),
        compiler_params=_cparams("parallel"),
        name="combine",
    )(gate, x2d, rows4)


def _moe_half(x2d, m):
    t = x2d.shape[0]
    rows = EXPERT_ROWS
    cap = t * TOP_K + N_EXPERTS * rows
    n_blk = cap // rows
    xn, idx, rank, gate, cnt = _router(x2d, m["g2"], m["wr_hi"], m["wr_lo"], m["br"], m["tri"])

    counts = cnt[:, 0].astype(jnp.int32)
    padded = (counts + rows - 1) // rows * rows
    pends = jnp.cumsum(padded)
    pstart = pends - padded
    n_used = pends[-1:] // rows
    blk_src = jnp.minimum(jnp.arange(n_blk, dtype=jnp.int32), n_used - 1)
    starts = (blk_src * rows)[:, None]
    owner = (pstart[None, :] <= starts) & (starts < pends[None, :])
    blk_e = jnp.sum(jnp.where(owner, jnp.arange(N_EXPERTS)[None, :], 0), axis=1).astype(jnp.int32)
    filled_to = jnp.sum(jnp.where(owner, (pstart + counts)[None, :], 0), axis=1)
    blk_valid = jnp.clip(filled_to - blk_src * rows, 0, rows)
    in_use = jnp.arange(n_blk) < n_used
    blk_valid = jnp.where(in_use, blk_valid, 0).astype(jnp.int32)
    expert = jnp.arange(N_EXPERTS)
    has_rows = counts > 0
    later = jnp.where(has_rows[None, :] & (expert[None, :] > expert[:, None]), expert[None, :], N_EXPERTS)
    next_expert = jnp.min(later, axis=1)
    next_expert = jnp.where(next_expert == N_EXPERTS, -1, next_expert)
    expert_slot = (jnp.cumsum(has_rows) - 1) % 2
    blk_first = (in_use & (jnp.sum(jnp.where(owner, pstart[None, :], 0), axis=1) == blk_src * rows)).astype(jnp.int32)
    blk_next = jnp.sum(jnp.where(owner, next_expert[None, :], 0), axis=1).astype(jnp.int32)
    blk_slot = jnp.sum(jnp.where(owner, expert_slot[None, :], 0), axis=1).astype(jnp.int32)

    dest = _dest(pstart.astype(jnp.int32), idx, rank)
    xs = _sc_scatter(xn, dest, cap)
    out_sorted = _experts(blk_e, blk_src, blk_valid, blk_first, blk_next, blk_slot, xs, m["w1"], m["b1"], m["w2"],
                          m["b2"])
    rows4 = _sc_gather(out_sorted, dest.reshape(TOP_K * t))
    return _combine(gate, x2d, rows4.reshape(TOP_K, t, D_MODEL // 2))


def _prep_moe(norm2_g, w_router, b_router, w_moe_in, b_moe_in, w_moe_out, b_moe_out):
    r = jnp.arange(ROUTER_ROWS)
    wr_hi = w_router.T.astype(BF16)
    return dict(
        g2=norm2_g.reshape(1, D_MODEL),
        wr_hi=wr_hi,
        wr_lo=(w_router.T - wr_hi.astype(F32)).astype(BF16),
        br=b_router.reshape(N_EXPERTS, 1),
        tri=(r[:, None] < r[None, :]).astype(BF16),
        w1=w_moe_in,
        b1=b_moe_in.reshape(N_EXPERTS, 1, 2 * D_FF),
        w2=w_moe_out,
        b2=b_moe_out.reshape(N_EXPERTS, 1, D_MODEL),
    )


def kernel(x_prompt, x_sample, norm1_g, w_in, q_norm_g, k_norm_g, attn_sink, sgu_ln_g, sgu_ln_b, w_spatial,
           b_spatial, attn_out_g, sgu_out_g, w_out, norm2_g, w_router, b_router, w_moe_in, b_moe_in, w_moe_out,
           b_moe_out):
    p = _prep_params(norm1_g[0], w_in[0], q_norm_g[0], k_norm_g[0], attn_sink[0], sgu_ln_g[0], sgu_ln_b[0],
                     w_spatial[0], b_spatial[0], attn_out_g[0], sgu_out_g[0], w_out[0])
    m = _prep_moe(norm2_g[0], w_router[0], b_router[0], w_moe_in[0], b_moe_in[0], w_moe_out[0], b_moe_out[0])
    outs = []
    for x in (x_prompt, x_sample):
        x2 = _mix_half(x, p)
        outs.append(_moe_half(x2, m).reshape(x.shape))
    return tuple(outs)
```

```python
import functools
import math

import jax
import jax.numpy as jnp
from jax import lax
from jax.experimental import pallas as pl
from jax.experimental.pallas import tpu as pltpu
from jax.experimental.pallas import tpu_sc as plsc

D_MODEL = 1024
HEAD_DIM = 64
N_Q_HEADS = 8
N_KV_HEADS = 2
Q_PER_KV = N_Q_HEADS // N_KV_HEADS
ATTN_WIDTH = N_Q_HEADS * HEAD_DIM
KV_WIDTH = N_KV_HEADS * HEAD_DIM
QK_WIDTH = ATTN_WIDTH + KV_WIDTH
KV_DUP_WIDTH = 2 * KV_WIDTH
N_SGU_GROUPS = 8
SGU_GROUP_DIM = 64
SGU_WIDTH = N_SGU_GROUPS * SGU_GROUP_DIM
IN_PROJ_WIDTH = ATTN_WIDTH + 2 * KV_WIDTH + 2 * SGU_WIDTH
BLOCK = 128
ROPE_THETA = 500000.0
ROPE_DIM = HEAD_DIM // 4
N_EXPERTS = 32
TOP_K = 4
D_FF = D_MODEL
SWIGLU_LIMIT = 7.0
SWIGLU_ALPHA = 1.702
EPS = 1e-6
LOG2_E = 1.4426950408889634

LANES = 128
IN_PROJ_ROWS = 1024
IN_PROJ_CHUNK = 256
MIXER_ROWS = 1024
ROUTER_ROWS = 1024
DEST_ROWS = 8192
COMBINE_ROWS = 1024
EXPERT_ROWS = 1024
EXPERT_ROW_STEPS = (256, 512, 768, 1024)
VMEM_LIMIT_BYTES = 56 * 1024 * 1024

F32 = jnp.float32
BF16 = jnp.bfloat16


def _cparams(*semantics):
    return pltpu.CompilerParams(dimension_semantics=semantics, vmem_limit_bytes=VMEM_LIMIT_BYTES)


def _dup_heads(tile):
    low = lax.broadcasted_iota(jnp.int32, tile.shape, 1) < HEAD_DIM
    swapped = pltpu.roll(tile, HEAD_DIM, axis=1)
    return jnp.where(low, tile, swapped), jnp.where(low, swapped, tile)


def _gelu_tanh(x):
    k = 2.0 * math.sqrt(2.0 / math.pi) * LOG2_E
    return x / (1.0 + jnp.exp2(x * (-k - (k * 0.044715) * (x * x))))


def _in_proj_kernel(x_ref, g1_ref, w_ref, qkg_ref, cos_ref, sina_ref, sinb_ref, seg_ref, lng_ref, lnb_ref,
                    q_ref, k_ref, v_ref, u_ref, vn_ref):
    for r0 in range(0, IN_PROJ_ROWS, IN_PROJ_CHUNK):
        rs = slice(r0, r0 + IN_PROJ_CHUNK)
        x = x_ref[rs, :]
        h = x * lax.rsqrt(jnp.mean(x * x, axis=-1, keepdims=True) + EPS) * g1_ref[...]
        z = jnp.dot(h.astype(BF16), w_ref[...], preferred_element_type=F32)

        qk = z[:, :QK_WIDTH]
        ss = jnp.dot((qk * qk).astype(BF16), seg_ref[...], preferred_element_type=F32)
        qkn = qk * lax.rsqrt(ss * (1.0 / HEAD_DIM) + EPS) * qkg_ref[...]
        cos, sina, sinb = cos_ref[rs, :], sina_ref[rs, :], sinb_ref[rs, :]
        for c in range(QK_WIDTH // LANES):
            xc = qkn[:, c * LANES:(c + 1) * LANES]
            up = pltpu.roll(xc, LANES - ROPE_DIM // 2, axis=1)
            dn = pltpu.roll(xc, ROPE_DIM // 2, axis=1)
            rc = xc * cos + up * sina + dn * sinb
            if c < ATTN_WIDTH // LANES:
                q_ref[rs, c * LANES:(c + 1) * LANES] = (rc * (HEAD_DIM ** -0.5 * LOG2_E)).astype(BF16)
            else:
                k0, k1 = _dup_heads(rc)
                k_ref[rs, :LANES] = k0.astype(BF16)
                k_ref[rs, LANES:] = k1.astype(BF16)

        v0, v1 = _dup_heads(z[:, QK_WIDTH:QK_WIDTH + KV_WIDTH])
        v_ref[rs, :LANES] = v0.astype(BF16)
        v_ref[rs, LANES:] = v1.astype(BF16)
        su = z[:, QK_WIDTH + KV_WIDTH:QK_WIDTH + KV_WIDTH + SGU_WIDTH]
        sv = z[:, QK_WIDTH + KV_WIDTH + SGU_WIDTH:]
        u_ref[rs, :] = _gelu_tanh(su).astype(BF16)
        gv = _gelu_tanh(sv)
        mu = jnp.mean(gv, axis=-1, keepdims=True)
        gc = gv - mu
        ln = gc * lax.rsqrt(jnp.mean(gc * gc, axis=-1, keepdims=True) + EPS) * lng_ref[...] + lnb_ref[...]
        vn_ref[rs, :] = ln.astype(BF16)


def _rope_tables(seq):
    half = ROPE_DIM // 2
    inv_freq = ROPE_THETA ** (-(jnp.arange(half, dtype=F32) * 2.0) / ROPE_DIM)
    ang = jnp.arange(seq).astype(F32)[:, None] * inv_freq[None, :]
    cos, sin = jnp.cos(ang), jnp.sin(ang)
    j = jnp.arange(LANES) % HEAD_DIM
    f = j % half
    cos_t = jnp.where(j[None, :] < ROPE_DIM, cos[:, f], 1.0)
    sina_t = jnp.where(j[None, :] < half, -sin[:, f], 0.0)
    sinb_t = jnp.where((j[None, :] >= half) & (j[None, :] < ROPE_DIM), sin[:, f], 0.0)
    return cos_t.astype(F32), sina_t.astype(F32), sinb_t.astype(F32)


def _in_proj(x2d, seq, g1, w_in, qkg, tables, seg, lng, lnb):
    t = x2d.shape[0]
    rows = IN_PROJ_ROWS
    n_seq = seq // rows
    const = lambda i: (0, 0)
    tab = pl.BlockSpec((rows, LANES), lambda i: (i % n_seq, 0))
    return pl.pallas_call(
        _in_proj_kernel,
        grid=(t // rows,),
        in_specs=[
            pl.BlockSpec((rows, D_MODEL), lambda i: (i, 0)),
            pl.BlockSpec((1, D_MODEL), const),
            pl.BlockSpec((D_MODEL, IN_PROJ_WIDTH), const),
            pl.BlockSpec((1, QK_WIDTH), const),
            tab, tab, tab,
            pl.BlockSpec((QK_WIDTH, QK_WIDTH), const),
            pl.BlockSpec((1, SGU_WIDTH), const),
            pl.BlockSpec((1, SGU_WIDTH), const),
        ],
        out_specs=[
            pl.BlockSpec((rows, ATTN_WIDTH), lambda i: (i, 0)),
            pl.BlockSpec((rows, KV_DUP_WIDTH), lambda i: (i, 0)),
            pl.BlockSpec((rows, KV_DUP_WIDTH), lambda i: (i, 0)),
            pl.BlockSpec((rows, SGU_WIDTH), lambda i: (i, 0)),
            pl.BlockSpec((rows, SGU_WIDTH), lambda i: (i, 0)),
        ],
        out_shape=[
            jax.ShapeDtypeStruct((t, ATTN_WIDTH), BF16),
            jax.ShapeDtypeStruct((t, KV_DUP_WIDTH), BF16),
            jax.ShapeDtypeStruct((t, KV_DUP_WIDTH), BF16),
            jax.ShapeDtypeStruct((t, SGU_WIDTH), BF16),
            jax.ShapeDtypeStruct((t, SGU_WIDTH), BF16),
        ],
        compiler_params=_cparams("parallel"),
        name="in_proj",
    )(x2d, g1, w_in, qkg, *tables, seg, lng, lnb)


def _mixer_kernel(sink_ref, q_ref, kp_ref, kc_ref, kn_ref, vp_ref, vc_ref, vx_ref, u_ref, g_ref, x_ref,
                  ws_ref, bs_ref, ag_ref, sg_ref, wo_ref, o_ref, mix_ref):
    i = pl.program_id(1)
    n_i = pl.num_programs(1)
    n_sub = MIXER_ROWS // BLOCK
    kwin = jnp.concatenate([kp_ref[...], kc_ref[...], kn_ref[...]], axis=0)
    vwin = jnp.concatenate([vp_ref[...], vc_ref[...], vx_ref[...]], axis=0)

    srows = Q_PER_KV * BLOCK
    r = lax.broadcasted_iota(jnp.int32, (srows, 3 * BLOCK), 0) & (BLOCK - 1)
    c = lax.broadcasted_iota(jnp.int32, (srows, 3 * BLOCK), 1)
    band = (c >= r) & (c <= r + 2 * BLOCK)
    hrow = lax.broadcasted_iota(jnp.int32, (srows, 1), 0) // BLOCK
    low = lax.broadcasted_iota(jnp.int32, (BLOCK, LANES), 1) < HEAD_DIM
    keep = (low.astype(BF16), (~low).astype(BF16))
    ones = jnp.ones((3 * BLOCK, LANES), BF16)

    for j in range(n_sub):
        valid = band
        if j == 0:
            valid = valid & ((c >= BLOCK) | (i > 0))
        if j == n_sub - 1:
            valid = valid & ((c < 2 * BLOCK) | (i < n_i - 1))
        kj = kwin[j * BLOCK:(j + 3) * BLOCK, :]
        vj = vwin[j * BLOCK:(j + 3) * BLOCK, :]
        a_tiles = []
        for hk in range(N_KV_HEADS):
            qs = jnp.concatenate(
                [q_ref[j * BLOCK:(j + 1) * BLOCK, (h // 2) * LANES:(h // 2 + 1) * LANES] * keep[h % 2]
                 for h in range(hk * Q_PER_KV, (hk + 1) * Q_PER_KV)], axis=0)
            kh = kj[:, hk * LANES:(hk + 1) * LANES]
            vh = vj[:, hk * LANES:(hk + 1) * LANES]
            s = lax.dot_general(qs, kh, (((1,), (1,)), ((), ())), preferred_element_type=F32)
            s = jnp.where(valid, s, -jnp.inf)
            sink = jnp.zeros((srows, 1), F32)
            for g in range(Q_PER_KV):
                sink = jnp.where(hrow == g, sink_ref[hk * Q_PER_KV + g] * LOG2_E, sink)
            m = jnp.maximum(jnp.max(s, axis=-1, keepdims=True), sink)
            p = jnp.exp2(s - m).astype(BF16)
            ov = jnp.dot(p, jnp.concatenate([vh, ones], axis=-1), preferred_element_type=F32)
            o = ov[:, :LANES] / (ov[:, LANES:] + jnp.exp2(sink - m))
            for g in range(0, Q_PER_KV, 2):
                a_tiles.append(jnp.where(low, o[g * BLOCK:(g + 1) * BLOCK, :], o[(g + 1) * BLOCK:(g + 2) * BLOCK, :]))
        a = jnp.concatenate(a_tiles, axis=-1)
        a = a * lax.rsqrt(jnp.mean(a * a, axis=-1, keepdims=True) + EPS) * ag_ref[...]

        mixed_tiles = []
        for t in range(SGU_WIDTH // LANES):
            vt = g_ref[j * BLOCK:(j + 1) * BLOCK, t * LANES:(t + 1) * LANES]
            mixed_tiles.append(jnp.where(low, jnp.dot(ws_ref[2 * t], vt, preferred_element_type=F32),
                                         jnp.dot(ws_ref[2 * t + 1], vt, preferred_element_type=F32)))
        mixed = jnp.concatenate(mixed_tiles, axis=-1) + bs_ref[...]
        gated = u_ref[j * BLOCK:(j + 1) * BLOCK, :].astype(F32) * mixed
        gated = gated * lax.rsqrt(jnp.mean(gated * gated, axis=-1, keepdims=True) + EPS) * sg_ref[...]
        mix_ref[j * BLOCK:(j + 1) * BLOCK, :] = jnp.concatenate([a, gated], axis=-1).astype(BF16)

    o_ref[...] = x_ref[...] + jnp.dot(mix_ref[...], wo_ref[...], preferred_element_type=F32)


def _mixer(batch, seq, sink, q, k, v, u, vn, x2d, ws, bs, ag, sg, wo):
    rows = MIXER_ROWS
    n_i = seq // rows
    sub = rows // BLOCK
    n_blk = batch * seq // BLOCK
    const2 = lambda b, i, s: (0, 0)
    cur = lambda b, i, s: (b * n_i + i, 0)
    prv = lambda b, i, s: (jnp.maximum((b * n_i + i) * sub - 1, 0), 0)
    nxt = lambda b, i, s: (jnp.minimum((b * n_i + i + 1) * sub, n_blk - 1), 0)
    grid_spec = pltpu.PrefetchScalarGridSpec(
        num_scalar_prefetch=1,
        grid=(batch, n_i),
        in_specs=[
            pl.BlockSpec((rows, ATTN_WIDTH), cur),
            pl.BlockSpec((BLOCK, KV_DUP_WIDTH), prv),
            pl.BlockSpec((rows, KV_DUP_WIDTH), cur),
            pl.BlockSpec((BLOCK, KV_DUP_WIDTH), nxt),
            pl.BlockSpec((BLOCK, KV_DUP_WIDTH), prv),
            pl.BlockSpec((rows, KV_DUP_WIDTH), cur),
            pl.BlockSpec((BLOCK, KV_DUP_WIDTH), nxt),
            pl.BlockSpec((rows, SGU_WIDTH), cur),
            pl.BlockSpec((rows, SGU_WIDTH), cur),
            pl.BlockSpec((rows, D_MODEL), cur),
            pl.BlockSpec((N_SGU_GROUPS, BLOCK, BLOCK), lambda b, i, s: (0, 0, 0)),
            pl.BlockSpec((BLOCK, SGU_WIDTH), const2),
            pl.BlockSpec((1, ATTN_WIDTH), const2),
            pl.BlockSpec((1, SGU_WIDTH), const2),
            pl.BlockSpec((D_MODEL, D_MODEL), const2),
        ],
        out_specs=pl.BlockSpec((rows, D_MODEL), cur),
        scratch_shapes=[pltpu.VMEM((rows, D_MODEL), BF16)],
    )
    return pl.pallas_call(
        _mixer_kernel,
        grid_spec=grid_spec,
        out_shape=jax.ShapeDtypeStruct((batch * seq, D_MODEL), F32),
        compiler_params=_cparams("parallel", "parallel"),
        name="mixer",
    )(sink, q, k, k, k, v, v, v, u, vn, x2d, ws, bs, ag, sg, wo)


def _mix_half(x, p):
    batch, seq, _ = x.shape
    x2d = x.reshape(batch * seq, D_MODEL)
    q, k, v, u, vn = _in_proj(x2d, seq, p["g1"], p["w_in"], p["qkg"], _rope_tables(seq), p["seg"], p["lng"],
                              p["lnb"])
    return _mixer(batch, seq, p["sink"], q, k, v, u, vn, x2d, p["ws"], p["bs"], p["ag"], p["sg"], p["wo"])


def _prep_params(norm1_g, w_in, q_norm_g, k_norm_g, attn_sink, sgu_ln_g, sgu_ln_b, w_spatial, b_spatial,
                 attn_out_g, sgu_out_g, w_out):
    head = jnp.arange(QK_WIDTH) // HEAD_DIM
    return dict(
        g1=norm1_g.reshape(1, D_MODEL),
        w_in=w_in.astype(BF16),
        qkg=jnp.concatenate([jnp.tile(q_norm_g, N_Q_HEADS), jnp.tile(k_norm_g, N_KV_HEADS)]).reshape(1, QK_WIDTH),
        seg=(head[:, None] == head[None, :]).astype(BF16),
        lng=sgu_ln_g.reshape(1, SGU_WIDTH),
        lnb=sgu_ln_b.reshape(1, SGU_WIDTH),
        sink=attn_sink.astype(F32),
        ws=w_spatial.astype(BF16),
        bs=jnp.repeat(b_spatial.T, SGU_GROUP_DIM, axis=1),
        ag=attn_out_g.reshape(1, ATTN_WIDTH),
        sg=sgu_out_g.reshape(1, SGU_WIDTH),
        wo=w_out.astype(BF16),
    )


def _pack_bf16_pair(lo, hi):
    lo_b = lax.bitcast_convert_type(lo.astype(BF16).astype(F32), jnp.uint32) >> 16
    hi_b = lax.bitcast_convert_type(hi.astype(BF16).astype(F32), jnp.uint32) & jnp.uint32(0xFFFF0000)
    return hi_b | lo_b


def _unpack_bf16_pair(packed):
    lo = lax.bitcast_convert_type(packed << 16, F32).astype(BF16)
    hi = lax.bitcast_convert_type(packed & jnp.uint32(0xFFFF0000), F32).astype(BF16)
    return lo, hi


def _router_kernel(x_ref, g2_ref, wh_ref, wl_ref, br_ref, tri_ref, xn_ref, idx_ref, rank_ref, gate_ref, cnt_ref,
                   run_ref):
    @pl.when(pl.program_id(0) == 0)
    def _():
        run_ref[...] = jnp.zeros_like(run_ref)

    x = x_ref[...]
    xn = x * lax.rsqrt(jnp.mean(x * x, axis=-1, keepdims=True) + EPS) * g2_ref[...]
    xn_ref[...] = lax.bitcast_convert_type(_pack_bf16_pair(xn[:, :D_MODEL // 2], xn[:, D_MODEL // 2:]), jnp.int32)

    xh = xn.astype(BF16)
    xl = (xn - xh.astype(F32)).astype(BF16)
    nt = (((1,), (1,)), ((), ()))
    logits = (lax.dot_general(wh_ref[...], xh, nt, preferred_element_type=F32)
              + lax.dot_general(wh_ref[...], xl, nt, preferred_element_type=F32)
              + lax.dot_general(wl_ref[...], xh, nt, preferred_element_type=F32)) + br_ref[...]
    rows = logits.shape[1]
    erow = lax.broadcasted_iota(jnp.int32, (N_EXPERTS, rows), 0)
    work = logits
    vals, sels = [], []
    for k in range(TOP_K):
        m = jnp.max(work, axis=0, keepdims=True)
        ik = jnp.min(jnp.where(work == m, erow, N_EXPERTS), axis=0, keepdims=True)
        sel = erow == ik
        idx_ref[k:k + 1, :] = ik
        vals.append(m)
        sels.append(sel)
        work = jnp.where(sel, -jnp.inf, work)

    exps = [jnp.exp(v - vals[0]) for v in vals]
    den = exps[0] + exps[1] + exps[2] + exps[3]
    gate_ref[...] = jnp.zeros_like(gate_ref)
    for k in range(TOP_K):
        gate_ref[k:k + 1, :] = exps[k] / den

    onehot = jnp.zeros((N_EXPERTS, rows), F32)
    for sel in sels:
        onehot = onehot + sel.astype(F32)
    before = jnp.dot(onehot.astype(BF16), tri_ref[...], preferred_element_type=F32) + run_ref[:, :1]
    for k in range(TOP_K):
        rank_ref[k:k + 1, :] = jnp.sum(jnp.where(sels[k], before, 0.0), axis=0, keepdims=True).astype(jnp.int32)
    run_ref[...] = run_ref[...] + jnp.sum(onehot, axis=1, keepdims=True)
    cnt_ref[...] = run_ref[...]


def _router(x2d, g2, wr_hi, wr_lo, br, tri):
    t = x2d.shape[0]
    rows = ROUTER_ROWS
    const = lambda i: (0, 0)
    return pl.pallas_call(
        _router_kernel,
        grid=(t // rows,),
        in_specs=[
            pl.BlockSpec((rows, D_MODEL), lambda i: (i, 0)),
            pl.BlockSpec((1, D_MODEL), const),
            pl.BlockSpec((N_EXPERTS, D_MODEL), const),
            pl.BlockSpec((N_EXPERTS, D_MODEL), const),
            pl.BlockSpec((N_EXPERTS, 1), const),
            pl.BlockSpec((rows, rows), const),
        ],
        out_specs=[
            pl.BlockSpec((rows, D_MODEL // 2), lambda i: (i, 0)),
            pl.BlockSpec((TOP_K, rows), lambda i: (0, i)),
            pl.BlockSpec((TOP_K, rows), lambda i: (0, i)),
            pl.BlockSpec((2 * TOP_K, rows), lambda i: (0, i)),
            pl.BlockSpec((N_EXPERTS, LANES), const),
        ],
        out_shape=[
            jax.ShapeDtypeStruct((t, D_MODEL // 2), jnp.int32),
            jax.ShapeDtypeStruct((TOP_K, t), jnp.int32),
            jax.ShapeDtypeStruct((TOP_K, t), jnp.int32),
            jax.ShapeDtypeStruct((2 * TOP_K, t), F32),
            jax.ShapeDtypeStruct((N_EXPERTS, LANES), F32),
        ],
        scratch_shapes=[pltpu.VMEM((N_EXPERTS, LANES), F32)],
        compiler_params=_cparams("arbitrary"),
        name="router",
    )(x2d, g2, wr_hi, wr_lo, br, tri)


def _dest_kernel(pstart_ref, idx_ref, rank_ref, dest_ref):
    idx = idx_ref[...]
    dest = rank_ref[...]
    for e in range(N_EXPERTS):
        dest = dest + jnp.where(idx == e, pstart_ref[e], 0)
    dest_ref[...] = dest


def _dest(pstart, idx, rank):
    t = idx.shape[1]
    rows = min(DEST_ROWS, t)
    blk =pl.BlockSpec((TOP_K, rows), lambda i, s: (0, i))
    grid_spec = pltpu.PrefetchScalarGridSpec(num_scalar_prefetch=1, grid=(t // rows,), in_specs=[blk, blk],
                                             out_specs=blk)
    return pl.pallas_call(
        _dest_kernel,
        grid_spec=grid_spec,
        out_shape=jax.ShapeDtypeStruct((TOP_K, t), jnp.int32),
        compiler_params=_cparams("parallel"),
        name="dest",
    )(pstart, idx, rank)


SC_CORES = 2
SC_SUBCORES = 16
SC_WORKERS = SC_CORES * SC_SUBCORES
SC_WINDOW = 128


def _sc_mesh():
    return plsc.VectorSubcoreMesh(core_axis_name="c", subcore_axis_name="s", num_cores=SC_CORES,
                                  num_subcores=SC_SUBCORES)


def _sc_worker():
    return lax.axis_index("s") * SC_CORES + lax.axis_index("c")


def _sc_scatter(rows, idx, cap):
    t, width = rows.shape
    n_idx = idx.shape[0]
    per_worker = t // SC_WORKERS
    assert per_worker * SC_WORKERS == t and per_worker % SC_WINDOW == 0
    idx_flat = idx.reshape(n_idx * t)

    @functools.partial(
        pl.kernel,
        mesh=_sc_mesh(),
        out_type=jax.ShapeDtypeStruct((cap, width), rows.dtype),
        scratch_types=[
            pltpu.VMEM((SC_WINDOW,), jnp.int32),
            pltpu.VMEM((SC_WINDOW, width), rows.dtype),
            pltpu.SemaphoreType.DMA,
        ],
        name="sc_scatter",
    )
    def scatter(rows_hbm, idx_hbm, out_hbm, idx_v, rows_v, sem):
        base = _sc_worker() * per_worker

        @pl.loop(0, per_worker // SC_WINDOW)
        def _(step):
            off = pl.multiple_of(base + step * SC_WINDOW, SC_WINDOW)
            pltpu.sync_copy(rows_hbm.at[pl.ds(off, SC_WINDOW)], rows_v)
            for k in range(n_idx):
                pltpu.sync_copy(idx_hbm.at[pl.ds(pl.multiple_of(k * t + off, SC_WINDOW), SC_WINDOW)], idx_v)
                pltpu.async_copy(rows_v, out_hbm.at[idx_v], sem).wait()

    return scatter(rows, idx_flat)


def _experts_kernel(blk_e_ref, blk_src_ref, blk_valid_ref, blk_first_ref, blk_next_ref, blk_slot_ref,
                    x_ref, w1_hbm, b1_ref, w2_hbm, b2_ref, o_ref, w1_buf, w2_buf, sems):
    del blk_src_ref
    b = pl.program_id(0)
    valid = blk_valid_ref[b]
    slot = blk_slot_ref[b]
    half = D_MODEL // 2

    def weight_copies(expert, s):
        return (pltpu.make_async_copy(w1_hbm.at[expert], w1_buf.at[s], sems.at[0, s]),
                pltpu.make_async_copy(w2_hbm.at[expert], w2_buf.at[s], sems.at[1, s]))

    @pl.when(b == 0)
    def _():
        for copy in weight_copies(blk_e_ref[0], slot):
            copy.start()

    @pl.when(blk_first_ref[b] == 1)
    def _():
        for copy in weight_copies(blk_e_ref[b], slot):
            copy.wait()

        @pl.when(blk_next_ref[b] >= 0)
        def _():
            for copy in weight_copies(blk_next_ref[b], 1 - slot):
                copy.start()

    w1_ref = w1_buf.at[slot]
    w2_ref = w2_buf.at[slot]

    def run(rows):
        row = lax.broadcasted_iota(jnp.int32, (rows, half), 0)
        x = jnp.where(row < valid, x_ref[:rows, :], 0)
        lo, hi = _unpack_bf16_pair(lax.bitcast_convert_type(x, jnp.uint32))
        h = (jnp.dot(lo, w1_ref[:half, :].astype(BF16), preferred_element_type=F32)
             + jnp.dot(hi, w1_ref[half:, :].astype(BF16), preferred_element_type=F32) + b1_ref[...])
        gate = jnp.minimum(h[:, :D_FF], SWIGLU_LIMIT)
        up = jnp.clip(h[:, D_FF:], -SWIGLU_LIMIT, SWIGLU_LIMIT)
        act = (up + 1.0) * (gate / (1.0 + jnp.exp2(gate * (-SWIGLU_ALPHA * LOG2_E))))
        o = jnp.dot(act.astype(BF16), w2_ref[...].astype(BF16), preferred_element_type=F32) + b2_ref[...]
        o_ref[:rows, :] = lax.bitcast_convert_type(_pack_bf16_pair(o[:, :half], o[:, half:]), jnp.int32)
        if rows < EXPERT_ROWS:
            o_ref[rows:, :] = jnp.zeros((EXPERT_ROWS - rows, half), jnp.int32)

    lower = 0
    for rows in EXPERT_ROW_STEPS:
        pl.when((valid > lower) & (valid <= rows))(functools.partial(run, rows))
        lower = rows

    @pl.when(valid == 0)
    def _():
        o_ref[...] = jnp.zeros_like(o_ref)


def _experts(blk_e, blk_src, blk_valid, blk_first, blk_next, blk_slot, xs, w1, b1, w2, b2):
    cap = xs.shape[0]
    rows = EXPERT_ROWS
    grid_spec = pltpu.PrefetchScalarGridSpec(
        num_scalar_prefetch=6,
        grid=(cap // rows,),
        in_specs=[
            pl.BlockSpec((rows, D_MODEL // 2), lambda b, be, bs, *_: (bs[b], 0)),
            pl.BlockSpec(memory_space=pl.ANY),
            pl.BlockSpec((None, 1, 2 * D_FF), lambda b, be, *_: (be[b], 0, 0)),
            pl.BlockSpec(memory_space=pl.ANY),
            pl.BlockSpec((None, 1, D_MODEL), lambda b, be, *_: (be[b], 0, 0)),
        ],
        out_specs=pl.BlockSpec((rows, D_MODEL // 2), lambda b, *_: (b, 0)),
        scratch_shapes=[
            pltpu.VMEM((2, D_MODEL, 2 * D_FF), F32),
            pltpu.VMEM((2, D_FF, D_MODEL), F32),
            pltpu.SemaphoreType.DMA((2, 2)),
        ],
    )
    return pl.pallas_call(
        _experts_kernel,
        grid_spec=grid_spec,
        out_shape=jax.ShapeDtypeStruct((cap, D_MODEL // 2), jnp.int32),
        compiler_params=_cparams("arbitrary"),
        name="experts",
    )(blk_e, blk_src, blk_valid, blk_first, blk_next, blk_slot, xs, w1, b1, w2, b2)


def _sc_gather(table, idx):
    n = idx.shape[0]
    width = table.shape[1]
    per_worker = n // SC_WORKERS
    assert per_worker * SC_WORKERS == n and per_worker % SC_WINDOW == 0

    @functools.partial(
        pl.kernel,
        mesh=_sc_mesh(),
        out_type=jax.ShapeDtypeStruct((n, width), table.dtype),
        scratch_types=[
            pltpu.VMEM((SC_WINDOW,), jnp.int32),
            pltpu.VMEM((SC_WINDOW, width), table.dtype),
            pltpu.SemaphoreType.DMA,
        ],
        name="sc_gather",
    )
    def gather(table_hbm, idx_hbm, out_hbm, idx_v, rows_v, sem):
        base = _sc_worker() * per_worker

        @pl.loop(0, per_worker // SC_WINDOW)
        def _(step):
            off = pl.multiple_of(base + step * SC_WINDOW, SC_WINDOW)
            pltpu.sync_copy(idx_hbm.at[pl.ds(off, SC_WINDOW)], idx_v)
            pltpu.async_copy(table_hbm.at[idx_v], rows_v, sem).wait()
            pltpu.sync_copy(rows_v, out_hbm.at[pl.ds(off, SC_WINDOW)])

    return gather(table, idx)


def _combine_kernel(gate_ref, x_ref, rows_ref, y_ref):
    gate_t = gate_ref[...].T
    half = D_MODEL // 2
    lo_sum = x_ref[:, :half]
    hi_sum = x_ref[:, half:]
    for k in range(TOP_K):
        packed = lax.bitcast_convert_type(rows_ref[k], jnp.uint32)
        g = gate_t[:, k:k + 1]
        lo_sum = lo_sum + g * lax.bitcast_convert_type(packed << 16, F32)
        hi_sum = hi_sum + g * lax.bitcast_convert_type(packed & jnp.uint32(0xFFFF0000), F32)
    y_ref[:, :half] = lo_sum
    y_ref[:, half:] = hi_sum


def _combine(gate, x2d, rows4):
    t = x2d.shape[0]
    rows = COMBINE_ROWS
    return pl.pallas_call(
        _combine_kernel,
        grid=(t // rows,),
        in_specs=[
            pl.BlockSpec((2 * TOP_K, rows), lambda i: (0, i)),
            pl.BlockSpec((rows, D_MODEL), lambda i: (i, 0)),
            pl.BlockSpec((TOP_K, rows, D_MODEL // 2), lambda i: (0, i, 0)),
        ],
        out_specs=pl.BlockSpec((rows, D_MODEL), lambda i: (i, 0)),
        out_shape=jax.ShapeDtypeStruct((t, D_MODEL), F32),
        compiler_params=_cparams("parallel"),
        name="combine",
    )(gate, x2d, rows4)


def _moe_half(x2d, m):
    t = x2d.shape[0]
    rows = EXPERT_ROWS
    cap = t * TOP_K + N_EXPERTS * rows
    n_blk = cap // rows
    xn, idx, rank, gate, cnt = _router(x2d, m["g2"], m["wr_hi"], m["wr_lo"], m["br"], m["tri"])

    counts = cnt[:, 0].astype(jnp.int32)
    padded = (counts + rows - 1) // rows * rows
    pends = jnp.cumsum(padded)
    pstart = pends - padded
    n_used = pends[-1:] // rows
    blk_src = jnp.minimum(jnp.arange(n_blk, dtype=jnp.int32), n_used - 1)
    starts = (blk_src * rows)[:, None]
    owner = (pstart[None, :] <= starts) & (starts < pends[None, :])
    blk_e = jnp.sum(jnp.where(owner, jnp.arange(N_EXPERTS)[None, :], 0), axis=1).astype(jnp.int32)
    filled_to = jnp.sum(jnp.where(owner, (pstart + counts)[None, :], 0), axis=1)
    blk_valid = jnp.clip(filled_to - blk_src * rows, 0, rows)
    in_use = jnp.arange(n_blk) < n_used
    blk_valid = jnp.where(in_use, blk_valid, 0).astype(jnp.int32)
    expert = jnp.arange(N_EXPERTS)
    has_rows = counts > 0
    later = jnp.where(has_rows[None, :] & (expert[None, :] > expert[:, None]), expert[None, :], N_EXPERTS)
    next_expert = jnp.min(later, axis=1)
    next_expert = jnp.where(next_expert == N_EXPERTS, -1, next_expert)
    expert_slot = (jnp.cumsum(has_rows) - 1) % 2
    blk_first = (in_use & (jnp.sum(jnp.where(owner, pstart[None, :], 0), axis=1) == blk_src * rows)).astype(jnp.int32)
    blk_next = jnp.sum(jnp.where(owner, next_expert[None, :], 0), axis=1).astype(jnp.int32)
    blk_slot = jnp.sum(jnp.where(owner, expert_slot[None, :], 0), axis=1).astype(jnp.int32)

    dest = _dest(pstart.astype(jnp.int32), idx, rank)
    xs = _sc_scatter(xn, dest, cap)
    out_sorted = _experts(blk_e, blk_src, blk_valid, blk_first, blk_next, blk_slot, xs, m["w1"], m["b1"], m["w2"],
                          m["b2"])
    rows4 = _sc_gather(out_sorted, dest.reshape(TOP_K * t))
    return _combine(gate, x2d, rows4.reshape(TOP_K, t, D_MODEL // 2))


def _prep_moe(norm2_g, w_router, b_router, w_moe_in, b_moe_in, w_moe_out, b_moe_out):
    r = jnp.arange(ROUTER_ROWS)
    wr_hi = w_router.T.astype(BF16)
    return dict(
        g2=norm2_g.reshape(1, D_MODEL),
        wr_hi=wr_hi,
        wr_lo=(w_router.T - wr_hi.astype(F32)).astype(BF16),
        br=b_router.reshape(N_EXPERTS, 1),
        tri=(r[:, None] < r[None, :]).astype(BF16),
        w1=w_moe_in,
        b1=b_moe_in.reshape(N_EXPERTS, 1, 2 * D_FF),
        w2=w_moe_out,
        b2=b_moe_out.reshape(N_EXPERTS, 1, D_MODEL),
    )


def kernel(x_prompt, x_sample, norm1_g, w_in, q_norm_g, k_norm_g, attn_sink, sgu_ln_g, sgu_ln_b, w_spatial,
           b_spatial, attn_out_g, sgu_out_g, w_out, norm2_g, w_router, b_router, w_moe_in, b_moe_in, w_moe_out,
           b_moe_out):
    p = _prep_params(norm1_g[0], w_in[0], q_norm_g[0], k_norm_g[0], attn_sink[0], sgu_ln_g[0], sgu_ln_b[0],
                     w_spatial[0], b_spatial[0], attn_out_g[0], sgu_out_g[0], w_out[0])
    m = _prep_moe(norm2_g[0], w_router[0], b_router[0], w_moe_in[0], b_moe_in[0], w_moe_out[0], b_moe_out[0])
    outs = []
    for x in (x_prompt, x_sample):
        x2 = _mix_half(x, p)
        outs.append(_moe_half(x2, m).reshape(x.shape))
    return tuple(outs)
```

```python
import functools
import math

import jax
import jax.numpy as jnp
from jax import lax
from jax.experimental import pallas as pl
from jax.experimental.pallas import tpu as pltpu
from jax.experimental.pallas import tpu_sc as plsc

D_MODEL = 1024
HEAD_DIM = 64
N_Q_HEADS = 8
N_KV_HEADS = 2
Q_PER_KV = N_Q_HEADS // N_KV_HEADS
ATTN_WIDTH = N_Q_HEADS * HEAD_DIM
KV_WIDTH = N_KV_HEADS * HEAD_DIM
QK_WIDTH = ATTN_WIDTH + KV_WIDTH
KV_DUP_WIDTH = 2 * KV_WIDTH
N_SGU_GROUPS = 8
SGU_GROUP_DIM = 64
SGU_WIDTH = N_SGU_GROUPS * SGU_GROUP_DIM
IN_PROJ_WIDTH = ATTN_WIDTH + 2 * KV_WIDTH + 2 * SGU_WIDTH
BLOCK = 128
ROPE_THETA = 500000.0
ROPE_DIM = HEAD_DIM // 4
N_EXPERTS = 32
TOP_K = 4
D_FF = D_MODEL
SWIGLU_LIMIT = 7.0
SWIGLU_ALPHA = 1.702
EPS = 1e-6
LOG2_E = 1.4426950408889634

LANES = 128
IN_PROJ_ROWS = 1024
IN_PROJ_CHUNK = 256
MIXER_ROWS = 1024
ROUTER_ROWS = 1024
DEST_ROWS = 8192
COMBINE_ROWS = 1024
EXPERT_ROWS = 1024
EXPERT_ROW_STEPS = (128, 256, 384, 512, 640, 768, 896, 1024)
VMEM_LIMIT_BYTES = 56 * 1024 * 1024

F32 = jnp.float32
BF16 = jnp.bfloat16


def _cparams(*semantics):
    return pltpu.CompilerParams(dimension_semantics=semantics, vmem_limit_bytes=VMEM_LIMIT_BYTES)


def _dup_heads(tile):
    low = lax.broadcasted_iota(jnp.int32, tile.shape, 1) < HEAD_DIM
    swapped = pltpu.roll(tile, HEAD_DIM, axis=1)
    return jnp.where(low, tile, swapped), jnp.where(low, swapped, tile)


def _gelu_tanh(x):
    k = 2.0 * math.sqrt(2.0 / math.pi) * LOG2_E
    return x / (1.0 + jnp.exp2(x * (-k - (k * 0.044715) * (x * x))))


def _in_proj_kernel(x_ref, g1_ref, w_ref, qkg_ref, cos_ref, sina_ref, sinb_ref, seg_ref, lng_ref, lnb_ref,
                    q_ref, k_ref, v_ref, u_ref, vn_ref):
    for r0 in range(0, IN_PROJ_ROWS, IN_PROJ_CHUNK):
        rs = slice(r0, r0 + IN_PROJ_CHUNK)
        x = x_ref[rs, :]
        h = x * lax.rsqrt(jnp.mean(x * x, axis=-1, keepdims=True) + EPS) * g1_ref[...]
        z = jnp.dot(h.astype(BF16), w_ref[...], preferred_element_type=F32)

        qk = z[:, :QK_WIDTH]
        ss = jnp.dot((qk * qk).astype(BF16), seg_ref[...], preferred_element_type=F32)
        qkn = qk * lax.rsqrt(ss * (1.0 / HEAD_DIM) + EPS) * qkg_ref[...]
        cos, sina, sinb = cos_ref[rs, :], sina_ref[rs, :], sinb_ref[rs, :]
        for c in range(QK_WIDTH // LANES):
            xc = qkn[:, c * LANES:(c + 1) * LANES]
            up = pltpu.roll(xc, LANES - ROPE_DIM // 2, axis=1)
            dn = pltpu.roll(xc, ROPE_DIM // 2, axis=1)
            rc = xc * cos + up * sina + dn * sinb
            if c < ATTN_WIDTH // LANES:
                q_ref[rs, c * LANES:(c + 1) * LANES] = (rc * (HEAD_DIM ** -0.5 * LOG2_E)).astype(BF16)
            else:
                k0, k1 = _dup_heads(rc)
                k_ref[rs, :LANES] = k0.astype(BF16)
                k_ref[rs, LANES:] = k1.astype(BF16)

        v0, v1 = _dup_heads(z[:, QK_WIDTH:QK_WIDTH + KV_WIDTH])
        v_ref[rs, :LANES] = v0.astype(BF16)
        v_ref[rs, LANES:] = v1.astype(BF16)
        su = z[:, QK_WIDTH + KV_WIDTH:QK_WIDTH + KV_WIDTH + SGU_WIDTH]
        sv = z[:, QK_WIDTH + KV_WIDTH + SGU_WIDTH:]
        u_ref[rs, :] = _gelu_tanh(su).astype(BF16)
        gv = _gelu_tanh(sv)
        mu = jnp.mean(gv, axis=-1, keepdims=True)
        gc = gv - mu
        ln = gc * lax.rsqrt(jnp.mean(gc * gc, axis=-1, keepdims=True) + EPS) * lng_ref[...] + lnb_ref[...]
        vn_ref[rs, :] = ln.astype(BF16)


def _rope_tables(seq):
    half = ROPE_DIM // 2
    inv_freq = ROPE_THETA ** (-(jnp.arange(half, dtype=F32) * 2.0) / ROPE_DIM)
    ang = jnp.arange(seq).astype(F32)[:, None] * inv_freq[None, :]
    cos, sin = jnp.cos(ang), jnp.sin(ang)
    j = jnp.arange(LANES) % HEAD_DIM
    f = j % half
    cos_t = jnp.where(j[None, :] < ROPE_DIM, cos[:, f], 1.0)
    sina_t = jnp.where(j[None, :] < half, -sin[:, f], 0.0)
    sinb_t = jnp.where((j[None, :] >= half) & (j[None, :] < ROPE_DIM), sin[:, f], 0.0)
    return cos_t.astype(F32), sina_t.astype(F32), sinb_t.astype(F32)


def _in_proj(x2d, seq, g1, w_in, qkg, tables, seg, lng, lnb):
    t = x2d.shape[0]
    rows = IN_PROJ_ROWS
    n_seq = seq // rows
    const = lambda i: (0, 0)
    tab = pl.BlockSpec((rows, LANES), lambda i: (i % n_seq, 0))
    return pl.pallas_call(
        _in_proj_kernel,
        grid=(t // rows,),
        in_specs=[
            pl.BlockSpec((rows, D_MODEL), lambda i: (i, 0)),
            pl.BlockSpec((1, D_MODEL), const),
            pl.BlockSpec((D_MODEL, IN_PROJ_WIDTH), const),
            pl.BlockSpec((1, QK_WIDTH), const),
            tab, tab, tab,
            pl.BlockSpec((QK_WIDTH, QK_WIDTH), const),
            pl.BlockSpec((1, SGU_WIDTH), const),
            pl.BlockSpec((1, SGU_WIDTH), const),
        ],
        out_specs=[
            pl.BlockSpec((rows, ATTN_WIDTH), lambda i: (i, 0)),
            pl.BlockSpec((rows, KV_DUP_WIDTH), lambda i: (i, 0)),
            pl.BlockSpec((rows, KV_DUP_WIDTH), lambda i: (i, 0)),
            pl.BlockSpec((rows, SGU_WIDTH), lambda i: (i, 0)),
            pl.BlockSpec((rows, SGU_WIDTH), lambda i: (i, 0)),
        ],
        out_shape=[
            jax.ShapeDtypeStruct((t, ATTN_WIDTH), BF16),
            jax.ShapeDtypeStruct((t, KV_DUP_WIDTH), BF16),
            jax.ShapeDtypeStruct((t, KV_DUP_WIDTH), BF16),
            jax.ShapeDtypeStruct((t, SGU_WIDTH), BF16),
            jax.ShapeDtypeStruct((t, SGU_WIDTH), BF16),
        ],
        compiler_params=_cparams("parallel"),
        name="in_proj",
    )(x2d, g1, w_in, qkg, *tables, seg, lng, lnb)


def _mixer_kernel(sink_ref, q_ref, kp_ref, kc_ref, kn_ref, vp_ref, vc_ref, vx_ref, u_ref, g_ref, x_ref,
                  ws_ref, bs_ref, ag_ref, sg_ref, wo_ref, o_ref, mix_ref):
    i = pl.program_id(1)
    n_i = pl.num_programs(1)
    n_sub = MIXER_ROWS // BLOCK
    kwin = jnp.concatenate([kp_ref[...], kc_ref[...], kn_ref[...]], axis=0)
    vwin = jnp.concatenate([vp_ref[...], vc_ref[...], vx_ref[...]], axis=0)

    srows = Q_PER_KV * BLOCK
    r = lax.broadcasted_iota(jnp.int32, (srows, 3 * BLOCK), 0) & (BLOCK - 1)
    c = lax.broadcasted_iota(jnp.int32, (srows, 3 * BLOCK), 1)
    band = (c >= r) & (c <= r + 2 * BLOCK)
    hrow = lax.broadcasted_iota(jnp.int32, (srows, 1), 0) // BLOCK
    low = lax.broadcasted_iota(jnp.int32, (BLOCK, LANES), 1) < HEAD_DIM
    keep = (low.astype(BF16), (~low).astype(BF16))
    ones = jnp.ones((3 * BLOCK, LANES), BF16)

    for j in range(n_sub):
        valid = band
        if j == 0:
            valid = valid & ((c >= BLOCK) | (i > 0))
        if j == n_sub - 1:
            valid = valid & ((c < 2 * BLOCK) | (i < n_i - 1))
        kj = kwin[j * BLOCK:(j + 3) * BLOCK, :]
        vj = vwin[j * BLOCK:(j + 3) * BLOCK, :]
        a_tiles = []
        for hk in range(N_KV_HEADS):
            qs = jnp.concatenate(
                [q_ref[j * BLOCK:(j + 1) * BLOCK, (h // 2) * LANES:(h // 2 + 1) * LANES] * keep[h % 2]
                 for h in range(hk * Q_PER_KV, (hk + 1) * Q_PER_KV)], axis=0)
            kh = kj[:, hk * LANES:(hk + 1) * LANES]
            vh = vj[:, hk * LANES:(hk + 1) * LANES]
            s = lax.dot_general(qs, kh, (((1,), (1,)), ((), ())), preferred_element_type=F32)
            s = jnp.where(valid, s, -jnp.inf)
            sink = jnp.zeros((srows, 1), F32)
            for g in range(Q_PER_KV):
                sink = jnp.where(hrow == g, sink_ref[hk * Q_PER_KV + g] * LOG2_E, sink)
            m = jnp.maximum(jnp.max(s, axis=-1, keepdims=True), sink)
            p = jnp.exp2(s - m).astype(BF16)
            ov = jnp.dot(p, jnp.concatenate([vh, ones], axis=-1), preferred_element_type=F32)
            o = ov[:, :LANES] / (ov[:, LANES:] + jnp.exp2(sink - m))
            for g in range(0, Q_PER_KV, 2):
                a_tiles.append(jnp.where(low, o[g * BLOCK:(g + 1) * BLOCK, :], o[(g + 1) * BLOCK:(g + 2) * BLOCK, :]))
        a = jnp.concatenate(a_tiles, axis=-1)
        a = a * lax.rsqrt(jnp.mean(a * a, axis=-1, keepdims=True) + EPS) * ag_ref[...]

        mixed_tiles = []
        for t in range(SGU_WIDTH // LANES):
            vt = g_ref[j * BLOCK:(j + 1) * BLOCK, t * LANES:(t + 1) * LANES]
            mixed_tiles.append(jnp.where(low, jnp.dot(ws_ref[2 * t], vt, preferred_element_type=F32),
                                         jnp.dot(ws_ref[2 * t + 1], vt, preferred_element_type=F32)))
        mixed = jnp.concatenate(mixed_tiles, axis=-1) + bs_ref[...]
        gated = u_ref[j * BLOCK:(j + 1) * BLOCK, :].astype(F32) * mixed
        gated = gated * lax.rsqrt(jnp.mean(gated * gated, axis=-1, keepdims=True) + EPS) * sg_ref[...]
        mix_ref[j * BLOCK:(j + 1) * BLOCK, :] = jnp.concatenate([a, gated], axis=-1).astype(BF16)

    o_ref[...] = x_ref[...] + jnp.dot(mix_ref[...], wo_ref[...], preferred_element_type=F32)


def _mixer(batch, seq, sink, q, k, v, u, vn, x2d, ws, bs, ag, sg, wo):
    rows = MIXER_ROWS
    n_i = seq // rows
    sub = rows // BLOCK
    n_blk = batch * seq // BLOCK
    const2 = lambda b, i, s: (0, 0)
    cur = lambda b, i, s: (b * n_i + i, 0)
    prv = lambda b, i, s: (jnp.maximum((b * n_i + i) * sub - 1, 0), 0)
    nxt = lambda b, i, s: (jnp.minimum((b * n_i + i + 1) * sub, n_blk - 1), 0)
    grid_spec = pltpu.PrefetchScalarGridSpec(
        num_scalar_prefetch=1,
        grid=(batch, n_i),
        in_specs=[
            pl.BlockSpec((rows, ATTN_WIDTH), cur),
            pl.BlockSpec((BLOCK, KV_DUP_WIDTH), prv),
            pl.BlockSpec((rows, KV_DUP_WIDTH), cur),
            pl.BlockSpec((BLOCK, KV_DUP_WIDTH), nxt),
            pl.BlockSpec((BLOCK, KV_DUP_WIDTH), prv),
            pl.BlockSpec((rows, KV_DUP_WIDTH), cur),
            pl.BlockSpec((BLOCK, KV_DUP_WIDTH), nxt),
            pl.BlockSpec((rows, SGU_WIDTH), cur),
            pl.BlockSpec((rows, SGU_WIDTH), cur),
            pl.BlockSpec((rows, D_MODEL), cur),
            pl.BlockSpec((N_SGU_GROUPS, BLOCK, BLOCK), lambda b, i, s: (0, 0, 0)),
            pl.BlockSpec((BLOCK, SGU_WIDTH), const2),
            pl.BlockSpec((1, ATTN_WIDTH), const2),
            pl.BlockSpec((1, SGU_WIDTH), const2),
            pl.BlockSpec((D_MODEL, D_MODEL), const2),
        ],
        out_specs=pl.BlockSpec((rows, D_MODEL), cur),
        scratch_shapes=[pltpu.VMEM((rows, D_MODEL), BF16)],
    )
    return pl.pallas_call(
        _mixer_kernel,
        grid_spec=grid_spec,
        out_shape=jax.ShapeDtypeStruct((batch * seq, D_MODEL), F32),
        compiler_params=_cparams("parallel", "parallel"),
        name="mixer",
    )(sink, q, k, k, k, v, v, v, u, vn, x2d, ws, bs, ag, sg, wo)


def _mix_half(x, p):
    batch, seq, _ = x.shape
    x2d = x.reshape(batch * seq, D_MODEL)
    q, k, v, u, vn = _in_proj(x2d, seq, p["g1"], p["w_in"], p["qkg"], _rope_tables(seq), p["seg"], p["lng"],
                              p["lnb"])
    return _mixer(batch, seq, p["sink"], q, k, v, u, vn, x2d, p["ws"], p["bs"], p["ag"], p["sg"], p["wo"])


def _prep_params(norm1_g, w_in, q_norm_g, k_norm_g, attn_sink, sgu_ln_g, sgu_ln_b, w_spatial, b_spatial,
                 attn_out_g, sgu_out_g, w_out):
    head = jnp.arange(QK_WIDTH) // HEAD_DIM
    return dict(
        g1=norm1_g.reshape(1, D_MODEL),
        w_in=w_in.astype(BF16),
        qkg=jnp.concatenate([jnp.tile(q_norm_g, N_Q_HEADS), jnp.tile(k_norm_g, N_KV_HEADS)]).reshape(1, QK_WIDTH),
        seg=(head[:, None] == head[None, :]).astype(BF16),
        lng=sgu_ln_g.reshape(1, SGU_WIDTH),
        lnb=sgu_ln_b.reshape(1, SGU_WIDTH),
        sink=attn_sink.astype(F32),
        ws=w_spatial.astype(BF16),
        bs=jnp.repeat(b_spatial.T, SGU_GROUP_DIM, axis=1),
        ag=attn_out_g.reshape(1, ATTN_WIDTH),
        sg=sgu_out_g.reshape(1, SGU_WIDTH),
        wo=w_out.astype(BF16),
    )


def _pack_bf16_pair(lo, hi):
    lo_b = lax.bitcast_convert_type(lo.astype(BF16).astype(F32), jnp.uint32) >> 16
    hi_b = lax.bitcast_convert_type(hi.astype(BF16).astype(F32), jnp.uint32) & jnp.uint32(0xFFFF0000)
    return hi_b | lo_b


def _unpack_bf16_pair(packed):
    lo = lax.bitcast_convert_type(packed << 16, F32).astype(BF16)
    hi = lax.bitcast_convert_type(packed & jnp.uint32(0xFFFF0000), F32).astype(BF16)
    return lo, hi


def _router_kernel(x_ref, g2_ref, wh_ref, wl_ref, br_ref, tri_ref, xn_ref, idx_ref, rank_ref, gate_ref, cnt_ref,
                   run_ref):
    @pl.when(pl.program_id(0) == 0)
    def _():
        run_ref[...] = jnp.zeros_like(run_ref)

    x = x_ref[...]
    xn = x * lax.rsqrt(jnp.mean(x * x, axis=-1, keepdims=True) + EPS) * g2_ref[...]
    xn_ref[...] = lax.bitcast_convert_type(_pack_bf16_pair(xn[:, :D_MODEL // 2], xn[:, D_MODEL // 2:]), jnp.int32)

    xh = xn.astype(BF16)
    xl = (xn - xh.astype(F32)).astype(BF16)
    nt = (((1,), (1,)), ((), ()))
    logits = (lax.dot_general(wh_ref[...], xh, nt, preferred_element_type=F32)
              + lax.dot_general(wh_ref[...], xl, nt, preferred_element_type=F32)
              + lax.dot_general(wl_ref[...], xh, nt, preferred_element_type=F32)) + br_ref[...]
    rows = logits.shape[1]
    erow = lax.broadcasted_iota(jnp.int32, (N_EXPERTS, rows), 0)
    work = logits
    vals, sels = [], []
    for k in range(TOP_K):
        m = jnp.max(work, axis=0, keepdims=True)
        ik = jnp.min(jnp.where(work == m, erow, N_EXPERTS), axis=0, keepdims=True)
        sel = erow == ik
        idx_ref[k:k + 1, :] = ik
        vals.append(m)
        sels.append(sel)
        work = jnp.where(sel, -jnp.inf, work)

    exps = [jnp.exp(v - vals[0]) for v in vals]
    den = exps[0] + exps[1] + exps[2] + exps[3]
    gate_ref[...] = jnp.zeros_like(gate_ref)
    for k in range(TOP_K):
        gate_ref[k:k + 1, :] = exps[k] / den

    onehot = jnp.zeros((N_EXPERTS, rows), F32)
    for sel in sels:
        onehot = onehot + sel.astype(F32)
    before = jnp.dot(onehot.astype(BF16), tri_ref[...], preferred_element_type=F32) + run_ref[:, :1]
    for k in range(TOP_K):
        rank_ref[k:k + 1, :] = jnp.sum(jnp.where(sels[k], before, 0.0), axis=0, keepdims=True).astype(jnp.int32)
    run_ref[...] = run_ref[...] + jnp.sum(onehot, axis=1, keepdims=True)
    cnt_ref[...] = run_ref[...]


def _router(x2d, g2, wr_hi, wr_lo, br, tri):
    t = x2d.shape[0]
    rows = ROUTER_ROWS
    const = lambda i: (0, 0)
    return pl.pallas_call(
        _router_kernel,
        grid=(t // rows,),
        in_specs=[
            pl.BlockSpec((rows, D_MODEL), lambda i: (i, 0)),
            pl.BlockSpec((1, D_MODEL), const),
            pl.BlockSpec((N_EXPERTS, D_MODEL), const),
            pl.BlockSpec((N_EXPERTS, D_MODEL), const),
            pl.BlockSpec((N_EXPERTS, 1), const),
            pl.BlockSpec((rows, rows), const),
        ],
        out_specs=[
            pl.BlockSpec((rows, D_MODEL // 2), lambda i: (i, 0)),
            pl.BlockSpec((TOP_K, rows), lambda i: (0, i)),
            pl.BlockSpec((TOP_K, rows), lambda i: (0, i)),
            pl.BlockSpec((2 * TOP_K, rows), lambda i: (0, i)),
            pl.BlockSpec((N_EXPERTS, LANES), const),
        ],
        out_shape=[
            jax.ShapeDtypeStruct((t, D_MODEL // 2), jnp.int32),
            jax.ShapeDtypeStruct((TOP_K, t), jnp.int32),
            jax.ShapeDtypeStruct((TOP_K, t), jnp.int32),
            jax.ShapeDtypeStruct((2 * TOP_K, t), F32),
            jax.ShapeDtypeStruct((N_EXPERTS, LANES), F32),
        ],
        scratch_shapes=[pltpu.VMEM((N_EXPERTS, LANES), F32)],
        compiler_params=_cparams("arbitrary"),
        name="router",
    )(x2d, g2, wr_hi, wr_lo, br, tri)


def _dest_kernel(pstart_ref, idx_ref, rank_ref, dest_ref):
    idx = idx_ref[...]
    dest = rank_ref[...]
    for e in range(N_EXPERTS):
        dest = dest + jnp.where(idx == e, pstart_ref[e], 0)
    dest_ref[...] = dest


def _dest(pstart, idx, rank):
    t = idx.shape[1]
    rows = min(DEST_ROWS, t)
    blk =pl.BlockSpec((TOP_K, rows), lambda i, s: (0, i))
    grid_spec = pltpu.PrefetchScalarGridSpec(num_scalar_prefetch=1, grid=(t // rows,), in_specs=[blk, blk],
                                             out_specs=blk)
    return pl.pallas_call(
        _dest_kernel,
        grid_spec=grid_spec,
        out_shape=jax.ShapeDtypeStruct((TOP_K, t), jnp.int32),
        compiler_params=_cparams("parallel"),
        name="dest",
    )(pstart, idx, rank)


SC_CORES = 2
SC_SUBCORES = 16
SC_WORKERS = SC_CORES * SC_SUBCORES
SC_WINDOW = 128


def _sc_mesh():
    return plsc.VectorSubcoreMesh(core_axis_name="c", subcore_axis_name="s", num_cores=SC_CORES,
                                  num_subcores=SC_SUBCORES)


def _sc_worker():
    return lax.axis_index("s") * SC_CORES + lax.axis_index("c")


def _sc_scatter(rows, idx, cap):
    t, width = rows.shape
    n_idx = idx.shape[0]
    per_worker = t // SC_WORKERS
    assert per_worker * SC_WORKERS == t and per_worker % SC_WINDOW == 0
    idx_flat = idx.reshape(n_idx * t)

    @functools.partial(
        pl.kernel,
        mesh=_sc_mesh(),
        out_type=jax.ShapeDtypeStruct((cap, width), rows.dtype),
        scratch_types=[
            pltpu.VMEM((SC_WINDOW,), jnp.int32),
            pltpu.VMEM((SC_WINDOW, width), rows.dtype),
            pltpu.SemaphoreType.DMA,
        ],
        name="sc_scatter",
    )
    def scatter(rows_hbm, idx_hbm, out_hbm, idx_v, rows_v, sem):
        base = _sc_worker() * per_worker

        @pl.loop(0, per_worker // SC_WINDOW)
        def _(step):
            off = pl.multiple_of(base + step * SC_WINDOW, SC_WINDOW)
            pltpu.sync_copy(rows_hbm.at[pl.ds(off, SC_WINDOW)], rows_v)
            for k in range(n_idx):
                pltpu.sync_copy(idx_hbm.at[pl.ds(pl.multiple_of(k * t + off, SC_WINDOW), SC_WINDOW)], idx_v)
                pltpu.async_copy(rows_v, out_hbm.at[idx_v], sem).wait()

    return scatter(rows, idx_flat)


def _experts_kernel(blk_e_ref, blk_src_ref, blk_valid_ref, blk_first_ref, blk_next_ref, blk_slot_ref,
                    x_ref, w1_hbm, b1_ref, w2_hbm, b2_ref, o_ref, w1_buf, w2_buf, sems):
    del blk_src_ref
    b = pl.program_id(0)
    valid = blk_valid_ref[b]
    slot = blk_slot_ref[b]
    half = D_MODEL // 2

    def weight_copies(expert, s):
        return (pltpu.make_async_copy(w1_hbm.at[expert], w1_buf.at[s], sems.at[0, s]),
                pltpu.make_async_copy(w2_hbm.at[expert], w2_buf.at[s], sems.at[1, s]))

    @pl.when(b == 0)
    def _():
        for copy in weight_copies(blk_e_ref[0], slot):
            copy.start()

    @pl.when(blk_first_ref[b] == 1)
    def _():
        for copy in weight_copies(blk_e_ref[b], slot):
            copy.wait()

        @pl.when(blk_next_ref[b] >= 0)
        def _():
            for copy in weight_copies(blk_next_ref[b], 1 - slot):
                copy.start()

    w1_ref = w1_buf.at[slot]
    w2_ref = w2_buf.at[slot]

    def run(rows):
        row = lax.broadcasted_iota(jnp.int32, (rows, half), 0)
        x = jnp.where(row < valid, x_ref[:rows, :], 0)
        lo, hi = _unpack_bf16_pair(lax.bitcast_convert_type(x, jnp.uint32))
        h = (jnp.dot(lo, w1_ref[:half, :].astype(BF16), preferred_element_type=F32)
             + jnp.dot(hi, w1_ref[half:, :].astype(BF16), preferred_element_type=F32) + b1_ref[...])
        gate = jnp.minimum(h[:, :D_FF], SWIGLU_LIMIT)
        up = jnp.clip(h[:, D_FF:], -SWIGLU_LIMIT, SWIGLU_LIMIT)
        act = (up + 1.0) * (gate / (1.0 + jnp.exp2(gate * (-SWIGLU_ALPHA * LOG2_E))))
        o = jnp.dot(act.astype(BF16), w2_ref[...].astype(BF16), preferred_element_type=F32) + b2_ref[...]
        o_ref[:rows, :] = lax.bitcast_convert_type(_pack_bf16_pair(o[:, :half], o[:, half:]), jnp.int32)
        if rows < EXPERT_ROWS:
            o_ref[rows:, :] = jnp.zeros((EXPERT_ROWS - rows, half), jnp.int32)

    lower = 0
    for rows in EXPERT_ROW_STEPS:
        pl.when((valid > lower) & (valid <= rows))(functools.partial(run, rows))
        lower = rows

    @pl.when(valid == 0)
    def _():
        o_ref[...] = jnp.zeros_like(o_ref)


def _experts(blk_e, blk_src, blk_valid, blk_first, blk_next, blk_slot, xs, w1, b1, w2, b2):
    cap = xs.shape[0]
    rows = EXPERT_ROWS
    grid_spec = pltpu.PrefetchScalarGridSpec(
        num_scalar_prefetch=6,
        grid=(cap // rows,),
        in_specs=[
            pl.BlockSpec((rows, D_MODEL // 2), lambda b, be, bs, *_: (bs[b], 0)),
            pl.BlockSpec(memory_space=pl.ANY),
            pl.BlockSpec((None, 1, 2 * D_FF), lambda b, be, *_: (be[b], 0, 0)),
            pl.BlockSpec(memory_space=pl.ANY),
            pl.BlockSpec((None, 1, D_MODEL), lambda b, be, *_: (be[b], 0, 0)),
        ],
        out_specs=pl.BlockSpec((rows, D_MODEL // 2), lambda b, *_: (b, 0)),
        scratch_shapes=[
            pltpu.VMEM((2, D_MODEL, 2 * D_FF), F32),
            pltpu.VMEM((2, D_FF, D_MODEL), F32),
            pltpu.SemaphoreType.DMA((2, 2)),
        ],
    )
    return pl.pallas_call(
        _experts_kernel,
        grid_spec=grid_spec,
        out_shape=jax.ShapeDtypeStruct((cap, D_MODEL // 2), jnp.int32),
        compiler_params=_cparams("arbitrary"),
        name="experts",
    )(blk_e, blk_src, blk_valid, blk_first, blk_next, blk_slot, xs, w1, b1, w2, b2)


def _sc_gather(table, idx):
    n = idx.shape[0]
    width = table.shape[1]
    per_worker = n // SC_WORKERS
    assert per_worker * SC_WORKERS == n and per_worker % SC_WINDOW == 0

    @functools.partial(
        pl.kernel,
        mesh=_sc_mesh(),
        out_type=jax.ShapeDtypeStruct((n, width), table.dtype),
        scratch_types=[
            pltpu.VMEM((SC_WINDOW,), jnp.int32),
            pltpu.VMEM((SC_WINDOW, width), table.dtype),
            pltpu.SemaphoreType.DMA,
        ],
        name="sc_gather",
    )
    def gather(table_hbm, idx_hbm, out_hbm, idx_v, rows_v, sem):
        base = _sc_worker() * per_worker

        @pl.loop(0, per_worker // SC_WINDOW)
        def _(step):
            off = pl.multiple_of(base + step * SC_WINDOW, SC_WINDOW)
            pltpu.sync_copy(idx_hbm.at[pl.ds(off, SC_WINDOW)], idx_v)
            pltpu.async_copy(table_hbm.at[idx_v], rows_v, sem).wait()
            pltpu.sync_copy(rows_v, out_hbm.at[pl.ds(off, SC_WINDOW)])

    return gather(table, idx)


def _combine_kernel(gate_ref, x_ref, rows_ref, y_ref):
    gate_t = gate_ref[...].T
    half = D_MODEL // 2
    lo_sum = x_ref[:, :half]
    hi_sum = x_ref[:, half:]
    for k in range(TOP_K):
        packed = lax.bitcast_convert_type(rows_ref[k], jnp.uint32)
        g = gate_t[:, k:k + 1]
        lo_sum = lo_sum + g * lax.bitcast_convert_type(packed << 16, F32)
        hi_sum = hi_sum + g * lax.bitcast_convert_type(packed & jnp.uint32(0xFFFF0000), F32)
    y_ref[:, :half] = lo_sum
    y_ref[:, half:] = hi_sum


def _combine(gate, x2d, rows4):
    t = x2d.shape[0]
    rows = COMBINE_ROWS
    return pl.pallas_call(
        _combine_kernel,
        grid=(t // rows,),
        in_specs=[
            pl.BlockSpec((2 * TOP_K, rows), lambda i: (0, i)),
            pl.BlockSpec((rows, D_MODEL), lambda i: (i, 0)),
            pl.BlockSpec((TOP_K, rows, D_MODEL // 2), lambda i: (0, i, 0)),
        ],
        out_specs=pl.BlockSpec((rows, D_MODEL), lambda i: (i, 0)),
        out_shape=jax.ShapeDtypeStruct((t, D_MODEL), F32),
        compiler_params=_cparams("parallel"),
        name="combine",
    )(gate, x2d, rows4)


def _moe_half(x2d, m):
    t = x2d.shape[0]
    rows = EXPERT_ROWS
    cap = t * TOP_K + N_EXPERTS * rows
    n_blk = cap // rows
    xn, idx, rank, gate, cnt = _router(x2d, m["g2"], m["wr_hi"], m["wr_lo"], m["br"], m["tri"])

    counts = cnt[:, 0].astype(jnp.int32)
    padded = (counts + rows - 1) // rows * rows
    pends = jnp.cumsum(padded)
    pstart = pends - padded
    n_used = pends[-1:] // rows
    blk_src = jnp.minimum(jnp.arange(n_blk, dtype=jnp.int32), n_used - 1)
    starts = (blk_src * rows)[:, None]
    owner = (pstart[None, :] <= starts) & (starts < pends[None, :])
    blk_e = jnp.sum(jnp.where(owner, jnp.arange(N_EXPERTS)[None, :], 0), axis=1).astype(jnp.int32)
    filled_to = jnp.sum(jnp.where(owner, (pstart + counts)[None, :], 0), axis=1)
    blk_valid = jnp.clip(filled_to - blk_src * rows, 0, rows)
    in_use = jnp.arange(n_blk) < n_used
    blk_valid = jnp.where(in_use, blk_valid, 0).astype(jnp.int32)
    expert = jnp.arange(N_EXPERTS)
    has_rows = counts > 0
    later = jnp.where(has_rows[None, :] & (expert[None, :] > expert[:, None]), expert[None, :], N_EXPERTS)
    next_expert = jnp.min(later, axis=1)
    next_expert = jnp.where(next_expert == N_EXPERTS, -1, next_expert)
    expert_slot = (jnp.cumsum(has_rows) - 1) % 2
    blk_first = (in_use & (jnp.sum(jnp.where(owner, pstart[None, :], 0), axis=1) == blk_src * rows)).astype(jnp.int32)
    blk_next = jnp.sum(jnp.where(owner, next_expert[None, :], 0), axis=1).astype(jnp.int32)
    blk_slot = jnp.sum(jnp.where(owner, expert_slot[None, :], 0), axis=1).astype(jnp.int32)

    dest = _dest(pstart.astype(jnp.int32), idx, rank)
    xs = _sc_scatter(xn, dest, cap)
    out_sorted = _experts(blk_e, blk_src, blk_valid, blk_first, blk_next, blk_slot, xs, m["w1"], m["b1"], m["w2"],
                          m["b2"])
    rows4 = _sc_gather(out_sorted, dest.reshape(TOP_K * t))
    return _combine(gate, x2d, rows4.reshape(TOP_K, t, D_MODEL // 2))


def _prep_moe(norm2_g, w_router, b_router, w_moe_in, b_moe_in, w_moe_out, b_moe_out):
    r = jnp.arange(ROUTER_ROWS)
    wr_hi = w_router.T.astype(BF16)
    return dict(
        g2=norm2_g.reshape(1, D_MODEL),
        wr_hi=wr_hi,
        wr_lo=(w_router.T - wr_hi.astype(F32)).astype(BF16),
        br=b_router.reshape(N_EXPERTS, 1),
        tri=(r[:, None] < r[None, :]).astype(BF16),
        w1=w_moe_in,
        b1=b_moe_in.reshape(N_EXPERTS, 1, 2 * D_FF),
        w2=w_moe_out,
        b2=b_moe_out.reshape(N_EXPERTS, 1, D_MODEL),
    )


def kernel(x_prompt, x_sample, norm1_g, w_in, q_norm_g, k_norm_g, attn_sink, sgu_ln_g, sgu_ln_b, w_spatial,
           b_spatial, attn_out_g, sgu_out_g, w_out, norm2_g, w_router, b_router, w_moe_in, b_moe_in, w_moe_out,
           b_moe_out):
    p = _prep_params(norm1_g[0], w_in[0], q_norm_g[0], k_norm_g[0], attn_sink[0], sgu_ln_g[0], sgu_ln_b[0],
                     w_spatial[0], b_spatial[0], attn_out_g[0], sgu_out_g[0], w_out[0])
    m = _prep_moe(norm2_g[0], w_router[0], b_router[0], w_moe_in[0], b_moe_in[0], w_moe_out[0], b_moe_out[0])
    outs = []
    for x in (x_prompt, x_sample):
        x2 = _mix_half(x, p)
        outs.append(_moe_half(x2, m).reshape(x.shape))
    return tuple(outs)
```

```python
import functools
import math

import jax
import jax.numpy as jnp
from jax import lax
from jax.experimental import pallas as pl
from jax.experimental.pallas import tpu as pltpu
from jax.experimental.pallas import tpu_sc as plsc

D_MODEL = 1024
HEAD_DIM = 64
N_Q_HEADS = 8
N_KV_HEADS = 2
Q_PER_KV = N_Q_HEADS // N_KV_HEADS
ATTN_WIDTH = N_Q_HEADS * HEAD_DIM
KV_WIDTH = N_KV_HEADS * HEAD_DIM
QK_WIDTH = ATTN_WIDTH + KV_WIDTH
KV_DUP_WIDTH = 2 * KV_WIDTH
N_SGU_GROUPS = 8
SGU_GROUP_DIM = 64
SGU_WIDTH = N_SGU_GROUPS * SGU_GROUP_DIM
IN_PROJ_WIDTH = ATTN_WIDTH + 2 * KV_WIDTH + 2 * SGU_WIDTH
BLOCK = 128
ROPE_THETA = 500000.0
ROPE_DIM = HEAD_DIM // 4
N_EXPERTS = 32
TOP_K = 4
D_FF = D_MODEL
SWIGLU_LIMIT = 7.0
SWIGLU_ALPHA = 1.702
EPS = 1e-6
LOG2_E = 1.4426950408889634

LANES = 128
IN_PROJ_ROWS = 1024
IN_PROJ_CHUNK = 256
MIXER_ROWS = 1024
ROUTER_ROWS = 1024
DEST_ROWS = 8192
COMBINE_ROWS = 1024
EXPERT_ROWS = 1024
EXPERT_ROW_STEPS = (256, 512, 768, 1024)
VMEM_LIMIT_BYTES = 56 * 1024 * 1024

F32 = jnp.float32
BF16 = jnp.bfloat16


def _cparams(*semantics):
    return pltpu.CompilerParams(dimension_semantics=semantics, vmem_limit_bytes=VMEM_LIMIT_BYTES)


def _dup_heads(tile):
    low = lax.broadcasted_iota(jnp.int32, tile.shape, 1) < HEAD_DIM
    swapped = pltpu.roll(tile, HEAD_DIM, axis=1)
    return jnp.where(low, tile, swapped), jnp.where(low, swapped, tile)


def _gelu_tanh(x):
    k = 2.0 * math.sqrt(2.0 / math.pi) * LOG2_E
    return x / (1.0 + jnp.exp2(x * (-k - (k * 0.044715) * (x * x))))


def _in_proj_kernel(x_ref, g1_ref, w_ref, qkg_ref, cos_ref, sina_ref, sinb_ref, seg_ref, lng_ref, lnb_ref,
                    q_ref, k_ref, v_ref, u_ref, vn_ref):
    for r0 in range(0, IN_PROJ_ROWS, IN_PROJ_CHUNK):
        rs = slice(r0, r0 + IN_PROJ_CHUNK)
        x = x_ref[rs, :]
        h = x * lax.rsqrt(jnp.mean(x * x, axis=-1, keepdims=True) + EPS) * g1_ref[...]
        z = jnp.dot(h.astype(BF16), w_ref[...], preferred_element_type=F32)

        qk = z[:, :QK_WIDTH]
        ss = jnp.dot((qk * qk).astype(BF16), seg_ref[...], preferred_element_type=F32)
        qkn = qk * lax.rsqrt(ss * (1.0 / HEAD_DIM) + EPS) * qkg_ref[...]
        cos, sina, sinb = cos_ref[rs, :], sina_ref[rs, :], sinb_ref[rs, :]
        for c in range(QK_WIDTH // LANES):
            xc = qkn[:, c * LANES:(c + 1) * LANES]
            up = pltpu.roll(xc, LANES - ROPE_DIM // 2, axis=1)
            dn = pltpu.roll(xc, ROPE_DIM // 2, axis=1)
            rc = xc * cos + up * sina + dn * sinb
            if c < ATTN_WIDTH // LANES:
                q_ref[rs, c * LANES:(c + 1) * LANES] = (rc * (HEAD_DIM ** -0.5 * LOG2_E)).astype(BF16)
            else:
                k0, k1 = _dup_heads(rc)
                k_ref[rs, :LANES] = k0.astype(BF16)
                k_ref[rs, LANES:] = k1.astype(BF16)

        v0, v1 = _dup_heads(z[:, QK_WIDTH:QK_WIDTH + KV_WIDTH])
        v_ref[rs, :LANES] = v0.astype(BF16)
        v_ref[rs, LANES:] = v1.astype(BF16)
        su = z[:, QK_WIDTH + KV_WIDTH:QK_WIDTH + KV_WIDTH + SGU_WIDTH]
        sv = z[:, QK_WIDTH + KV_WIDTH + SGU_WIDTH:]
        u_ref[rs, :] = _gelu_tanh(su).astype(BF16)
        gv = _gelu_tanh(sv)
        mu = jnp.mean(gv, axis=-1, keepdims=True)
        gc = gv - mu
        ln = gc * lax.rsqrt(jnp.mean(gc * gc, axis=-1, keepdims=True) + EPS) * lng_ref[...] + lnb_ref[...]
        vn_ref[rs, :] = ln.astype(BF16)


def _rope_tables(seq):
    half = ROPE_DIM // 2
    inv_freq = ROPE_THETA ** (-(jnp.arange(half, dtype=F32) * 2.0) / ROPE_DIM)
    ang = jnp.arange(seq).astype(F32)[:, None] * inv_freq[None, :]
    cos, sin = jnp.cos(ang), jnp.sin(ang)
    j = jnp.arange(LANES) % HEAD_DIM
    f = j % half
    cos_t = jnp.where(j[None, :] < ROPE_DIM, cos[:, f], 1.0)
    sina_t = jnp.where(j[None, :] < half, -sin[:, f], 0.0)
    sinb_t = jnp.where((j[None, :] >= half) & (j[None, :] < ROPE_DIM), sin[:, f], 0.0)
    return cos_t.astype(F32), sina_t.astype(F32), sinb_t.astype(F32)


def _in_proj(x2d, seq, g1, w_in, qkg, tables, seg, lng, lnb):
    t = x2d.shape[0]
    rows = IN_PROJ_ROWS
    n_seq = seq // rows
    const = lambda i: (0, 0)
    tab = pl.BlockSpec((rows, LANES), lambda i: (i % n_seq, 0))
    return pl.pallas_call(
        _in_proj_kernel,
        grid=(t // rows,),
        in_specs=[
            pl.BlockSpec((rows, D_MODEL), lambda i: (i, 0)),
            pl.BlockSpec((1, D_MODEL), const),
            pl.BlockSpec((D_MODEL, IN_PROJ_WIDTH), const),
            pl.BlockSpec((1, QK_WIDTH), const),
            tab, tab, tab,
            pl.BlockSpec((QK_WIDTH, QK_WIDTH), const),
            pl.BlockSpec((1, SGU_WIDTH), const),
            pl.BlockSpec((1, SGU_WIDTH), const),
        ],
        out_specs=[
            pl.BlockSpec((rows, ATTN_WIDTH), lambda i: (i, 0)),
            pl.BlockSpec((rows, KV_DUP_WIDTH), lambda i: (i, 0)),
            pl.BlockSpec((rows, KV_DUP_WIDTH), lambda i: (i, 0)),
            pl.BlockSpec((rows, SGU_WIDTH), lambda i: (i, 0)),
            pl.BlockSpec((rows, SGU_WIDTH), lambda i: (i, 0)),
        ],
        out_shape=[
            jax.ShapeDtypeStruct((t, ATTN_WIDTH), BF16),
            jax.ShapeDtypeStruct((t, KV_DUP_WIDTH), BF16),
            jax.ShapeDtypeStruct((t, KV_DUP_WIDTH), BF16),
            jax.ShapeDtypeStruct((t, SGU_WIDTH), BF16),
            jax.ShapeDtypeStruct((t, SGU_WIDTH), BF16),
        ],
        compiler_params=_cparams("parallel"),
        name="in_proj",
    )(x2d, g1, w_in, qkg, *tables, seg, lng, lnb)


def _mixer_kernel(sink_ref, q_ref, kp_ref, kc_ref, kn_ref, vp_ref, vc_ref, vx_ref, u_ref, g_ref, x_ref,
                  ws_ref, bs_ref, ag_ref, sg_ref, wo_ref, o_ref, mix_ref):
    i = pl.program_id(1)
    n_i = pl.num_programs(1)
    n_sub = MIXER_ROWS // BLOCK
    kwin = jnp.concatenate([kp_ref[...], kc_ref[...], kn_ref[...]], axis=0)
    vwin = jnp.concatenate([vp_ref[...], vc_ref[...], vx_ref[...]], axis=0)

    srows = N_Q_HEADS * BLOCK
    r = lax.broadcasted_iota(jnp.int32, (srows, 3 * BLOCK), 0) & (BLOCK - 1)
    c = lax.broadcasted_iota(jnp.int32, (srows, 3 * BLOCK), 1)
    band = (c >= r) & (c <= r + 2 * BLOCK)
    hrow = lax.broadcasted_iota(jnp.int32, (srows, 1), 0) // BLOCK
    low = lax.broadcasted_iota(jnp.int32, (BLOCK, LANES), 1) < HEAD_DIM
    keep = (low.astype(BF16), (~low).astype(BF16))
    lowrow, highrow = keep[0][:1, :], keep[1][:1, :]
    ones = jnp.ones((3 * BLOCK, LANES), BF16)

    for j in range(n_sub):
        valid = band
        if j == 0:
            valid = valid & ((c >= BLOCK) | (i > 0))
        if j == n_sub - 1:
            valid = valid & ((c < 2 * BLOCK) | (i < n_i - 1))
        kj = kwin[j * BLOCK:(j + 3) * BLOCK, :]
        vj = vwin[j * BLOCK:(j + 3) * BLOCK, :]
        k01 = kj[:, :LANES] * lowrow + kj[:, LANES:] * highrow
        v01 = vj[:, :LANES] * lowrow + vj[:, LANES:] * highrow
        qs = jnp.concatenate(
            [q_ref[j * BLOCK:(j + 1) * BLOCK, (h % Q_PER_KV) * LANES:(h % Q_PER_KV + 1) * LANES] * keep[h // Q_PER_KV]
             for h in range(N_Q_HEADS)], axis=0)
        s = lax.dot_general(qs, k01, (((1,), (1,)), ((), ())), preferred_element_type=F32)
        s = jnp.where(valid, s, -jnp.inf)
        sink = jnp.zeros((srows, 1), F32)
        for h in range(N_Q_HEADS):
            sink = jnp.where(hrow == h, sink_ref[h] * LOG2_E, sink)
        m = jnp.maximum(jnp.max(s, axis=-1, keepdims=True), sink)
        p = jnp.exp2(s - m).astype(BF16)
        ov = jnp.dot(p, jnp.concatenate([v01, ones], axis=-1), preferred_element_type=F32)
        o = ov[:, :LANES] / (ov[:, LANES:] + jnp.exp2(sink - m))
        a_tiles = [jnp.where(low, o[t * BLOCK:(t + 1) * BLOCK, :], o[(t + Q_PER_KV) * BLOCK:(t + Q_PER_KV + 1) * BLOCK, :])
                   for t in range(Q_PER_KV)]
        a = jnp.concatenate(a_tiles, axis=-1)
        a = a * lax.rsqrt(jnp.mean(a * a, axis=-1, keepdims=True) + EPS) * ag_ref[...]

        mixed_tiles = []
        for t in range(SGU_WIDTH // LANES):
            vt = g_ref[j * BLOCK:(j + 1) * BLOCK, t * LANES:(t + 1) * LANES]
            mixed_tiles.append(jnp.where(low, jnp.dot(ws_ref[2 * t], vt, preferred_element_type=F32),
                                         jnp.dot(ws_ref[2 * t + 1], vt, preferred_element_type=F32)))
        mixed = jnp.concatenate(mixed_tiles, axis=-1) + bs_ref[...]
        gated = u_ref[j * BLOCK:(j + 1) * BLOCK, :].astype(F32) * mixed
        gated = gated * lax.rsqrt(jnp.mean(gated * gated, axis=-1, keepdims=True) + EPS) * sg_ref[...]
        mix_ref[j * BLOCK:(j + 1) * BLOCK, :] = jnp.concatenate([a, gated], axis=-1).astype(BF16)

    o_ref[...] = x_ref[...] + jnp.dot(mix_ref[...], wo_ref[...], preferred_element_type=F32)


def _mixer(batch, seq, sink, q, k, v, u, vn, x2d, ws, bs, ag, sg, wo):
    rows = MIXER_ROWS
    n_i = seq // rows
    sub = rows // BLOCK
    n_blk = batch * seq // BLOCK
    const2 = lambda b, i, s: (0, 0)
    cur = lambda b, i, s: (b * n_i + i, 0)
    prv = lambda b, i, s: (jnp.maximum((b * n_i + i) * sub - 1, 0), 0)
    nxt = lambda b, i, s: (jnp.minimum((b * n_i + i + 1) * sub, n_blk - 1), 0)
    grid_spec = pltpu.PrefetchScalarGridSpec(
        num_scalar_prefetch=1,
        grid=(batch, n_i),
        in_specs=[
            pl.BlockSpec((rows, ATTN_WIDTH), cur),
            pl.BlockSpec((BLOCK, KV_DUP_WIDTH), prv),
            pl.BlockSpec((rows, KV_DUP_WIDTH), cur),
            pl.BlockSpec((BLOCK, KV_DUP_WIDTH), nxt),
            pl.BlockSpec((BLOCK, KV_DUP_WIDTH), prv),
            pl.BlockSpec((rows, KV_DUP_WIDTH), cur),
            pl.BlockSpec((BLOCK, KV_DUP_WIDTH), nxt),
            pl.BlockSpec((rows, SGU_WIDTH), cur),
            pl.BlockSpec((rows, SGU_WIDTH), cur),
            pl.BlockSpec((rows, D_MODEL), cur),
            pl.BlockSpec((N_SGU_GROUPS, BLOCK, BLOCK), lambda b, i, s: (0, 0, 0)),
            pl.BlockSpec((BLOCK, SGU_WIDTH), const2),
            pl.BlockSpec((1, ATTN_WIDTH), const2),
            pl.BlockSpec((1, SGU_WIDTH), const2),
            pl.BlockSpec((D_MODEL, D_MODEL), const2),
        ],
        out_specs=pl.BlockSpec((rows, D_MODEL), cur),
        scratch_shapes=[pltpu.VMEM((rows, D_MODEL), BF16)],
    )
    return pl.pallas_call(
        _mixer_kernel,
        grid_spec=grid_spec,
        out_shape=jax.ShapeDtypeStruct((batch * seq, D_MODEL), F32),
        compiler_params=_cparams("parallel", "parallel"),
        name="mixer",
    )(sink, q, k, k, k, v, v, v, u, vn, x2d, ws, bs, ag, sg, wo)


def _mix_half(x, p):
    batch, seq, _ = x.shape
    x2d = x.reshape(batch * seq, D_MODEL)
    q, k, v, u, vn = _in_proj(x2d, seq, p["g1"], p["w_in"], p["qkg"], _rope_tables(seq), p["seg"], p["lng"],
                              p["lnb"])
    return _mixer(batch, seq, p["sink"], q, k, v, u, vn, x2d, p["ws"], p["bs"], p["ag"], p["sg"], p["wo"])


def _prep_params(norm1_g, w_in, q_norm_g, k_norm_g, attn_sink, sgu_ln_g, sgu_ln_b, w_spatial, b_spatial,
                 attn_out_g, sgu_out_g, w_out):
    head = jnp.arange(QK_WIDTH) // HEAD_DIM
    def pair_heads(a, axis):
        shape = a.shape
        a = a.reshape(shape[:axis] + (N_KV_HEADS, Q_PER_KV, HEAD_DIM) + shape[axis + 1:])
        return jnp.swapaxes(a, axis, axis + 1).reshape(shape)

    w_in = jnp.concatenate([pair_heads(w_in[:, :ATTN_WIDTH], 1), w_in[:, ATTN_WIDTH:]], axis=1)
    attn_out_g = pair_heads(attn_out_g, 0)
    w_out = jnp.concatenate([pair_heads(w_out[:ATTN_WIDTH], 0), w_out[ATTN_WIDTH:]], axis=0)
    return dict(
        g1=norm1_g.reshape(1, D_MODEL),
        w_in=w_in.astype(BF16),
        qkg=jnp.concatenate([jnp.tile(q_norm_g, N_Q_HEADS), jnp.tile(k_norm_g, N_KV_HEADS)]).reshape(1, QK_WIDTH),
        seg=(head[:, None] == head[None, :]).astype(BF16),
        lng=sgu_ln_g.reshape(1, SGU_WIDTH),
        lnb=sgu_ln_b.reshape(1, SGU_WIDTH),
        sink=attn_sink.astype(F32),
        ws=w_spatial.astype(BF16),
        bs=jnp.repeat(b_spatial.T, SGU_GROUP_DIM, axis=1),
        ag=attn_out_g.reshape(1, ATTN_WIDTH),
        sg=sgu_out_g.reshape(1, SGU_WIDTH),
        wo=w_out.astype(BF16),
    )


def _pack_bf16_pair(lo, hi):
    lo_b = lax.bitcast_convert_type(lo.astype(BF16).astype(F32), jnp.uint32) >> 16
    hi_b = lax.bitcast_convert_type(hi.astype(BF16).astype(F32), jnp.uint32) & jnp.uint32(0xFFFF0000)
    return hi_b | lo_b


def _unpack_bf16_pair(packed):
    lo = lax.bitcast_convert_type(packed << 16, F32).astype(BF16)
    hi = lax.bitcast_convert_type(packed & jnp.uint32(0xFFFF0000), F32).astype(BF16)
    return lo, hi


def _router_kernel(x_ref, g2_ref, wh_ref, wl_ref, br_ref, tri_ref, xn_ref, idx_ref, rank_ref, gate_ref, cnt_ref,
                   run_ref):
    @pl.when(pl.program_id(0) == 0)
    def _():
        run_ref[...] = jnp.zeros_like(run_ref)

    x = x_ref[...]
    xn = x * lax.rsqrt(jnp.mean(x * x, axis=-1, keepdims=True) + EPS) * g2_ref[...]
    xn_ref[...] = lax.bitcast_convert_type(_pack_bf16_pair(xn[:, :D_MODEL // 2], xn[:, D_MODEL // 2:]), jnp.int32)

    xh = xn.astype(BF16)
    xl = (xn - xh.astype(F32)).astype(BF16)
    nt = (((1,), (1,)), ((), ()))
    logits = (lax.dot_general(wh_ref[...], xh, nt, preferred_element_type=F32)
              + lax.dot_general(wh_ref[...], xl, nt, preferred_element_type=F32)
              + lax.dot_general(wl_ref[...], xh, nt, preferred_element_type=F32)) + br_ref[...]
    rows = logits.shape[1]
    erow = lax.broadcasted_iota(jnp.int32, (N_EXPERTS, rows), 0)
    work = logits
    vals, sels = [], []
    for k in range(TOP_K):
        m = jnp.max(work, axis=0, keepdims=True)
        ik = jnp.min(jnp.where(work == m, erow, N_EXPERTS), axis=0, keepdims=True)
        sel = erow == ik
        idx_ref[k:k + 1, :] = ik
        vals.append(m)
        sels.append(sel)
        work = jnp.where(sel, -jnp.inf, work)

    exps = [jnp.exp(v - vals[0]) for v in vals]
    den = exps[0] + exps[1] + exps[2] + exps[3]
    gate_ref[...] = jnp.zeros_like(gate_ref)
    for k in range(TOP_K):
        gate_ref[k:k + 1, :] = exps[k] / den

    onehot = jnp.zeros((N_EXPERTS, rows), F32)
    for sel in sels:
        onehot = onehot + sel.astype(F32)
    before = jnp.dot(onehot.astype(BF16), tri_ref[...], preferred_element_type=F32) + run_ref[:, :1]
    for k in range(TOP_K):
        rank_ref[k:k + 1, :] = jnp.sum(jnp.where(sels[k], before, 0.0), axis=0, keepdims=True).astype(jnp.int32)
    run_ref[...] = run_ref[...] + jnp.sum(onehot, axis=1, keepdims=True)
    cnt_ref[...] = run_ref[...]


def _router(x2d, g2, wr_hi, wr_lo, br, tri):
    t = x2d.shape[0]
    rows = ROUTER_ROWS
    const = lambda i: (0, 0)
    return pl.pallas_call(
        _router_kernel,
        grid=(t // rows,),
        in_specs=[
            pl.BlockSpec((rows, D_MODEL), lambda i: (i, 0)),
            pl.BlockSpec((1, D_MODEL), const),
            pl.BlockSpec((N_EXPERTS, D_MODEL), const),
            pl.BlockSpec((N_EXPERTS, D_MODEL), const),
            pl.BlockSpec((N_EXPERTS, 1), const),
            pl.BlockSpec((rows, rows), const),
        ],
        out_specs=[
            pl.BlockSpec((rows, D_MODEL // 2), lambda i: (i, 0)),
            pl.BlockSpec((TOP_K, rows), lambda i: (0, i)),
            pl.BlockSpec((TOP_K, rows), lambda i: (0, i)),
            pl.BlockSpec((2 * TOP_K, rows), lambda i: (0, i)),
            pl.BlockSpec((N_EXPERTS, LANES), const),
        ],
        out_shape=[
            jax.ShapeDtypeStruct((t, D_MODEL // 2), jnp.int32),
            jax.ShapeDtypeStruct((TOP_K, t), jnp.int32),
            jax.ShapeDtypeStruct((TOP_K, t), jnp.int32),
            jax.ShapeDtypeStruct((2 * TOP_K, t), F32),
            jax.ShapeDtypeStruct((N_EXPERTS, LANES), F32),
        ],
        scratch_shapes=[pltpu.VMEM((N_EXPERTS, LANES), F32)],
        compiler_params=_cparams("arbitrary"),
        name="router",
    )(x2d, g2, wr_hi, wr_lo, br, tri)


def _dest_kernel(pstart_ref, idx_ref, rank_ref, dest_ref):
    idx = idx_ref[...]
    dest = rank_ref[...]
    for e in range(N_EXPERTS):
        dest = dest + jnp.where(idx == e, pstart_ref[e], 0)
    dest_ref[...] = dest


def _dest(pstart, idx, rank):
    t = idx.shape[1]
    rows = min(DEST_ROWS, t)
    blk =pl.BlockSpec((TOP_K, rows), lambda i, s: (0, i))
    grid_spec = pltpu.PrefetchScalarGridSpec(num_scalar_prefetch=1, grid=(t // rows,), in_specs=[blk, blk],
                                             out_specs=blk)
    return pl.pallas_call(
        _dest_kernel,
        grid_spec=grid_spec,
        out_shape=jax.ShapeDtypeStruct((TOP_K, t), jnp.int32),
        compiler_params=_cparams("parallel"),
        name="dest",
    )(pstart, idx, rank)


SC_CORES = 2
SC_SUBCORES = 16
SC_WORKERS = SC_CORES * SC_SUBCORES
SC_WINDOW = 128


def _sc_mesh():
    return plsc.VectorSubcoreMesh(core_axis_name="c", subcore_axis_name="s", num_cores=SC_CORES,
                                  num_subcores=SC_SUBCORES)


def _sc_worker():
    return lax.axis_index("s") * SC_CORES + lax.axis_index("c")


def _sc_scatter(rows, idx, cap):
    t, width = rows.shape
    n_idx = idx.shape[0]
    per_worker = t // SC_WORKERS
    assert per_worker * SC_WORKERS == t and per_worker % SC_WINDOW == 0
    idx_flat = idx.reshape(n_idx * t)

    @functools.partial(
        pl.kernel,
        mesh=_sc_mesh(),
        out_type=jax.ShapeDtypeStruct((cap, width), rows.dtype),
        scratch_types=[
            pltpu.VMEM((SC_WINDOW,), jnp.int32),
            pltpu.VMEM((SC_WINDOW, width), rows.dtype),
            pltpu.SemaphoreType.DMA,
        ],
        name="sc_scatter",
    )
    def scatter(rows_hbm, idx_hbm, out_hbm, idx_v, rows_v, sem):
        base = _sc_worker() * per_worker

        @pl.loop(0, per_worker // SC_WINDOW)
        def _(step):
            off = pl.multiple_of(base + step * SC_WINDOW, SC_WINDOW)
            pltpu.sync_copy(rows_hbm.at[pl.ds(off, SC_WINDOW)], rows_v)
            for k in range(n_idx):
                pltpu.sync_copy(idx_hbm.at[pl.ds(pl.multiple_of(k * t + off, SC_WINDOW), SC_WINDOW)], idx_v)
                pltpu.async_copy(rows_v, out_hbm.at[idx_v], sem).wait()

    return scatter(rows, idx_flat)


def _experts_kernel(blk_e_ref, blk_src_ref, blk_valid_ref, blk_first_ref, blk_next_ref, blk_slot_ref,
                    x_ref, w1_hbm, b1_ref, w2_hbm, b2_ref, o_ref, w1_buf, w2_buf, sems):
    del blk_src_ref
    b = pl.program_id(0)
    valid = blk_valid_ref[b]
    slot = blk_slot_ref[b]
    half = D_MODEL // 2

    def weight_copies(expert, s):
        return (pltpu.make_async_copy(w1_hbm.at[expert], w1_buf.at[s], sems.at[0, s]),
                pltpu.make_async_copy(w2_hbm.at[expert], w2_buf.at[s], sems.at[1, s]))

    @pl.when(b == 0)
    def _():
        for copy in weight_copies(blk_e_ref[0], slot):
            copy.start()

    @pl.when(blk_first_ref[b] == 1)
    def _():
        for copy in weight_copies(blk_e_ref[b], slot):
            copy.wait()

        @pl.when(blk_next_ref[b] >= 0)
        def _():
            for copy in weight_copies(blk_next_ref[b], 1 - slot):
                copy.start()

    w1_ref = w1_buf.at[slot]
    w2_ref = w2_buf.at[slot]

    def run(rows):
        row = lax.broadcasted_iota(jnp.int32, (rows, half), 0)
        x = jnp.where(row < valid, x_ref[:rows, :], 0)
        lo, hi = _unpack_bf16_pair(lax.bitcast_convert_type(x, jnp.uint32))
        h = (jnp.dot(lo, w1_ref[:half, :].astype(BF16), preferred_element_type=F32)
             + jnp.dot(hi, w1_ref[half:, :].astype(BF16), preferred_element_type=F32) + b1_ref[...])
        gate = jnp.minimum(h[:, :D_FF], SWIGLU_LIMIT)
        up = jnp.clip(h[:, D_FF:], -SWIGLU_LIMIT, SWIGLU_LIMIT)
        act = (up + 1.0) * (gate / (1.0 + jnp.exp2(gate * (-SWIGLU_ALPHA * LOG2_E))))
        o = jnp.dot(act.astype(BF16), w2_ref[...].astype(BF16), preferred_element_type=F32) + b2_ref[...]
        o_ref[:rows, :] = lax.bitcast_convert_type(_pack_bf16_pair(o[:, :half], o[:, half:]), jnp.int32)
        if rows < EXPERT_ROWS:
            o_ref[rows:, :] = jnp.zeros((EXPERT_ROWS - rows, half), jnp.int32)

    lower = 0
    for rows in EXPERT_ROW_STEPS:
        pl.when((valid > lower) & (valid <= rows))(functools.partial(run, rows))
        lower = rows

    @pl.when(valid == 0)
    def _():
        o_ref[...] = jnp.zeros_like(o_ref)


def _experts(blk_e, blk_src, blk_valid, blk_first, blk_next, blk_slot, xs, w1, b1, w2, b2):
    cap = xs.shape[0]
    rows = EXPERT_ROWS
    grid_spec = pltpu.PrefetchScalarGridSpec(
        num_scalar_prefetch=6,
        grid=(cap // rows,),
        in_specs=[
            pl.BlockSpec((rows, D_MODEL // 2), lambda b, be, bs, *_: (bs[b], 0)),
            pl.BlockSpec(memory_space=pl.ANY),
            pl.BlockSpec((None, 1, 2 * D_FF), lambda b, be, *_: (be[b], 0, 0)),
            pl.BlockSpec(memory_space=pl.ANY),
            pl.BlockSpec((None, 1, D_MODEL), lambda b, be, *_: (be[b], 0, 0)),
        ],
        out_specs=pl.BlockSpec((rows, D_MODEL // 2), lambda b, *_: (b, 0)),
        scratch_shapes=[
            pltpu.VMEM((2, D_MODEL, 2 * D_FF), F32),
            pltpu.VMEM((2, D_FF, D_MODEL), F32),
            pltpu.SemaphoreType.DMA((2, 2)),
        ],
    )
    return pl.pallas_call(
        _experts_kernel,
        grid_spec=grid_spec,
        out_shape=jax.ShapeDtypeStruct((cap, D_MODEL // 2), jnp.int32),
        compiler_params=_cparams("arbitrary"),
        name="experts",
    )(blk_e, blk_src, blk_valid, blk_first, blk_next, blk_slot, xs, w1, b1, w2, b2)


def _sc_gather(table, idx):
    n = idx.shape[0]
    width = table.shape[1]
    per_worker = n // SC_WORKERS
    assert per_worker * SC_WORKERS == n and per_worker % SC_WINDOW == 0

    @functools.partial(
        pl.kernel,
        mesh=_sc_mesh(),
        out_type=jax.ShapeDtypeStruct((n, width), table.dtype),
        scratch_types=[
            pltpu.VMEM((SC_WINDOW,), jnp.int32),
            pltpu.VMEM((SC_WINDOW, width), table.dtype),
            pltpu.SemaphoreType.DMA,
        ],
        name="sc_gather",
    )
    def gather(table_hbm, idx_hbm, out_hbm, idx_v, rows_v, sem):
        base = _sc_worker() * per_worker

        @pl.loop(0, per_worker // SC_WINDOW)
        def _(step):
            off = pl.multiple_of(base + step * SC_WINDOW, SC_WINDOW)
            pltpu.sync_copy(idx_hbm.at[pl.ds(off, SC_WINDOW)], idx_v)
            pltpu.async_copy(table_hbm.at[idx_v], rows_v, sem).wait()
            pltpu.sync_copy(rows_v, out_hbm.at[pl.ds(off, SC_WINDOW)])

    return gather(table, idx)


def _combine_kernel(gate_ref, x_ref, rows_ref, y_ref):
    gate_t = gate_ref[...].T
    half = D_MODEL // 2
    lo_sum = x_ref[:, :half]
    hi_sum = x_ref[:, half:]
    for k in range(TOP_K):
        packed = lax.bitcast_convert_type(rows_ref[k], jnp.uint32)
        g = gate_t[:, k:k + 1]
        lo_sum = lo_sum + g * lax.bitcast_convert_type(packed << 16, F32)
        hi_sum = hi_sum + g * lax.bitcast_convert_type(packed & jnp.uint32(0xFFFF0000), F32)
    y_ref[:, :half] = lo_sum
    y_ref[:, half:] = hi_sum


def _combine(gate, x2d, rows4):
    t = x2d.shape[0]
    rows = COMBINE_ROWS
    return pl.pallas_call(
        _combine_kernel,
        grid=(t // rows,),
        in_specs=[
            pl.BlockSpec((2 * TOP_K, rows), lambda i: (0, i)),
            pl.BlockSpec((rows, D_MODEL), lambda i: (i, 0)),
            pl.BlockSpec((TOP_K, rows, D_MODEL // 2), lambda i: (0, i, 0)),
        ],
        out_specs=pl.BlockSpec((rows, D_MODEL), lambda i: (i, 0)),
        out_shape=jax.ShapeDtypeStruct((t, D_MODEL), F32),
        compiler_params=_cparams("parallel"),
        name="combine",
    )(gate, x2d, rows4)


def _moe_half(x2d, m):
    t = x2d.shape[0]
    rows = EXPERT_ROWS
    cap = t * TOP_K + N_EXPERTS * rows
    n_blk = cap // rows
    xn, idx, rank, gate, cnt = _router(x2d, m["g2"], m["wr_hi"], m["wr_lo"], m["br"], m["tri"])

    counts = cnt[:, 0].astype(jnp.int32)
    padded = (counts + rows - 1) // rows * rows
    pends = jnp.cumsum(padded)
    pstart = pends - padded
    n_used = pends[-1:] // rows
    blk_src = jnp.minimum(jnp.arange(n_blk, dtype=jnp.int32), n_used - 1)
    starts = (blk_src * rows)[:, None]
    owner = (pstart[None, :] <= starts) & (starts < pends[None, :])
    blk_e = jnp.sum(jnp.where(owner, jnp.arange(N_EXPERTS)[None, :], 0), axis=1).astype(jnp.int32)
    filled_to = jnp.sum(jnp.where(owner, (pstart + counts)[None, :], 0), axis=1)
    blk_valid = jnp.clip(filled_to - blk_src * rows, 0, rows)
    in_use = jnp.arange(n_blk) < n_used
    blk_valid = jnp.where(in_use, blk_valid, 0).astype(jnp.int32)
    expert = jnp.arange(N_EXPERTS)
    has_rows = counts > 0
    later = jnp.where(has_rows[None, :] & (expert[None, :] > expert[:, None]), expert[None, :], N_EXPERTS)
    next_expert = jnp.min(later, axis=1)
    next_expert = jnp.where(next_expert == N_EXPERTS, -1, next_expert)
    expert_slot = (jnp.cumsum(has_rows) - 1) % 2
    blk_first = (in_use & (jnp.sum(jnp.where(owner, pstart[None, :], 0), axis=1) == blk_src * rows)).astype(jnp.int32)
    blk_next = jnp.sum(jnp.where(owner, next_expert[None, :], 0), axis=1).astype(jnp.int32)
    blk_slot = jnp.sum(jnp.where(owner, expert_slot[None, :], 0), axis=1).astype(jnp.int32)

    dest = _dest(pstart.astype(jnp.int32), idx, rank)
    xs = _sc_scatter(xn, dest, cap)
    out_sorted = _experts(blk_e, blk_src, blk_valid, blk_first, blk_next, blk_slot, xs, m["w1"], m["b1"], m["w2"],
                          m["b2"])
    rows4 = _sc_gather(out_sorted, dest.reshape(TOP_K * t))
    return _combine(gate, x2d, rows4.reshape(TOP_K, t, D_MODEL // 2))


def _prep_moe(norm2_g, w_router, b_router, w_moe_in, b_moe_in, w_moe_out, b_moe_out):
    r = jnp.arange(ROUTER_ROWS)
    wr_hi = w_router.T.astype(BF16)
    return dict(
        g2=norm2_g.reshape(1, D_MODEL),
        wr_hi=wr_hi,
        wr_lo=(w_router.T - wr_hi.astype(F32)).astype(BF16),
        br=b_router.reshape(N_EXPERTS, 1),
        tri=(r[:, None] < r[None, :]).astype(BF16),
        w1=w_moe_in,
        b1=b_moe_in.reshape(N_EXPERTS, 1, 2 * D_FF),
        w2=w_moe_out,
        b2=b_moe_out.reshape(N_EXPERTS, 1, D_MODEL),
    )


def kernel(x_prompt, x_sample, norm1_g, w_in, q_norm_g, k_norm_g, attn_sink, sgu_ln_g, sgu_ln_b, w_spatial,
           b_spatial, attn_out_g, sgu_out_g, w_out, norm2_g, w_router, b_router, w_moe_in, b_moe_in, w_moe_out,
           b_moe_out):
    p = _prep_params(norm1_g[0], w_in[0], q_norm_g[0], k_norm_g[0], attn_sink[0], sgu_ln_g[0], sgu_ln_b[0],
                     w_spatial[0], b_spatial[0], attn_out_g[0], sgu_out_g[0], w_out[0])
    m = _prep_moe(norm2_g[0], w_router[0], b_router[0], w_moe_in[0], b_moe_in[0], w_moe_out[0], b_moe_out[0])
    outs = []
    for x in (x_prompt, x_sample):
        x2 = _mix_half(x, p)
        outs.append(_moe_half(x2, m).reshape(x.shape))
    return tuple(outs)
```
